```python
import math
import jax, jax.numpy as jnp
from jax import lax
import numpy as np

D_MODEL = 1024
BATCH = 8
SEQ = 2048
DEPTH = 4
DEC_BATCH = 32
DEC_SEQ = 4
PAST_LEN = 16384
PAGE_SIZE = 128

NSA_HEADS = 16
NSA_KV_HEADS = 2
NSA_GROUP = NSA_HEADS // NSA_KV_HEADS
NSA_DIM = D_MODEL // NSA_HEADS
CMP_BLOCK = 32
SLC_BLOCK = 64
N_SELECT = 16
WINDOW = 512
MLA_HEADS = 8
MLA_NOPE = 128
MLA_ROPE = 64
MLA_V = D_MODEL // MLA_HEADS
Q_LORA = 384
KV_LORA = 256
ROPE_THETA = 10000.0
N_BUCKETS = 32
MAX_DISTANCE = 128
D_FF = ((8 * D_MODEL // 3 + 255) // 256) * 256
Q_BLOCK = 128
SLC_Q_BLOCK = 32
NSA_SCALE = NSA_DIM ** -0.5
MLA_SCALE = (MLA_NOPE + MLA_ROPE) ** -0.5
NEG_BIG = -1e30
RMS_EPS = 1e-6
IN_SIZES = (NSA_HEADS * NSA_DIM,
            2 * NSA_KV_HEADS * NSA_DIM,
            2 * NSA_KV_HEADS * NSA_DIM,
            2 * NSA_KV_HEADS * NSA_DIM,
            3 * NSA_HEADS,
            Q_LORA,
            KV_LORA,
            MLA_ROPE,
            2 * D_MODEL)
D_IN = sum(IN_SIZES)

kernel_name = 'hybrid_nsa_mla_adaln_decoder_step'


def rms_norm(x, g):
    xf = x.astype(jnp.float32)
    y = xf * lax.rsqrt(jnp.mean(xf * xf, axis=-1, keepdims=True) + RMS_EPS)
    return (y * g.astype(jnp.float32)).astype(x.dtype)


def masked_softmax(logits, mask, axis):
    z = jnp.where(mask, logits.astype(jnp.float32), NEG_BIG)
    e = jnp.exp(z - jnp.max(z, axis=axis, keepdims=True)) * mask
    return e / jnp.maximum(jnp.sum(e, axis=axis, keepdims=True), 1e-30)


def t5_bucket(dist):
    max_exact = N_BUCKETS // 2
    d = jnp.maximum(dist, 0)
    log_ratio = jnp.log(jnp.maximum(d, 1).astype(jnp.float32) / max_exact) / math.log(MAX_DISTANCE / max_exact)
    large = jnp.minimum(max_exact + (log_ratio * (N_BUCKETS - max_exact)).astype(jnp.int32), N_BUCKETS - 1)
    return jnp.where(d < max_exact, d, large)


def rope(x, pos):
    half = MLA_ROPE // 2
    inv = ROPE_THETA ** (-jnp.arange(half, dtype=jnp.float32) / half)
    ang = pos.astype(jnp.float32)[:, None] * inv[None, :]
    ang = ang.reshape(ang.shape[0], *([1] * (x.ndim - 3)), half)
    cos, sin = jnp.cos(ang), jnp.sin(ang)
    xf = x.astype(jnp.float32)
    x1, x2 = xf[..., :half], xf[..., half:]
    return jnp.concatenate([x1 * cos - x2 * sin, x1 * sin + x2 * cos], axis=-1).astype(x.dtype)


def sweep_queries(fn, block, q_pos, *qs):
    nb = q_pos.shape[0] // block
    def to_blocks(a):
        return jnp.moveaxis(a.reshape(a.shape[0], nb, block, *a.shape[2:]), 1, 0)
    xs = (q_pos.reshape(nb, block),) + tuple(to_blocks(a) for a in qs)
    out = jnp.moveaxis(lax.map(lambda args: fn(*args), xs), 0, 1)
    return out.reshape(out.shape[0], nb * block, *out.shape[3:])


def compress_blocks(rows, pe, w1, b1, w2):
    B, S = rows.shape[:2]
    n = S // CMP_BLOCK
    blk = rows[:, :n * CMP_BLOCK].reshape(B, n, CMP_BLOCK, NSA_KV_HEADS, NSA_DIM) + pe[:, None, :]
    flat = blk.transpose(0, 1, 3, 2, 4).reshape(B, n, NSA_KV_HEADS, CMP_BLOCK * NSA_DIM)
    return jax.nn.gelu(flat @ w1 + b1) @ w2


def cmp_attention(q, kc, vc, q_pos, table):
    B, T = q.shape[:2]
    N = kc.shape[1]
    qg = q.reshape(B, T, NSA_KV_HEADS, NSA_GROUP, NSA_DIM)
    end = (jnp.arange(N, dtype=jnp.int32) + 1) * CMP_BLOCK - 1
    dist = q_pos[:, None] - end[None, :]
    bias = table[t5_bucket(dist)].reshape(T, N, NSA_KV_HEADS, NSA_GROUP).transpose(0, 2, 3, 1)
    logits = jnp.einsum('btghd,bngd->btghn', qg, kc) * NSA_SCALE + bias
    p = masked_softmax(logits, (dist >= 0)[None, :, None, None, :], axis=-1)
    o = jnp.einsum('btghn,bngd->btghd', p.astype(vc.dtype), vc)
    return o.reshape(B, T, NSA_HEADS, NSA_DIM), p


def select_blocks(imp, q_pos, n_slc):
    j = jnp.arange(n_slc, dtype=jnp.int32)[None, :]
    cur = (q_pos // SLC_BLOCK)[:, None]
    valid = j <= cur
    forced = (j == 0) | (j == cur) | (j == cur - 1)
    score = jnp.where(forced[None, :, None, :], jnp.inf, jnp.where(valid[None, :, None, :], imp, -jnp.inf))
    _, idx = lax.top_k(score, min(N_SELECT, n_slc))
    return idx


def slc_attention(q, idx, kb, vb, q_pos, table):
    B, T = q.shape[:2]
    gather = jax.vmap(jax.vmap(lambda blocks, ix: blocks[ix]))
    idx_g = idx.transpose(0, 2, 1, 3)
    k_sel = gather(kb.transpose(0, 3, 1, 2, 4), idx_g)
    v_sel = gather(vb.transpose(0, 3, 1, 2, 4), idx_g)
    k_pos = idx_g[..., None] * SLC_BLOCK + jnp.arange(SLC_BLOCK, dtype=jnp.int32)
    dist = q_pos[None, None, :, None, None] - k_pos
    bias = table.reshape(N_BUCKETS, NSA_KV_HEADS, NSA_GROUP)[t5_bucket(dist), jnp.arange(NSA_KV_HEADS)[None, :, None, None, None]]
    bias = jnp.moveaxis(bias, -1, 3)
    qg = q.reshape(B, T, NSA_KV_HEADS, NSA_GROUP, NSA_DIM).transpose(0, 2, 1, 3, 4)
    logits = jnp.einsum('bgthd,bgtnkd->bgthnk', qg, k_sel) * NSA_SCALE + bias
    p = masked_softmax(logits, (dist >= 0)[:, :, :, None], axis=(-2, -1))
    o = jnp.einsum('bgthnk,bgtnkd->bgthd', p.astype(v_sel.dtype), v_sel)
    return o.transpose(0, 2, 1, 3, 4).reshape(B, T, NSA_HEADS, NSA_DIM)


def win_attention(q, k, v, q_pos, k_pos, table):
    B, T = q.shape[:2]
    S = k.shape[1]
    qg = q.reshape(B, T, NSA_KV_HEADS, NSA_GROUP, NSA_DIM)
    dist = q_pos[:, None] - k_pos[None, :]
    mask = (dist >= 0) & (dist < WINDOW) & (k_pos[None, :] >= 0)
    bias = table[t5_bucket(dist)].reshape(T, S, NSA_KV_HEADS, NSA_GROUP).transpose(0, 2, 3, 1)
    logits = jnp.einsum('btghd,bsgd->btghs', qg, k) * NSA_SCALE + bias
    p = masked_softmax(logits, mask[None, :, None, None, :], axis=-1)
    o = jnp.einsum('btghs,bsgd->btghd', p.astype(v.dtype), v)
    return o.reshape(B, T, NSA_HEADS, NSA_DIM)


def mla_attention(q_lat, q_rope, ckv, k_rope, q_pos, k_pos):
    logits = (jnp.einsum('bthc,bsc->bhts', q_lat, ckv) + jnp.einsum('bthr,bsr->bhts', q_rope, k_rope)) * MLA_SCALE
    mask = k_pos[None, :] <= q_pos[:, None]
    p = masked_softmax(logits, mask[None, None], axis=-1)
    return jnp.einsum('bhts,bsc->bthc', p.astype(ckv.dtype), ckv)


def mixer(h, q_pos, l, P, past):
    B, T, _ = h.shape
    splits = np.cumsum(IN_SIZES)[:-1].tolist()
    q, kv_cmp, kv_slc, kv_win, g_nsa, c_q, c_kv, k_rope, g_merge = jnp.split(h @ P['w_in'][l], splits, axis=-1)
    kv_shape = (B, T, 2, NSA_KV_HEADS, NSA_DIM)
    q = q.reshape(B, T, NSA_HEADS, NSA_DIM)
    kv_cmp = kv_cmp.reshape(kv_shape)
    kv_slc = kv_slc.reshape(kv_shape)
    kv_win = kv_win.reshape(kv_shape)
    latent = jnp.concatenate([rms_norm(c_kv, P['mla_kv_norm_g'][l]), rope(k_rope, q_pos)], axis=-1)
    table = P['rel_bias_table']
    if past is None:
        cmp_all, slc_all, lat_all = kv_cmp, kv_slc, latent
    else:
        past_cmp, past_slc, past_lat, win_buf = past
        cmp_all = jnp.concatenate([past_cmp, kv_cmp], axis=1)
        slc_all = jnp.concatenate([past_slc, kv_slc], axis=1)
        lat_all = jnp.concatenate([past_lat, latent], axis=1)
    S = cmp_all.shape[1]
    k_pos = jnp.arange(S, dtype=jnp.int32)

    def phi(rows, i):
        return compress_blocks(rows, P['nsa_cmp_pe'][l, i], P['nsa_cmp_w1'][l, i], P['nsa_cmp_b1'][l, i], P['nsa_cmp_w2'][l, i])
    o_cmp, p_cmp = cmp_attention(q, phi(cmp_all[:, :, 0], 0), phi(cmp_all[:, :, 1], 1), q_pos, table)

    n_slc = -(-S // SLC_BLOCK)
    ratio = SLC_BLOCK // CMP_BLOCK
    imp = p_cmp.sum(axis=3)
    imp = jnp.pad(imp, ((0, 0), (0, 0), (0, 0), (0, n_slc * ratio - imp.shape[-1])))
    imp = imp.reshape(B, T, NSA_KV_HEADS, n_slc, ratio).sum(axis=-1)
    idx = select_blocks(imp, q_pos, n_slc)
    slc_pad = jnp.pad(slc_all, ((0, 0), (0, n_slc * SLC_BLOCK - S), (0, 0), (0, 0), (0, 0)))
    kb = slc_pad[:, :, 0].reshape(B, n_slc, SLC_BLOCK, NSA_KV_HEADS, NSA_DIM)
    vb = slc_pad[:, :, 1].reshape(B, n_slc, SLC_BLOCK, NSA_KV_HEADS, NSA_DIM)
    slc_fn = lambda pos, qb, ib: slc_attention(qb, ib, kb, vb, pos, table)
    o_slc = sweep_queries(slc_fn, SLC_Q_BLOCK, q_pos, q, idx) if past is None else slc_fn(q_pos, q, idx)

    if past is None:
        nqb = T // Q_BLOCK
        band_idx = jnp.arange(nqb, dtype=jnp.int32)[:, None] * Q_BLOCK + jnp.arange(WINDOW + Q_BLOCK, dtype=jnp.int32)[None, :]
        band = jnp.pad(kv_win, ((0, 0), (WINDOW, 0), (0, 0), (0, 0), (0, 0)))[:, band_idx]
        win_fn = lambda qb, kk, vv, qp, kp: win_attention(qb, kk, vv, qp, kp, table)
        o_win = jax.vmap(win_fn, in_axes=(1, 1, 1, 0, 0), out_axes=1)(
            q.reshape(B, nqb, Q_BLOCK, NSA_HEADS, NSA_DIM), band[:, :, :, 0], band[:, :, :, 1],
            q_pos.reshape(nqb, Q_BLOCK), band_idx - WINDOW).reshape(B, T, NSA_HEADS, NSA_DIM)
        win_state = kv_win[:, T - min(WINDOW, T):]
    else:
        win_all = jnp.concatenate([win_buf, kv_win], axis=1)
        W = win_all.shape[1]
        o_win = win_attention(q, win_all[:, :, 0], win_all[:, :, 1], q_pos, jnp.arange(S - W, S, dtype=jnp.int32), table)
        win_state = win_all[:, T:]

    g = jax.nn.sigmoid(g_nsa).reshape(B, T, NSA_HEADS, 3)
    o_nsa = (g[..., 0:1] * o_cmp + g[..., 1:2] * o_slc + g[..., 2:3] * o_win).reshape(B, T, D_MODEL)

    qm = (rms_norm(c_q, P['mla_q_norm_g'][l]) @ P['mla_w_uq'][l]).reshape(B, T, MLA_HEADS, MLA_NOPE + MLA_ROPE)
    q_lat = jnp.einsum('bthn,chn->bthc', qm[..., :MLA_NOPE], P['mla_w_uk'][l])
    q_rope = rope(qm[..., MLA_NOPE:], q_pos)
    ckv_all, kr_all = lat_all[..., :KV_LORA], lat_all[..., KV_LORA:]
    mla_fn = lambda pos, ql, qr: mla_attention(ql, qr, ckv_all, kr_all, pos, k_pos)
    o_lat = sweep_queries(mla_fn, Q_BLOCK, q_pos, q_lat, q_rope) if past is None else mla_fn(q_pos, q_lat, q_rope)
    o_mla = jnp.einsum('bthc,chd->bthd', o_lat, P['mla_w_uv'][l]).reshape(B, T, D_MODEL)

    g_a, g_b = jnp.split(jax.nn.sigmoid(g_merge), 2, axis=-1)
    y = (g_a * o_nsa + g_b * o_mla) @ P['w_out'][l]
    return y, (kv_cmp, kv_slc, latent, win_state)


def swiglu(h, w13, w2):
    a, b = jnp.split(h @ w13, 2, axis=-1)
    return (jax.nn.silu(a) * b) @ w2


def trunk(x, c, q_pos, P, past_fn):
    cond = jax.nn.silu(c)
    states = []
    for l in range(DEPTH):
        mod = (cond @ P['ada_w'][l] + P['ada_b'][l])[:, None, :]
        sh1, sc1, gt1, sh2, sc2, gt2 = jnp.split(mod, 6, axis=-1)
        h = rms_norm(x, P['norm_mix_g'][l]) * (1 + sc1) + sh1
        y, st = mixer(h, q_pos, l, P, None if past_fn is None else past_fn(l))
        x = x + gt1 * y
        h = rms_norm(x, P['norm_ffn_g'][l]) * (1 + sc2) + sh2
        x = x + gt2 * swiglu(h, P['ffn_w13'][l], P['ffn_w2'][l])
        states.append(st)
    new_state = [jnp.stack([s[i] for s in states]) for i in range(4)]
    return rms_norm(x, P['final_norm_g']), new_state


def setup_inputs(seed: int = 0) -> dict:
    key = jax.random.key(seed)
    ks = jax.random.split(key, 28)
    n_pages = PAST_LEN // PAGE_SIZE
    n_pool = (5 * DEC_BATCH * n_pages) // 4
    win_buf = min(WINDOW, PAST_LEN)
    kv_row = (2, NSA_KV_HEADS, NSA_DIM)

    def nrm(k, shape, scale=1.0):
        return scale * jax.random.normal(k, shape, jnp.float32)

    def gain(k, shape):
        return 1.0 + nrm(k, shape, 0.02)

    page_table = jax.random.permutation(ks[6], n_pool)[: DEC_BATCH * n_pages].reshape(DEC_BATCH, n_pages).astype(jnp.int32)
    return {
        'x_prompt': nrm(ks[0], (BATCH, SEQ, D_MODEL)),
        'x_sample': nrm(ks[1], (DEC_BATCH, DEC_SEQ, D_MODEL)),
        'cache_cmp_kv': nrm(ks[2], (DEPTH, n_pool, PAGE_SIZE) + kv_row),
        'cache_slc_kv': nrm(ks[3], (DEPTH, n_pool, PAGE_SIZE) + kv_row),
        'cache_mla': nrm(ks[4], (DEPTH, n_pool, PAGE_SIZE, KV_LORA + MLA_ROPE)),
        'state_win_kv': nrm(ks[5], (DEPTH, DEC_BATCH, win_buf) + kv_row),
        'page_table': page_table,
        'c_prompt': nrm(ks[7], (BATCH, D_MODEL)),
        'c_sample': nrm(ks[8], (DEC_BATCH, D_MODEL)),
        'rel_bias_table': nrm(ks[9], (N_BUCKETS, NSA_HEADS), 0.5),
        'ada_w': nrm(ks[10], (DEPTH, D_MODEL, 6 * D_MODEL), 0.5 * D_MODEL ** -0.5),
        'ada_b': nrm(ks[11], (DEPTH, 6 * D_MODEL), 0.01),
        'norm_mix_g': gain(ks[12], (DEPTH, D_MODEL)),
        'w_in': nrm(ks[13], (DEPTH, D_MODEL, D_IN), D_MODEL ** -0.5),
        'nsa_cmp_pe': nrm(ks[14], (DEPTH, 2, CMP_BLOCK, NSA_DIM), 0.1),
        'nsa_cmp_w1': nrm(ks[15], (DEPTH, 2, CMP_BLOCK * NSA_DIM, NSA_DIM), (CMP_BLOCK * NSA_DIM) ** -0.5),
        'nsa_cmp_b1': nrm(ks[16], (DEPTH, 2, NSA_DIM), 0.01),
        'nsa_cmp_w2': nrm(ks[17], (DEPTH, 2, NSA_DIM, NSA_DIM), NSA_DIM ** -0.5),
        'mla_q_norm_g': gain(ks[18], (DEPTH, Q_LORA)),
        'mla_w_uq': nrm(ks[19], (DEPTH, Q_LORA, MLA_HEADS * (MLA_NOPE + MLA_ROPE)), Q_LORA ** -0.5),
        'mla_kv_norm_g': gain(ks[20], (DEPTH, KV_LORA)),
        'mla_w_uk': nrm(ks[21], (DEPTH, KV_LORA, MLA_HEADS, MLA_NOPE), KV_LORA ** -0.5),
        'mla_w_uv': nrm(ks[22], (DEPTH, KV_LORA, MLA_HEADS, MLA_V), KV_LORA ** -0.5),
        'w_out': nrm(ks[23], (DEPTH, D_MODEL, D_MODEL), D_MODEL ** -0.5),
        'norm_ffn_g': gain(ks[24], (DEPTH, D_MODEL)),
        'ffn_w13': nrm(ks[25], (DEPTH, D_MODEL, 2 * D_FF), D_MODEL ** -0.5),
        'ffn_w2': nrm(ks[26], (DEPTH, D_FF, D_MODEL), D_FF ** -0.5),
        'final_norm_g': gain(ks[27], (D_MODEL,)),
    }


def reference(x_prompt, x_sample, cache_cmp_kv, cache_slc_kv, cache_mla, state_win_kv, page_table,
              c_prompt, c_sample, rel_bias_table, ada_w, ada_b, norm_mix_g, w_in, nsa_cmp_pe, nsa_cmp_w1,
              nsa_cmp_b1, nsa_cmp_w2, mla_q_norm_g, mla_w_uq, mla_kv_norm_g, mla_w_uk, mla_w_uv, w_out,
              norm_ffn_g, ffn_w13, ffn_w2, final_norm_g):
    P = dict(rel_bias_table=rel_bias_table, ada_w=ada_w, ada_b=ada_b, norm_mix_g=norm_mix_g, w_in=w_in,
             nsa_cmp_pe=nsa_cmp_pe, nsa_cmp_w1=nsa_cmp_w1, nsa_cmp_b1=nsa_cmp_b1, nsa_cmp_w2=nsa_cmp_w2,
             mla_q_norm_g=mla_q_norm_g, mla_w_uq=mla_w_uq, mla_kv_norm_g=mla_kv_norm_g, mla_w_uk=mla_w_uk,
             mla_w_uv=mla_w_uv, w_out=w_out, norm_ffn_g=norm_ffn_g, ffn_w13=ffn_w13, ffn_w2=ffn_w2,
             final_norm_g=final_norm_g)

    pos_p = jnp.arange(x_prompt.shape[1], dtype=jnp.int32)
    y_prompt, st_p = trunk(x_prompt, c_prompt, pos_p, P, None)

    n_seq = page_table.shape[0]

    def past_fn(l):
        def gather(cache):
            return cache[l][page_table].reshape(n_seq, -1, *cache.shape[3:])
        return (gather(cache_cmp_kv), gather(cache_slc_kv), gather(cache_mla), state_win_kv[l])

    pos_s = PAST_LEN + jnp.arange(x_sample.shape[1], dtype=jnp.int32)
    y_sample, st_s = trunk(x_sample, c_sample, pos_s, P, past_fn)
    return (y_prompt, y_sample, st_p[0], st_p[1], st_p[2], st_p[3], st_s[0], st_s[1], st_s[2], st_s[3])
```

```python
import functools
import math

import numpy as np
import jax
import jax.numpy as jnp
from jax import lax
from jax.experimental import pallas as pl
from jax.experimental.pallas import tpu as pltpu

F32 = jnp.float32
BF16 = jnp.bfloat16

D_MODEL = 1024
PAGE_SIZE = 128
NSA_HEADS = 16
NSA_KV_HEADS = 2
NSA_GROUP = NSA_HEADS // NSA_KV_HEADS
NSA_DIM = D_MODEL // NSA_HEADS
CMP_BLOCK = 32
SLC_BLOCK = 64
N_SELECT = 16
WINDOW = 512
MLA_HEADS = 8
MLA_NOPE = 128
MLA_ROPE = 64
MLA_V = D_MODEL // MLA_HEADS
Q_LORA = 384
KV_LORA = 256
LAT = KV_LORA + MLA_ROPE
ROPE_THETA = 10000.0
N_BUCKETS = 32
MAX_DISTANCE = 128
NSA_SCALE = NSA_DIM ** -0.5
MLA_SCALE = (MLA_NOPE + MLA_ROPE) ** -0.5
NEG_BIG = -1e30
RMS_EPS = 1e-6
KV_ROW = 2 * NSA_KV_HEADS * NSA_DIM
GATE_LANES = 128

SEG_Q = (0, 1024)
SEG_CMP = (1024, 1280)
SEG_SLC = (1280, 1536)
SEG_WIN = (1536, 1792)
SEG_GATE = (1792, 2048)
SEG_CQ = (2048, 2432)
SEG_CKV = (2432, 2688)
SEG_KR = (2688, 2816)
SEG_GM = (2816, 4864)
D_IN_PACKED = 4864

ATT_TILE = 256
VMEM_LIMIT = 56 * 1024 * 1024
CMP_PAGES = 32
MLA_PAGES = 16


def _cparams(sem):
    return pltpu.CompilerParams(dimension_semantics=sem, vmem_limit_bytes=VMEM_LIMIT)


def _dot(a, b):
    return jnp.dot(a, b, preferred_element_type=F32)


def _dot_nt(a, b):
    return lax.dot_general(a, b, (((1,), (1,)), ((), ())), preferred_element_type=F32)


def _rms(x, g):
    return x * lax.rsqrt(jnp.mean(x * x, axis=-1, keepdims=True) + RMS_EPS) * g


def _rope_lanes(x, cos2, sin2):
    w = x.shape[-1]
    lane = lax.broadcasted_iota(jnp.int32, x.shape, 1)
    swapped = jnp.where(lane % MLA_ROPE < MLA_ROPE // 2, pltpu.roll(x, w - MLA_ROPE // 2, 1),
                        pltpu.roll(x, MLA_ROPE // 2, 1))
    return x * cos2 + swapped * sin2


def _t5_bucket_np(dist):
    max_exact = N_BUCKETS // 2
    d = np.maximum(dist, 0)
    log_ratio = np.log(np.maximum(d, 1).astype(np.float32) / max_exact) / math.log(MAX_DISTANCE / max_exact)
    large = np.minimum(max_exact + (log_ratio * (N_BUCKETS - max_exact)).astype(np.int32), N_BUCKETS - 1)
    return np.where(d < max_exact, d, large).astype(np.int32)


def _rope_tables(pos, reps):
    half = MLA_ROPE // 2
    inv = ROPE_THETA ** (-jnp.arange(half, dtype=F32) / half)
    ang = pos.astype(F32)[:, None] * inv[None, :]
    cos, sin = jnp.cos(ang), jnp.sin(ang)
    cos2 = jnp.concatenate([cos, cos], axis=-1)
    sin2 = jnp.concatenate([-sin, sin], axis=-1)
    return jnp.tile(cos2, (1, reps)), jnp.tile(sin2, (1, reps))


def _bias_kernel(table_ref, bucket_ref, mask_ref, o_ref, *, shift):
    h = pl.program_id(0)
    bucket = bucket_ref[...]
    acc = jnp.zeros(bucket.shape, F32)
    for b in range(N_BUCKETS):
        acc = jnp.where(bucket == b, table_ref[b, h], acc)
    if shift:
        acc = acc - table_ref[N_BUCKETS - 1, h]
    o_ref[0] = acc + mask_ref[...]


def _expand_bias(table, bucket, addmask, shift):
    r, c = bucket.shape
    return pl.pallas_call(
        functools.partial(_bias_kernel, shift=shift),
        grid=(NSA_HEADS,),
        in_specs=[pl.BlockSpec(memory_space=pltpu.SMEM),
                  pl.BlockSpec((r, c), lambda h: (0, 0)),
                  pl.BlockSpec((r, c), lambda h: (0, 0))],
        out_specs=pl.BlockSpec((1, r, c), lambda h: (h, 0, 0)),
        out_shape=jax.ShapeDtypeStruct((NSA_HEADS, r, c), F32),
        compiler_params=_cparams(("arbitrary",)),
        name="bias_expand",
    )(table, jnp.asarray(bucket), jnp.asarray(addmask))


def _mod_kernel(c_ref, w_ref, b_ref, o_ref):
    c = c_ref[...]
    cond = (c * jax.nn.sigmoid(c)).astype(BF16)
    o_ref[0] = _dot(cond, w_ref[0].astype(BF16)) + b_ref[0]


def _modulation(c_all, ada_w, ada_b):
    depth, d, n = ada_w.shape
    rows = c_all.shape[0]
    tn = 1536
    return pl.pallas_call(
        _mod_kernel,
        grid=(depth, n // tn),
        in_specs=[pl.BlockSpec((rows, d), lambda l, j: (0, 0)),
                  pl.BlockSpec((1, d, tn), lambda l, j: (l, 0, j)),
                  pl.BlockSpec((1, 1, tn), lambda l, j: (l, 0, j))],
        out_specs=pl.BlockSpec((1, rows, tn), lambda l, j: (l, 0, j)),
        out_shape=jax.ShapeDtypeStruct((depth, rows, n), F32),
        compiler_params=_cparams(("arbitrary", "arbitrary")),
        name="adaln_mod",
    )(c_all, ada_w, ada_b.reshape(depth, 1, n))


def _inproj_kernel(x_ref, mod_ref, gn_ref, w_ref, cos_ref, sin_ref, gq_ref, wuq_ref, wuk_ref, gkv_ref,
                   q_ref, kvc_ref, kvs_ref, kvw_ref, ksk_ref, ksv_ref, kwk_ref, kwv_ref, gate_ref,
                   qmla_ref, lat_ref, latb_ref, gm_ref):
    x = x_ref[...]
    y = _rms(x, gn_ref[...])
    h = (y * (1.0 + mod_ref[:, 1, :]) + mod_ref[:, 0, :]).astype(BF16)

    def seg(s):
        return _dot(h, w_ref[:, s[0]:s[1]])

    q_ref[...] = (seg(SEG_Q) * NSA_SCALE).astype(BF16)
    kvc_ref[...] = seg(SEG_CMP)
    half = NSA_KV_HEADS * NSA_DIM
    for s, kv_ref, k_ref, v_ref in ((SEG_SLC, kvs_ref, ksk_ref, ksv_ref), (SEG_WIN, kvw_ref, kwk_ref, kwv_ref)):
        kv = seg(s)
        kv_ref[...] = kv
        for g in range(NSA_KV_HEADS):
            k_ref[0, g] = kv[:, g * NSA_DIM:(g + 1) * NSA_DIM].astype(BF16)
            v_ref[0, g] = kv[:, half + g * NSA_DIM:half + (g + 1) * NSA_DIM].astype(BF16)
    gate_ref[...] = jax.nn.sigmoid(seg(SEG_GATE))
    gm_ref[...] = jax.nn.sigmoid(seg(SEG_GM))

    cos = cos_ref[...]
    sin = sin_ref[...]
    cqn = _rms(seg(SEG_CQ), gq_ref[...]).astype(BF16)
    qm = _dot(cqn, wuq_ref[...])
    nope_w = MLA_HEADS * MLA_NOPE
    qr = _rope_lanes(qm[:, nope_w:], cos, sin) * MLA_SCALE
    for hd in range(MLA_HEADS):
        qn = qm[:, hd * MLA_NOPE:(hd + 1) * MLA_NOPE].astype(BF16)
        qmla_ref[0, hd, :, 0:KV_LORA] = (_dot(qn, wuk_ref[hd]) * MLA_SCALE).astype(BF16)
        qmla_ref[0, hd, :, KV_LORA:LAT] = qr[:, hd * MLA_ROPE:(hd + 1) * MLA_ROPE].astype(BF16)
    ckv = _rms(seg(SEG_CKV), gkv_ref[...])
    kr = _rope_lanes(seg(SEG_KR), cos[:, 0:128], sin[:, 0:128])[:, 0:MLA_ROPE]
    lat_ref[:, 0:KV_LORA] = ckv
    lat_ref[:, KV_LORA:LAT] = kr
    latb_ref[:, 0:KV_LORA] = ckv.astype(BF16)
    latb_ref[:, KV_LORA:LAT] = kr.astype(BF16)


def _inproj(x, mod, per_row_mod, lw, cos_t, sin_t, nb, nt, tm):
    m = nb * nt
    tpb = nt // tm
    d = D_MODEL
    if per_row_mod:
        mod_spec = pl.BlockSpec((tm, 6, d), lambda i: (i, 0, 0))
    else:
        mod_spec = pl.BlockSpec((1, 6, d), lambda i: (i // tpb, 0, 0))
    const2 = lambda i: (0, 0)
    row = lambda i: (i, 0)
    bt = lambda i: (i // tpb, 0, i % tpb, 0)
    g = NSA_KV_HEADS
    out_shape = (
        jax.ShapeDtypeStruct((m, d), BF16),
        jax.ShapeDtypeStruct((m, KV_ROW), F32),
        jax.ShapeDtypeStruct((m, KV_ROW), F32),
        jax.ShapeDtypeStruct((m, KV_ROW), F32),
        jax.ShapeDtypeStruct((nb, g, nt, NSA_DIM), BF16),
        jax.ShapeDtypeStruct((nb, g, nt, NSA_DIM), BF16),
        jax.ShapeDtypeStruct((nb, g, nt, NSA_DIM), BF16),
        jax.ShapeDtypeStruct((nb, g, nt, NSA_DIM), BF16),
        jax.ShapeDtypeStruct((m, g * GATE_LANES), F32),
        jax.ShapeDtypeStruct((nb, MLA_HEADS, nt, LAT), BF16),
        jax.ShapeDtypeStruct((m, LAT), F32),
        jax.ShapeDtypeStruct((m, LAT), BF16),
        jax.ShapeDtypeStruct((m, 2 * d), F32),
    )
    kv_spec = pl.BlockSpec((1, g, tm, NSA_DIM), bt)
    out_specs = (
        pl.BlockSpec((tm, d), row),
        pl.BlockSpec((tm, KV_ROW), row), pl.BlockSpec((tm, KV_ROW), row), pl.BlockSpec((tm, KV_ROW), row),
        kv_spec, kv_spec, kv_spec, kv_spec,
        pl.BlockSpec((tm, g * GATE_LANES), row),
        pl.BlockSpec((1, MLA_HEADS, tm, LAT), bt),
        pl.BlockSpec((tm, LAT), row), pl.BlockSpec((tm, LAT), row),
        pl.BlockSpec((tm, 2 * d), row),
    )
    in_specs = [
        pl.BlockSpec((tm, d), row), mod_spec, pl.BlockSpec((1, d), const2),
        pl.BlockSpec((d, D_IN_PACKED), const2),
        pl.BlockSpec((tm, MLA_HEADS * MLA_ROPE), lambda i: (i % tpb, 0)),
        pl.BlockSpec((tm, MLA_HEADS * MLA_ROPE), lambda i: (i % tpb, 0)),
        pl.BlockSpec((1, Q_LORA), const2),
        pl.BlockSpec((Q_LORA, MLA_HEADS * (MLA_NOPE + MLA_ROPE)), const2),
        pl.BlockSpec((MLA_HEADS, MLA_NOPE, KV_LORA), lambda i: (0, 0, 0)),
        pl.BlockSpec((1, KV_LORA), const2),
    ]
    return pl.pallas_call(
        _inproj_kernel, grid=(m // tm,), in_specs=in_specs, out_specs=out_specs, out_shape=out_shape,
        compiler_params=_cparams(("arbitrary",)), name="in_proj",
    )(x, mod, lw["norm_mix_g"], lw["w_in"], cos_t, sin_t, lw["mla_q_norm_g"], lw["w_uq"], lw["w_ukT"],
      lw["mla_kv_norm_g"])


def _compress_kernel(tab_ref, *refs, pages):
    del tab_ref
    page_refs = refs[:pages]
    pe_ref, w1_ref, b1_ref, w2_ref, o_ref, slabk_ref, slabv_ref = refs[pages:]
    half = KV_ROW // 2
    for k in range(pages):
        slabk_ref[k * PAGE_SIZE:(k + 1) * PAGE_SIZE, :] = page_refs[k][0, :, 0:half]
        slabv_ref[k * PAGE_SIZE:(k + 1) * PAGE_SIZE, :] = page_refs[k][0, :, half:KV_ROW]
    per_page = PAGE_SIZE // CMP_BLOCK
    acc = jnp.zeros((per_page * pages, KV_ROW), F32)
    for r in range(CMP_BLOCK):
        rows = [jnp.concatenate([slab[pl.ds(c * CMP_BLOCK + r, pages, stride=PAGE_SIZE), :]
                                 for slab in (slabk_ref, slabv_ref)], axis=1) for c in range(per_page)]
        xr = (jnp.concatenate(rows, axis=0) + pe_ref[r]).astype(BF16)
        acc = acc + _dot(xr, w1_ref[r])
    z = acc + b1_ref[...]
    hmid = 0.5 * z * (1.0 + jnp.tanh(math.sqrt(2.0 / math.pi) * (z + 0.044715 * (z * z * z))))
    o_ref[0] = _dot(hmid.astype(BF16), w2_ref[...])


def _compress(rows3d, page_ids, lw):
    n_logical = page_ids.shape[0]
    pages = min(CMP_PAGES, n_logical)
    assert n_logical % pages == 0
    steps = n_logical // pages
    per_page = PAGE_SIZE // CMP_BLOCK

    def page_map(k):
        return lambda s, tab: (tab[s * pages + k], 0, 0)

    in_specs = [pl.BlockSpec((1, PAGE_SIZE, KV_ROW), page_map(k)) for k in range(pages)]
    in_specs += [
        pl.BlockSpec((CMP_BLOCK, 1, KV_ROW), lambda s, tab: (0, 0, 0)),
        pl.BlockSpec((CMP_BLOCK, KV_ROW, KV_ROW), lambda s, tab: (0, 0, 0)),
        pl.BlockSpec((1, KV_ROW), lambda s, tab: (0, 0)),
        pl.BlockSpec((KV_ROW, KV_ROW), lambda s, tab: (0, 0)),
    ]
    out = pl.pallas_call(
        functools.partial(_compress_kernel, pages=pages),
        grid_spec=pltpu.PrefetchScalarGridSpec(
            num_scalar_prefetch=1, grid=(steps,), in_specs=in_specs,
            out_specs=pl.BlockSpec((1, per_page * pages, KV_ROW), lambda s, tab: (s, 0, 0)),
            scratch_shapes=[pltpu.VMEM((pages * PAGE_SIZE, KV_ROW // 2), F32),
                            pltpu.VMEM((pages * PAGE_SIZE, KV_ROW // 2), F32)]),
        out_shape=jax.ShapeDtypeStruct((steps, per_page * pages, KV_ROW), F32),
        compiler_params=_cparams(("arbitrary",)), name="cmp_compress",
    )(page_ids, *([rows3d] * pages), lw["cmp_pe"], lw["cmp_w1"], lw["cmp_b1"], lw["cmp_w2"])
    out = out.reshape(steps, per_page, pages, KV_ROW).transpose(0, 2, 1, 3)
    return out.reshape(n_logical * per_page, KV_ROW)


def _split_compressed(tok, nb):
    n = tok.shape[0] // nb
    t = tok.reshape(nb, n // 2, 2, 2, NSA_KV_HEADS, NSA_DIM)
    t = t.transpose(3, 0, 4, 2, 1, 5).reshape(2, nb, NSA_KV_HEADS, n, NSA_DIM).astype(BF16)
    return t[0], t[1]


def _cmp_prompt_kernel(q_ref, kc_ref, vc_ref, bias_ref, gate_ref, o_ref, sel_ref, *, tq, n_cmp):
    qi = pl.program_id(2)
    kc = kc_ref[0, 0]
    vc = vc_ref[0, 0]
    maskf = (bias_ref[0] > 0.5 * NEG_BIG).astype(F32)
    gates = gate_ref[0]
    imp = jnp.zeros((tq, n_cmp), F32)
    for hh in range(NSA_GROUP):
        qh = q_ref[0, :, hh * NSA_DIM:(hh + 1) * NSA_DIM]
        z = _dot_nt(qh, kc) + bias_ref[hh]
        e = jnp.exp(z - jnp.max(z, axis=-1, keepdims=True)) * maskf
        p = e / jnp.maximum(jnp.sum(e, axis=-1, keepdims=True), 1e-30)
        imp = imp + p
        o = _dot(p.astype(BF16), vc)
        o_ref[0, :, hh * NSA_DIM:(hh + 1) * NSA_DIM] = o * gates[:, hh:hh + 1]
    n_slc = n_cmp // 2
    imp_slc = imp[:, 0:n_slc] + imp[:, n_slc:n_cmp]
    t = qi * tq + lax.broadcasted_iota(jnp.int32, (tq, n_slc), 0)
    j = lax.broadcasted_iota(jnp.int32, (tq, n_slc), 1)
    cur = t // SLC_BLOCK
    forced = (j == 0) | (j == cur) | (j == cur - 1)
    score = jnp.where(forced, jnp.inf, jnp.where(j <= cur, imp_slc, -jnp.inf))
    rank = jnp.zeros((tq, n_slc), F32)
    for i in range(n_slc):
        ci = score[:, i:i + 1]
        beats = (ci > score) | ((ci == score) & (i < j))
        rank = rank + jnp.where(beats, 1.0, 0.0)
    sel_ref[0, 0] = jnp.where(rank < float(min(N_SELECT, n_slc)), 1.0, 0.0)


def _cmp_prompt(q, kc, vc, bias, gates, nb, nt):
    tq = ATT_TILE
    n_cmp = kc.shape[2]
    g = NSA_KV_HEADS
    gw = NSA_GROUP * NSA_DIM
    return pl.pallas_call(
        functools.partial(_cmp_prompt_kernel, tq=tq, n_cmp=n_cmp),
        grid=(nb, g, nt // tq),
        in_specs=[pl.BlockSpec((1, tq, gw), lambda b, gi, i: (b, i, gi)),
                  pl.BlockSpec((1, 1, n_cmp, NSA_DIM), lambda b, gi, i: (b, gi, 0, 0)),
                  pl.BlockSpec((1, 1, n_cmp, NSA_DIM), lambda b, gi, i: (b, gi, 0, 0)),
                  pl.BlockSpec((NSA_GROUP, tq, n_cmp), lambda b, gi, i: (gi, i, 0)),
                  pl.BlockSpec((1, tq, GATE_LANES), lambda b, gi, i: (b, i, gi))],
        out_specs=(pl.BlockSpec((1, tq, gw), lambda b, gi, i: (b, i, gi)),
                   pl.BlockSpec((1, 1, tq, n_cmp // 2), lambda b, gi, i: (b, gi, i, 0))),
        out_shape=(jax.ShapeDtypeStruct((nb, nt, D_MODEL), F32),
                   jax.ShapeDtypeStruct((nb, g, nt, n_cmp // 2), F32)),
        compiler_params=_cparams(("arbitrary", "arbitrary", "arbitrary")), name="cmp_attn_select",
    )(q.reshape(nb, nt, D_MODEL), kc, vc, bias, gates.reshape(nb, nt, g * GATE_LANES))


def _nsa_flash_kernel(*refs, tq, slc, gate_col):
    if slc:
        q_ref, k_ref, v_ref, bias_ref, gate_ref, sel_ref, exp_ref, o_ref, m_ref, l_ref, acc_ref = refs
    else:
        q_ref, k_ref, v_ref, bias_ref, gate_ref, o_ref, m_ref, l_ref, acc_ref = refs
    qi = pl.program_id(2)
    m_ref[...] = jnp.full(m_ref.shape, NEG_BIG, F32)
    l_ref[...] = jnp.zeros(l_ref.shape, F32)
    acc_ref[...] = jnp.zeros(acc_ref.shape, F32)
    if slc:
        sel = sel_ref[0, 0].astype(BF16)

    def tile(kt, btype):
        start = pl.multiple_of(kt * tq, tq)
        ks = k_ref[0, 0, pl.ds(start, tq), :]
        vs = v_ref[0, 0, pl.ds(start, tq), :]
        if slc:
            maskadd = (_dot(sel, exp_ref[kt]) - 1.0) * (-NEG_BIG)
        for hh in range(NSA_GROUP):
            s = _dot_nt(q_ref[0, :, hh * NSA_DIM:(hh + 1) * NSA_DIM], ks)
            if btype is not None:
                s = s + bias_ref[hh, btype]
            if slc:
                s = s + maskadd
            m_old = m_ref[hh]
            m_new = jnp.maximum(m_old, jnp.max(s, axis=-1, keepdims=True))
            alpha = jnp.exp(m_old - m_new)
            p = jnp.exp(s - m_new)
            l_ref[hh] = alpha * l_ref[hh] + jnp.sum(p, axis=-1, keepdims=True)
            acc_ref[hh] = alpha * acc_ref[hh] + _dot(p.astype(BF16), vs)
            m_ref[hh] = m_new

    if slc:
        def far(kt, carry):
            tile(kt, None)
            return carry
        lax.fori_loop(0, jnp.maximum(qi - 1, 0), far, 0)
    else:
        @pl.when(qi >= 2)
        def _():
            tile(qi - 2, 2)

    @pl.when(qi >= 1)
    def _():
        tile(qi - 1, 1)

    tile(qi, 0)
    gates = gate_ref[0]
    for hh in range(NSA_GROUP):
        c = gate_col * NSA_GROUP + hh
        o_ref[0, :, hh * NSA_DIM:(hh + 1) * NSA_DIM] = acc_ref[hh] / l_ref[hh] * gates[:, c:c + 1]


def _nsa_flash(q, k, v, bias_tiles, gates, nb, nt, sel=None, expand=None):
    tq = ATT_TILE
    g = NSA_KV_HEADS
    gw = NSA_GROUP * NSA_DIM
    slc = sel is not None
    in_specs = [pl.BlockSpec((1, tq, gw), lambda b, gi, i: (b, i, gi)),
                pl.BlockSpec((1, 1, nt, NSA_DIM), lambda b, gi, i: (b, gi, 0, 0)),
                pl.BlockSpec((1, 1, nt, NSA_DIM), lambda b, gi, i: (b, gi, 0, 0)),
                pl.BlockSpec((NSA_GROUP, 3, tq, tq), lambda b, gi, i: (gi, 0, 0, 0)),
                pl.BlockSpec((1, tq, GATE_LANES), lambda b, gi, i: (b, i, gi))]
    args = [q.reshape(nb, nt, D_MODEL), k, v, bias_tiles, gates.reshape(nb, nt, g * GATE_LANES)]
    if slc:
        n_slc = sel.shape[-1]
        in_specs += [pl.BlockSpec((1, 1, tq, n_slc), lambda b, gi, i: (b, gi, i, 0)),
                     pl.BlockSpec((nt // tq, n_slc, tq), lambda b, gi, i: (0, 0, 0))]
        args += [sel, expand]
    return pl.pallas_call(
        functools.partial(_nsa_flash_kernel, tq=tq, slc=slc, gate_col=1 if slc else 2),
        grid=(nb, g, nt // tq),
        in_specs=in_specs,
        out_specs=pl.BlockSpec((1, tq, gw), lambda b, gi, i: (b, i, gi)),
        out_shape=jax.ShapeDtypeStruct((nb, nt, D_MODEL), F32),
        scratch_shapes=[pltpu.VMEM((NSA_GROUP, tq, 1), F32), pltpu.VMEM((NSA_GROUP, tq, 1), F32),
                        pltpu.VMEM((NSA_GROUP, tq, NSA_DIM), F32)],
        compiler_params=_cparams(("arbitrary", "arbitrary", "arbitrary")),
        name="slc_attn" if slc else "win_attn",
    )(*args)


def _mla_prompt_kernel(q_ref, lat_ref, o_ref, m_ref, l_ref, acc_ref, *, tq):
    qi = pl.program_id(1)
    m_ref[...] = jnp.full(m_ref.shape, NEG_BIG, F32)
    l_ref[...] = jnp.zeros(l_ref.shape, F32)
    acc_ref[...] = jnp.zeros(acc_ref.shape, F32)

    def tile(kt, diag):
        start = pl.multiple_of(kt * tq, tq)
        ks = lat_ref[0, pl.ds(start, tq), :]
        vs = ks[:, 0:KV_LORA]
        if diag:
            row = lax.broadcasted_iota(jnp.int32, (tq, tq), 0)
            col = lax.broadcasted_iota(jnp.int32, (tq, tq), 1)
            causal = col <= row
        for hd in range(MLA_HEADS):
            s = _dot_nt(q_ref[0, hd], ks)
            if diag:
                s = jnp.where(causal, s, NEG_BIG)
            m_old = m_ref[hd]
            m_new = jnp.maximum(m_old, jnp.max(s, axis=-1, keepdims=True))
            alpha = jnp.exp(m_old - m_new)
            p = jnp.exp(s - m_new)
            l_ref[hd] = alpha * l_ref[hd] + jnp.sum(p, axis=-1, keepdims=True)
            acc_ref[hd] = alpha * acc_ref[hd] + _dot(p.astype(BF16), vs)
            m_ref[hd] = m_new

    def body(kt, carry):
        tile(kt, False)
        return carry
    lax.fori_loop(0, qi, body, 0)
    tile(qi, True)
    for hd in range(MLA_HEADS):
        o_ref[0, hd] = (acc_ref[hd] / l_ref[hd]).astype(BF16)


def _mla_prompt(qmla, latb, nb, nt):
    tq = ATT_TILE
    return pl.pallas_call(
        functools.partial(_mla_prompt_kernel, tq=tq),
        grid=(nb, nt // tq),
        in_specs=[pl.BlockSpec((1, MLA_HEADS, tq, LAT), lambda b, i: (b, 0, i, 0)),
                  pl.BlockSpec((1, nt, LAT), lambda b, i: (b, 0, 0))],
        out_specs=pl.BlockSpec((1, MLA_HEADS, tq, KV_LORA), lambda b, i: (b, 0, i, 0)),
        out_shape=jax.ShapeDtypeStruct((nb, MLA_HEADS, nt, KV_LORA), BF16),
        scratch_shapes=[pltpu.VMEM((MLA_HEADS, tq, 1), F32), pltpu.VMEM((MLA_HEADS, tq, 1), F32),
                        pltpu.VMEM((MLA_HEADS, tq, KV_LORA), F32)],
        compiler_params=_cparams(("arbitrary", "arbitrary")), name="mla_attn",
    )(qmla, latb.reshape(nb, nt, LAT))


def _merge_kernel(oc_ref, os_ref, ow_ref, ol_ref, gm_ref, x_ref, mod_ref, wuv_ref, wo_ref, o_ref):
    o_nsa = oc_ref[...] + os_ref[...] + ow_ref[...]
    o_mla = jnp.concatenate([_dot(ol_ref[0, hd], wuv_ref[hd]) for hd in range(MLA_HEADS)], axis=-1)
    gm = gm_ref[...]
    merged = (gm[:, 0:D_MODEL] * o_nsa + gm[:, D_MODEL:] * o_mla).astype(BF16)
    o_ref[...] = x_ref[...] + mod_ref[:, 2, :] * _dot(merged, wo_ref[...])


def _merge(o_cmp, o_slc, o_win, o_lat, gm, x, mod, per_row_mod, lw, nb, nt, tm):
    m = nb * nt
    tpb = nt // tm
    d = D_MODEL
    row = lambda i: (i, 0)
    if per_row_mod:
        mod_spec = pl.BlockSpec((tm, 6, d), lambda i: (i, 0, 0))
    else:
        mod_spec = pl.BlockSpec((1, 6, d), lambda i: (i // tpb, 0, 0))
    return pl.pallas_call(
        _merge_kernel, grid=(m // tm,),
        in_specs=[pl.BlockSpec((tm, d), row), pl.BlockSpec((tm, d), row), pl.BlockSpec((tm, d), row),
                  pl.BlockSpec((1, MLA_HEADS, tm, KV_LORA), lambda i: (i // tpb, 0, i % tpb, 0)),
                  pl.BlockSpec((tm, 2 * d), row), pl.BlockSpec((tm, d), row), mod_spec,
                  pl.BlockSpec((MLA_HEADS, KV_LORA, MLA_V), lambda i: (0, 0, 0)),
                  pl.BlockSpec((d, d), lambda i: (0, 0))],
        out_specs=pl.BlockSpec((tm, d), row),
        out_shape=jax.ShapeDtypeStruct((m, d), F32),
        compiler_params=_cparams(("arbitrary",)), name="merge_out_proj",
    )(o_cmp, o_slc, o_win, o_lat, gm, x, mod, lw["w_uv"], lw["w_out"])


def _ffn_kernel(x_ref, mod_ref, gn_ref, w1_ref, w3_ref, w2_ref, gf_ref, o_ref, h_ref, acc_ref, *, final):
    f = pl.program_id(1)

    @pl.when(f == 0)
    def _():
        y = _rms(x_ref[...], gn_ref[...])
        h_ref[...] = (y * (1.0 + mod_ref[:, 4, :]) + mod_ref[:, 3, :]).astype(BF16)
        acc_ref[...] = jnp.zeros(acc_ref.shape, F32)

    h = h_ref[...]
    a = _dot(h, w1_ref[...])
    b = _dot(h, w3_ref[...])
    act = (a * jax.nn.sigmoid(a) * b).astype(BF16)
    acc_ref[...] += _dot(act, w2_ref[...])

    @pl.when(f == pl.num_programs(1) - 1)
    def _():
        y = x_ref[...] + mod_ref[:, 5, :] * acc_ref[...]
        if final:
            y = _rms(y, gf_ref[...])
        o_ref[...] = y


def _ffn(x, mod, per_row_mod, lw, gfinal, final, nb, nt, tm):
    m = nb * nt
    tpb = nt // tm
    d = D_MODEL
    dff = lw["ffn_w2"].shape[0]
    tf = dff // 2
    nf = dff // tf
    row = lambda i, f: (i, 0)
    if per_row_mod:
        mod_spec = pl.BlockSpec((tm, 6, d), lambda i, f: (i, 0, 0))
    else:
        mod_spec = pl.BlockSpec((1, 6, d), lambda i, f: (i // tpb, 0, 0))
    return pl.pallas_call(
        functools.partial(_ffn_kernel, final=final), grid=(m // tm, nf),
        in_specs=[pl.BlockSpec((tm, d), row), mod_spec, pl.BlockSpec((1, d), lambda i, f: (0, 0)),
                  pl.BlockSpec((d, tf), lambda i, f: (0, f)),
                  pl.BlockSpec((d, tf), lambda i, f: (0, nf + f)),
                  pl.BlockSpec((tf, d), lambda i, f: (f, 0)),
                  pl.BlockSpec((1, d), lambda i, f: (0, 0))],
        out_specs=pl.BlockSpec((tm, d), row),
        out_shape=jax.ShapeDtypeStruct((m, d), F32),
        scratch_shapes=[pltpu.VMEM((tm, d), BF16), pltpu.VMEM((tm, d), F32)],
        compiler_params=_cparams(("arbitrary", "arbitrary")), name="ffn",
    )(x, mod, lw["norm_ffn_g"], lw["ffn_w13"], lw["ffn_w13"], lw["ffn_w2"], gfinal)


def _cmp_sample_kernel(q_ref, kc_ref, vc_ref, bias_ref, gate_ref, o_ref, idx_ref, *, nt, n_cmp, n_pick):
    rows = nt * NSA_GROUP
    n_past = n_cmp // 2
    lane = lax.broadcasted_iota(jnp.int32, (nt, n_past), 1)
    for g in range(NSA_KV_HEADS):
        qg = q_ref[0, :, g].reshape(rows, NSA_DIM)
        z = _dot_nt(qg, kc_ref[0, g]) + bias_ref[g]
        maskf = (bias_ref[g] > 0.5 * NEG_BIG).astype(F32)
        e = jnp.exp(z - jnp.max(z, axis=-1, keepdims=True)) * maskf
        p = e / jnp.maximum(jnp.sum(e, axis=-1, keepdims=True), 1e-30)
        o = _dot(p.astype(BF16), vc_ref[0, g]) * gate_ref[0, g, 0]
        o_ref[0, :, g] = o.reshape(nt, NSA_GROUP, NSA_DIM)
        imp = jnp.sum(p.reshape(nt, NSA_GROUP, n_cmp), axis=1)
        imp = imp[:, 0:n_past] + imp[:, n_past:n_cmp]
        score = jnp.where((lane == 0) | (lane == n_past - 1), jnp.inf, imp)

        def body(i, rank):
            ci = jnp.sum(jnp.where(lane == i, score, 0.0), axis=-1, keepdims=True)
            beats = (ci > score) | ((ci == score) & (i < lane))
            return rank + jnp.where(beats, 1, 0)
        rank = lax.fori_loop(0, n_past, body, jnp.zeros((nt, n_past), jnp.int32))
        out_lane = lax.broadcasted_iota(jnp.int32, (nt, 128), 1)
        picked = jnp.zeros((nt, 128), jnp.int32)
        for r in range(n_pick):
            ir = jnp.sum(jnp.where(rank == r, lane, 0), axis=-1, keepdims=True)
            picked = jnp.where(out_lane == r, ir, picked)
        idx_ref[0, g] = picked


def _cmp_sample(q5, kc, vc, bias, gate_cols, nb, nt, n_pick):
    n_cmp = kc.shape[2]
    g = NSA_KV_HEADS
    rows = nt * NSA_GROUP
    return pl.pallas_call(
        functools.partial(_cmp_sample_kernel, nt=nt, n_cmp=n_cmp, n_pick=n_pick),
        grid=(nb,),
        in_specs=[pl.BlockSpec((1, nt, g, NSA_GROUP, NSA_DIM), lambda b: (b, 0, 0, 0, 0)),
                  pl.BlockSpec((1, g, n_cmp, NSA_DIM), lambda b: (b, 0, 0, 0)),
                  pl.BlockSpec((1, g, n_cmp, NSA_DIM), lambda b: (b, 0, 0, 0)),
                  pl.BlockSpec((g, rows, n_cmp), lambda b: (0, 0, 0)),
                  pl.BlockSpec((1, g, 1, rows, 1), lambda b: (b, 0, 0, 0, 0))],
        out_specs=(pl.BlockSpec((1, nt, g, NSA_GROUP, NSA_DIM), lambda b: (b, 0, 0, 0, 0)),
                   pl.BlockSpec((1, g, nt, 128), lambda b: (b, 0, 0, 0))),
        out_shape=(jax.ShapeDtypeStruct((nb, nt, g, NSA_GROUP, NSA_DIM), F32),
                   jax.ShapeDtypeStruct((nb, g, nt, 128), jnp.int32)),
        compiler_params=_cparams(("arbitrary",)), name="cmp_attn_topk_sample",
    )(q5, kc, vc, bias, gate_cols)


def _slc_sample_kernel(idx_ref, pt_ref, *refs, nt, n_pick, n_past, n_pages):
    del pt_ref, n_pages
    nblk = NSA_KV_HEADS * n_pick
    blk_refs = refs[:nblk]
    q_ref, new_ref, bias_ref, biasc_ref, gate_ref, o_ref = refs[nblk:]
    b = pl.program_id(0)
    t = pl.program_id(1)
    half = NSA_KV_HEADS * NSA_DIM
    for g in range(NSA_KV_HEADS):
        qg = q_ref[0, 0, g]
        s_list, v_list = [], []
        for n in range(n_pick):
            blk = blk_refs[g * n_pick + n][0]
            j = idx_ref[((b * nt + t) * NSA_KV_HEADS + g) * n_pick + n]
            near = jnp.clip(j - (n_past - 3), 0, 2)
            s_list.append(_dot_nt(qg, blk[:, g * NSA_DIM:(g + 1) * NSA_DIM].astype(BF16)) + bias_ref[0, g, near])
            v_list.append(blk[:, half + g * NSA_DIM:half + (g + 1) * NSA_DIM].astype(BF16))
        new = new_ref[0]
        s_list.append(_dot_nt(qg, new[:, g * NSA_DIM:(g + 1) * NSA_DIM].astype(BF16)) + biasc_ref[0, g])
        v_list.append(new[:, half + g * NSA_DIM:half + (g + 1) * NSA_DIM].astype(BF16))
        m = s_list[0].max(axis=-1, keepdims=True)
        for s in s_list[1:]:
            m = jnp.maximum(m, s.max(axis=-1, keepdims=True))
        l = jnp.zeros((NSA_GROUP, 1), F32)
        acc = jnp.zeros((NSA_GROUP, NSA_DIM), F32)
        for s, v in zip(s_list, v_list):
            p = jnp.exp(s - m)
            l = l + jnp.sum(p, axis=-1, keepdims=True)
            acc = acc + _dot(p.astype(BF16), v)
        o_ref[0, 0, g] = acc / l * gate_ref[0, g, 0, 0]


def _slc_sample(idx_flat, pt_flat, cache_blocks, q5, new_rows, bias_near, bias_cur, gate_cols, nb, nt, n_pick,
                n_past, n_pages):
    g = NSA_KV_HEADS

    def blk_map(gi, n):
        def f(b, t, idx, pt):
            j = idx[((b * nt + t) * g + gi) * n_pick + n]
            return (pt[b * n_pages + (j >> 1)] * 2 + (j & 1), 0, 0)
        return f

    in_specs = [pl.BlockSpec((1, SLC_BLOCK, KV_ROW), blk_map(gi, n)) for gi in range(g) for n in range(n_pick)]
    in_specs += [
        pl.BlockSpec((1, 1, g, NSA_GROUP, NSA_DIM), lambda b, t, idx, pt: (b, t, 0, 0, 0)),
        pl.BlockSpec((1, 8, KV_ROW), lambda b, t, idx, pt: (b, 0, 0)),
        pl.BlockSpec((1, g, 3, NSA_GROUP, SLC_BLOCK), lambda b, t, idx, pt: (t, 0, 0, 0, 0)),
        pl.BlockSpec((1, g, NSA_GROUP, 8), lambda b, t, idx, pt: (t, 0, 0, 0)),
        pl.BlockSpec((1, g, 1, 1, NSA_GROUP, 1), lambda b, t, idx, pt: (b, 0, 0, t, 0, 0)),
    ]
    return pl.pallas_call(
        functools.partial(_slc_sample_kernel, nt=nt, n_pick=n_pick, n_past=n_past, n_pages=n_pages),
        grid_spec=pltpu.PrefetchScalarGridSpec(
            num_scalar_prefetch=2, grid=(nb, nt), in_specs=in_specs,
            out_specs=pl.BlockSpec((1, 1, g, NSA_GROUP, NSA_DIM), lambda b, t, idx, pt: (b, t, 0, 0, 0))),
        out_shape=jax.ShapeDtypeStruct((nb, nt, g, NSA_GROUP, NSA_DIM), F32),
        compiler_params=_cparams(("arbitrary", "arbitrary")), name="slc_attn_sample",
    )(idx_flat, pt_flat, *([cache_blocks] * (g * n_pick)), q5, new_rows, bias_near, bias_cur, gate_cols)


def _win_sample_kernel(q_ref, buf_ref, new_ref, bias_ref, biasn_ref, gate_ref, o_ref, *, nt):
    rows = nt * NSA_GROUP
    half = NSA_KV_HEADS * NSA_DIM
    buf = buf_ref[0, 0]
    new = new_ref[0]
    for g in range(NSA_KV_HEADS):
        qg = q_ref[0, :, g].reshape(rows, NSA_DIM)
        s1 = _dot_nt(qg, buf[:, g * NSA_DIM:(g + 1) * NSA_DIM].astype(BF16)) + bias_ref[g]
        s2 = _dot_nt(qg, new[:, g * NSA_DIM:(g + 1) * NSA_DIM].astype(BF16)) + biasn_ref[g]
        m = jnp.maximum(s1.max(axis=-1, keepdims=True), s2.max(axis=-1, keepdims=True))
        p1 = jnp.exp(s1 - m)
        p2 = jnp.exp(s2 - m)
        l = jnp.sum(p1, axis=-1, keepdims=True) + jnp.sum(p2, axis=-1, keepdims=True)
        acc = _dot(p1.astype(BF16), buf[:, half + g * NSA_DIM:half + (g + 1) * NSA_DIM].astype(BF16))
        acc = acc + _dot(p2.astype(BF16), new[:, half + g * NSA_DIM:half + (g + 1) * NSA_DIM].astype(BF16))
        o_ref[0, :, g] = (acc / l * gate_ref[0, g, 0]).reshape(nt, NSA_GROUP, NSA_DIM)


def _win_sample(q5, win_state, layer, new_rows, bias_buf, bias_new, gate_cols, nb, nt):
    g = NSA_KV_HEADS
    rows = nt * NSA_GROUP
    wlen = win_state.shape[2]
    return pl.pallas_call(
        functools.partial(_win_sample_kernel, nt=nt),
        grid=(nb,),
        in_specs=[pl.BlockSpec((1, nt, g, NSA_GROUP, NSA_DIM), lambda b: (b, 0, 0, 0, 0)),
                  pl.BlockSpec((1, 1, wlen, KV_ROW), lambda b: (layer, b, 0, 0)),
                  pl.BlockSpec((1, 8, KV_ROW), lambda b: (b, 0, 0)),
                  pl.BlockSpec((g, rows, wlen), lambda b: (0, 0, 0)),
                  pl.BlockSpec((g, rows, 8), lambda b: (0, 0, 0)),
                  pl.BlockSpec((1, g, 1, rows, 1), lambda b: (b, 0, 2, 0, 0))],
        out_specs=pl.BlockSpec((1, nt, g, NSA_GROUP, NSA_DIM), lambda b: (b, 0, 0, 0, 0)),
        out_shape=jax.ShapeDtypeStruct((nb, nt, g, NSA_GROUP, NSA_DIM), F32),
        compiler_params=_cparams(("arbitrary",)), name="win_attn_sample",
    )(q5, win_state, new_rows, bias_buf, bias_new, gate_cols)


def _mla_sample_kernel(pt_ref, *refs, pages, nt):
    del pt_ref
    page_refs = refs[:pages]
    q_ref, new_ref, o_ref, m_ref, l_ref, acc_ref = refs[pages:]
    step = pl.program_id(1)
    rows = MLA_HEADS * nt

    @pl.when(step == 0)
    def _():
        m_ref[...] = jnp.full(m_ref.shape, NEG_BIG, F32)
        l_ref[...] = jnp.zeros(l_ref.shape, F32)
        acc_ref[...] = jnp.zeros(acc_ref.shape, F32)

    q = q_ref[0]
    ks = [page_refs[k][0].astype(BF16) for k in range(pages)]
    ss = [_dot_nt(q, kk) for kk in ks]
    m_old = m_ref[...]
    m_new = m_old
    for s in ss:
        m_new = jnp.maximum(m_new, s.max(axis=-1, keepdims=True))
    alpha = jnp.exp(m_old - m_new)
    l = alpha * l_ref[...]
    acc = alpha * acc_ref[...]
    for s, kk in zip(ss, ks):
        p = jnp.exp(s - m_new)
        l = l + jnp.sum(p, axis=-1, keepdims=True)
        acc = acc + _dot(p.astype(BF16), kk[:, 0:KV_LORA])
    m_ref[...] = m_new
    l_ref[...] = l
    acc_ref[...] = acc

    @pl.when(step == pl.num_programs(1) - 1)
    def _():
        new = new_ref[0].astype(BF16)
        tq = lax.broadcasted_iota(jnp.int32, (rows, 8), 0) % nt
        tk = lax.broadcasted_iota(jnp.int32, (rows, 8), 1)
        s = jnp.where(tk <= tq, _dot_nt(q, new), NEG_BIG)
        m_o = m_ref[...]
        m_n = jnp.maximum(m_o, s.max(axis=-1, keepdims=True))
        a = jnp.exp(m_o - m_n)
        p = jnp.exp(s - m_n)
        lf = a * l_ref[...] + jnp.sum(p, axis=-1, keepdims=True)
        accf = a * acc_ref[...] + _dot(p.astype(BF16), new[:, 0:KV_LORA])
        o_ref[0] = (accf / lf).astype(BF16)


def _mla_sample(pt_flat, cache_pages, q_rows, new_rows, nb, nt, n_pages):
    pages = MLA_PAGES
    steps = n_pages // pages
    rows = MLA_HEADS * nt

    def page_map(k):
        return lambda b, s, pt: (pt[b * n_pages + s * pages + k], 0, 0)

    in_specs = [pl.BlockSpec((1, PAGE_SIZE, LAT), page_map(k)) for k in range(pages)]
    in_specs += [pl.BlockSpec((1, rows, LAT), lambda b, s, pt: (b, 0, 0)),
                 pl.BlockSpec((1, 8, LAT), lambda b, s, pt: (b, 0, 0))]
    return pl.pallas_call(
        functools.partial(_mla_sample_kernel, pages=pages, nt=nt),
        grid_spec=pltpu.PrefetchScalarGridSpec(
            num_scalar_prefetch=1, grid=(nb, steps), in_specs=in_specs,
            out_specs=pl.BlockSpec((1, rows, KV_LORA), lambda b, s, pt: (b, 0, 0)),
            scratch_shapes=[pltpu.VMEM((rows, 1), F32), pltpu.VMEM((rows, 1), F32),
                            pltpu.VMEM((rows, KV_LORA), F32)]),
        out_shape=jax.ShapeDtypeStruct((nb, rows, KV_LORA), BF16),
        compiler_params=_cparams(("arbitrary", "arbitrary")), name="mla_attn_sample",
    )(pt_flat, *([cache_pages] * pages), q_rows, new_rows)


def _pack_layer(l, w):
    d = D_MODEL
    w_in = w["w_in"][l]
    sizes = (NSA_HEADS * NSA_DIM, KV_ROW, KV_ROW, KV_ROW, 3 * NSA_HEADS, Q_LORA, KV_LORA, MLA_ROPE, 2 * d)
    offs = np.concatenate([[0], np.cumsum(sizes)])
    seg = [w_in[:, offs[i]:offs[i + 1]] for i in range(len(sizes))]
    gsrc = seg[4].reshape(d, NSA_KV_HEADS, NSA_GROUP, 3).transpose(0, 1, 3, 2).reshape(d, NSA_KV_HEADS, 3 * NSA_GROUP)
    gates = jnp.pad(gsrc, ((0, 0), (0, 0), (0, GATE_LANES - 3 * NSA_GROUP))).reshape(d, NSA_KV_HEADS * GATE_LANES)
    kr = jnp.pad(seg[7], ((0, 0), (0, 128 - MLA_ROPE)))
    w_packed = jnp.concatenate([seg[0], seg[1], seg[2], seg[3], gates, seg[5], seg[6], kr, seg[8]], axis=1)
    wuq = w["mla_w_uq"][l].reshape(Q_LORA, MLA_HEADS, MLA_NOPE + MLA_ROPE)
    wuq = jnp.concatenate([wuq[:, :, :MLA_NOPE].reshape(Q_LORA, -1), wuq[:, :, MLA_NOPE:].reshape(Q_LORA, -1)], axis=1)
    w1 = w["nsa_cmp_w1"][l].reshape(2, CMP_BLOCK, NSA_DIM, NSA_DIM)
    eye_g = jnp.eye(NSA_KV_HEADS, dtype=F32)
    eye_i = jnp.eye(2, dtype=F32)
    w1big = jnp.einsum("irdo,ij,gh->rigdjho", w1, eye_i, eye_g).reshape(CMP_BLOCK, KV_ROW, KV_ROW)
    w2big = jnp.einsum("ido,ij,gh->igdjho", w["nsa_cmp_w2"][l], eye_i, eye_g).reshape(KV_ROW, KV_ROW)
    pe = w["nsa_cmp_pe"][l]
    pe_big = jnp.broadcast_to(pe.transpose(1, 0, 2)[:, :, None, :], (CMP_BLOCK, 2, NSA_KV_HEADS, NSA_DIM))
    b1big = jnp.broadcast_to(w["nsa_cmp_b1"][l][:, None, :], (2, NSA_KV_HEADS, NSA_DIM))
    return dict(
        norm_mix_g=w["norm_mix_g"][l].reshape(1, d),
        w_in=w_packed.astype(BF16),
        mla_q_norm_g=w["mla_q_norm_g"][l].reshape(1, Q_LORA),
        w_uq=wuq.astype(BF16),
        w_ukT=w["mla_w_uk"][l].transpose(1, 2, 0).astype(BF16),
        mla_kv_norm_g=w["mla_kv_norm_g"][l].reshape(1, KV_LORA),
        w_uv=w["mla_w_uv"][l].transpose(1, 0, 2).astype(BF16),
        w_out=w["w_out"][l].astype(BF16),
        norm_ffn_g=w["norm_ffn_g"][l].reshape(1, d),
        ffn_w13=w["ffn_w13"][l].astype(BF16),
        ffn_w2=w["ffn_w2"][l].astype(BF16),
        cmp_pe=pe_big.reshape(CMP_BLOCK, 1, KV_ROW),
        cmp_w1=w1big.astype(BF16),
        cmp_b1=b1big.reshape(1, KV_ROW),
        cmp_w2=w2big.astype(BF16),
    )


def _cmp_block_ends(n_cmp):
    order = np.concatenate([np.arange(0, n_cmp, 2), np.arange(1, n_cmp, 2)])
    return (order + 1) * CMP_BLOCK - 1


def _prompt_bias_tables(table, nt):
    tq = ATT_TILE
    i = np.arange(tq)[:, None]
    j = np.arange(tq)[None, :]
    d0, d1, d2 = i - j, tq + i - j, 2 * tq + i - j
    bucket = np.concatenate([_t5_bucket_np(d0), _t5_bucket_np(d1), _t5_bucket_np(d2)], axis=0)
    mask = np.concatenate([np.where(d0 >= 0, 0.0, NEG_BIG), np.zeros((tq, tq)),
                           np.where(d2 < WINDOW, 0.0, NEG_BIG)], axis=0).astype(np.float32)
    tiles = _expand_bias(table, bucket, mask, True).reshape(NSA_HEADS, 3, tq, tq)
    n_cmp = nt // CMP_BLOCK
    dist = np.arange(nt)[:, None] - _cmp_block_ends(n_cmp)[None, :]
    cmp_bias = _expand_bias(table, _t5_bucket_np(dist), np.where(dist >= 0, 0.0, NEG_BIG).astype(np.float32), False)
    n_slc = nt // SLC_BLOCK
    expand = (np.arange(n_slc)[None, :, None] ==
              (np.arange(nt // tq)[:, None, None] * tq + np.arange(tq)[None, None, :]) // SLC_BLOCK)
    return tiles, cmp_bias, jnp.asarray(expand.astype(np.float32), dtype=BF16)


def _sample_bias_tables(table, nt, past, wlen):
    g, hg = NSA_KV_HEADS, NSA_GROUP
    rows = nt * hg
    q_pos = past + np.arange(nt)

    def per_group(b, width):
        return b.reshape(g, hg, nt, width).transpose(0, 2, 1, 3).reshape(g, rows, width)

    n_cmp = past // CMP_BLOCK
    dist = q_pos[:, None] - _cmp_block_ends(n_cmp)[None, :]
    cmp_bias = per_group(_expand_bias(table, _t5_bucket_np(dist),
                                      np.where(dist >= 0, 0.0, NEG_BIG).astype(np.float32), False), n_cmp)
    dist = q_pos[:, None] - (past - wlen + np.arange(wlen))[None, :]
    ok = (dist >= 0) & (dist < WINDOW)
    win_bias = per_group(_expand_bias(table, _t5_bucket_np(dist), np.where(ok, 0.0, NEG_BIG).astype(np.float32),
                                      False), wlen)
    dist = q_pos[:, None] - (past + np.arange(8))[None, :]
    ok = (dist >= 0) & (dist < WINDOW) & (np.arange(8)[None, :] < nt)
    new_mask = np.where(ok, 0.0, NEG_BIG).astype(np.float32)
    win_new = per_group(_expand_bias(table, _t5_bucket_np(dist), new_mask, False), 8)
    n_past = past // SLC_BLOCK
    near = np.stack([np.full((nt, SLC_BLOCK), 10 * MAX_DISTANCE),
                     q_pos[:, None] - ((n_past - 2) * SLC_BLOCK + np.arange(SLC_BLOCK))[None, :],
                     q_pos[:, None] - ((n_past - 1) * SLC_BLOCK + np.arange(SLC_BLOCK))[None, :]], axis=1)
    near = near.reshape(nt, 3 * SLC_BLOCK)
    slc_near = _expand_bias(table, _t5_bucket_np(near), np.zeros(near.shape, np.float32), True)
    slc_near = slc_near.reshape(g, hg, nt, 3, SLC_BLOCK).transpose(2, 0, 3, 1, 4)
    slc_cur = _expand_bias(table, _t5_bucket_np(dist), np.where((dist >= 0) & (np.arange(8)[None, :] < nt), 0.0,
                                                                 NEG_BIG).astype(np.float32), True)
    slc_cur = slc_cur.reshape(g, hg, nt, 8).transpose(2, 0, 1, 3)
    return cmp_bias, win_bias, win_new, slc_near, slc_cur


def kernel(x_prompt, x_sample, cache_cmp_kv, cache_slc_kv, cache_mla, state_win_kv, page_table, c_prompt, c_sample,
           rel_bias_table, ada_w, ada_b, norm_mix_g, w_in, nsa_cmp_pe, nsa_cmp_w1, nsa_cmp_b1, nsa_cmp_w2,
           mla_q_norm_g, mla_w_uq, mla_kv_norm_g, mla_w_uk, mla_w_uv, w_out, norm_ffn_g, ffn_w13, ffn_w2,
           final_norm_g):
    weights = dict(norm_mix_g=norm_mix_g, w_in=w_in, nsa_cmp_pe=nsa_cmp_pe, nsa_cmp_w1=nsa_cmp_w1,
                   nsa_cmp_b1=nsa_cmp_b1, nsa_cmp_w2=nsa_cmp_w2, mla_q_norm_g=mla_q_norm_g, mla_w_uq=mla_w_uq,
                   mla_kv_norm_g=mla_kv_norm_g, mla_w_uk=mla_w_uk, mla_w_uv=mla_w_uv, w_out=w_out,
                   norm_ffn_g=norm_ffn_g, ffn_w13=ffn_w13, ffn_w2=ffn_w2)
    depth = ada_w.shape[0]
    d = D_MODEL
    g = NSA_KV_HEADS
    pb, pt_len, _ = x_prompt.shape
    sb, st_len, _ = x_sample.shape
    n_pool = cache_cmp_kv.shape[1]
    n_pages = page_table.shape[1]
    past = n_pages * PAGE_SIZE
    wlen = state_win_kv.shape[2]
    n_past_blocks = past // SLC_BLOCK
    n_pick = N_SELECT - 1
    assert pt_len % ATT_TILE == 0 and pt_len >= WINDOW and st_len <= 8 and past >= wlen
    assert n_past_blocks > n_pick and n_pages % MLA_PAGES == 0

    layers = [_pack_layer(l, weights) for l in range(depth)]
    gfinal = final_norm_g.reshape(1, d)
    mod_all = _modulation(jnp.concatenate([c_prompt, c_sample], axis=0), ada_w, ada_b)

    tiles, cmp_bias_p, expand = _prompt_bias_tables(rel_bias_table, pt_len)
    cmp_bias_s, win_bias_s, win_new_s, slc_near_s, slc_cur_s = _sample_bias_tables(rel_bias_table, st_len, past, wlen)
    cos_p, sin_p = _rope_tables(jnp.arange(pt_len), MLA_HEADS)
    cos_s, sin_s = _rope_tables(jnp.tile(past + jnp.arange(st_len), sb), MLA_HEADS)
    prompt_pages = jnp.arange(pb * pt_len // PAGE_SIZE, dtype=jnp.int32)
    pt_flat = page_table.reshape(-1).astype(jnp.int32)

    x = x_prompt.reshape(pb * pt_len, d)
    st_p = [[], [], [], []]
    tm = 256
    for l in range(depth):
        lw = layers[l]
        mod = mod_all[l, :pb].reshape(pb, 6, d)
        (q, kvc, kvs, kvw, ksk, ksv, kwk, kwv, gates, qmla, lat, latb, gm) = _inproj(
            x, mod, False, lw, cos_p, sin_p, pb, pt_len, tm)
        tok = _compress(kvc.reshape(-1, PAGE_SIZE, KV_ROW), prompt_pages, lw)
        kc, vc = _split_compressed(tok, pb)
        o_cmp, sel = _cmp_prompt(q, kc, vc, cmp_bias_p, gates, pb, pt_len)
        o_slc = _nsa_flash(q, ksk, ksv, tiles, gates, pb, pt_len, sel=sel, expand=expand)
        o_win = _nsa_flash(q, kwk, kwv, tiles, gates, pb, pt_len)
        o_lat = _mla_prompt(qmla, latb, pb, pt_len)
        x = _merge(o_cmp.reshape(-1, d), o_slc.reshape(-1, d), o_win.reshape(-1, d), o_lat, gm, x, mod, False, lw,
                   pb, pt_len, tm)
        x = _ffn(x, mod, False, lw, gfinal, l == depth - 1, pb, pt_len, 512)
        st_p[0].append(kvc.reshape(pb, pt_len, 2, g, NSA_DIM))
        st_p[1].append(kvs.reshape(pb, pt_len, 2, g, NSA_DIM))
        st_p[2].append(lat.reshape(pb, pt_len, LAT))
        st_p[3].append(kvw.reshape(pb, pt_len, 2, g, NSA_DIM)[:, pt_len - min(WINDOW, pt_len):])
    y_prompt = x.reshape(pb, pt_len, d)

    ms = sb * st_len
    x = x_sample.reshape(ms, d)
    st_s = [[], [], [], []]
    cmp_pages = cache_cmp_kv.reshape(depth * n_pool, PAGE_SIZE, KV_ROW)
    slc_blocks = cache_slc_kv.reshape(depth * n_pool * (PAGE_SIZE // SLC_BLOCK), SLC_BLOCK, KV_ROW)
    mla_pages = cache_mla.reshape(depth * n_pool, PAGE_SIZE, LAT)
    win_state = state_win_kv.reshape(depth, sb, wlen, KV_ROW)
    for l in range(depth):
        lw = layers[l]
        mod = jnp.repeat(mod_all[l, pb:], st_len, axis=0).reshape(ms, 6, d)
        (q, kvc, kvs, kvw, _, _, _, _, gates, qmla, lat, _, gm) = _inproj(
            x, mod, True, lw, cos_s, sin_s, 1, ms, ms)
        pt_l = pt_flat + l * n_pool
        tok = _compress(cmp_pages, pt_l, lw)
        kc, vc = _split_compressed(tok, sb)
        q5 = q.reshape(sb, st_len, g, NSA_GROUP, NSA_DIM)
        gate_cols = gates.reshape(sb, st_len, g, GATE_LANES)[..., :3 * NSA_GROUP]
        gate_cols = gate_cols.reshape(sb, st_len, g, 3, NSA_GROUP).transpose(0, 2, 3, 1, 4)
        gate_rows = gate_cols.reshape(sb, g, 3, st_len * NSA_GROUP, 1)
        o_cmp, idx = _cmp_sample(q5, kc, vc, cmp_bias_s, gate_rows, sb, st_len, n_pick)
        idx_flat = idx[:, :, :st_len, :n_pick].transpose(0, 2, 1, 3).reshape(-1)
        pad8 = lambda a: jnp.pad(a.reshape(sb, st_len, -1), ((0, 0), (0, 8 - st_len), (0, 0)))
        o_slc = _slc_sample(idx_flat, pt_l, slc_blocks, q5, pad8(kvs), slc_near_s, slc_cur_s,
                            gate_cols.reshape(sb, g, 3, st_len, NSA_GROUP, 1)[:, :, 1:2], sb, st_len, n_pick,
                            n_past_blocks, n_pages)
        o_win = _win_sample(q5, win_state, l, pad8(kvw), win_bias_s, win_new_s, gate_rows, sb, st_len)
        q_rows = qmla.reshape(MLA_HEADS, sb, st_len, LAT).transpose(1, 0, 2, 3).reshape(sb, MLA_HEADS * st_len, LAT)
        o_lat = _mla_sample(pt_l, mla_pages, q_rows, pad8(lat), sb, st_len, n_pages)
        o_lat = o_lat.reshape(sb, MLA_HEADS, st_len, KV_LORA).transpose(1, 0, 2, 3).reshape(1, MLA_HEADS, ms, KV_LORA)
        x = _merge(o_cmp.reshape(ms, d), o_slc.reshape(ms, d), o_win.reshape(ms, d), o_lat, gm, x, mod, True, lw,
                   1, ms, ms)
        x = _ffn(x, mod, True, lw, gfinal, l == depth - 1, 1, ms, ms)
        st_s[0].append(kvc.reshape(sb, st_len, 2, g, NSA_DIM))
        st_s[1].append(kvs.reshape(sb, st_len, 2, g, NSA_DIM))
        st_s[2].append(lat.reshape(sb, st_len, LAT))
        win_all = jnp.concatenate([state_win_kv[l], kvw.reshape(sb, st_len, 2, g, NSA_DIM)], axis=1)
        st_s[3].append(win_all[:, st_len:])
    y_sample = x.reshape(sb, st_len, d)

    return (y_prompt, y_sample, jnp.stack(st_p[0]), jnp.stack(st_p[1]), jnp.stack(st_p[2]), jnp.stack(st_p[3]),
            jnp.stack(st_s[0]), jnp.stack(st_s[1]), jnp.stack(st_s[2]), jnp.stack(st_s[3]))
```

```python
import functools
import math

import numpy as np
import jax
import jax.numpy as jnp
from jax import lax
from jax.experimental import pallas as pl
from jax.experimental.pallas import tpu as pltpu

F32 = jnp.float32
BF16 = jnp.bfloat16

D_MODEL = 1024
PAGE_SIZE = 128
NSA_HEADS = 16
NSA_KV_HEADS = 2
NSA_GROUP = NSA_HEADS // NSA_KV_HEADS
NSA_DIM = D_MODEL // NSA_HEADS
CMP_BLOCK = 32
SLC_BLOCK = 64
N_SELECT = 16
WINDOW = 512
MLA_HEADS = 8
MLA_NOPE = 128
MLA_ROPE = 64
MLA_V = D_MODEL // MLA_HEADS
Q_LORA = 384
KV_LORA = 256
LAT = KV_LORA + MLA_ROPE
ROPE_THETA = 10000.0
N_BUCKETS = 32
MAX_DISTANCE = 128
NSA_SCALE = NSA_DIM ** -0.5
MLA_SCALE = (MLA_NOPE + MLA_ROPE) ** -0.5
NEG_BIG = -1e30
RMS_EPS = 1e-6
KV_ROW = 2 * NSA_KV_HEADS * NSA_DIM
GATE_LANES = 128

SEG_Q = (0, 1024)
SEG_CMP = (1024, 1280)
SEG_SLC = (1280, 1536)
SEG_WIN = (1536, 1792)
SEG_GATE = (1792, 2048)
SEG_CQ = (2048, 2432)
SEG_CKV = (2432, 2688)
SEG_KR = (2688, 2816)
SEG_GM = (2816, 4864)
D_IN_PACKED = 4864

ATT_TILE = 256
VMEM_LIMIT = 56 * 1024 * 1024
CMP_PAGES = 32
MLA_PAGES = 16


def _cparams(sem):
    return pltpu.CompilerParams(dimension_semantics=sem, vmem_limit_bytes=VMEM_LIMIT)


def _dot(a, b):
    return jnp.dot(a, b, preferred_element_type=F32)


def _dot_nt(a, b):
    return lax.dot_general(a, b, (((1,), (1,)), ((), ())), preferred_element_type=F32)


def _rms(x, g):
    return x * lax.rsqrt(jnp.mean(x * x, axis=-1, keepdims=True) + RMS_EPS) * g


def _rope_lanes(x, cos2, sin2):
    w = x.shape[-1]
    lane = lax.broadcasted_iota(jnp.int32, x.shape, 1)
    swapped = jnp.where(lane % MLA_ROPE < MLA_ROPE // 2, pltpu.roll(x, w - MLA_ROPE // 2, 1),
                        pltpu.roll(x, MLA_ROPE // 2, 1))
    return x * cos2 + swapped * sin2


def _t5_bucket_np(dist):
    max_exact = N_BUCKETS // 2
    d = np.maximum(dist, 0)
    log_ratio = np.log(np.maximum(d, 1).astype(np.float32) / max_exact) / math.log(MAX_DISTANCE / max_exact)
    large = np.minimum(max_exact + (log_ratio * (N_BUCKETS - max_exact)).astype(np.int32), N_BUCKETS - 1)
    return np.where(d < max_exact, d, large).astype(np.int32)


def _rope_tables(pos, reps):
    half = MLA_ROPE // 2
    inv = ROPE_THETA ** (-jnp.arange(half, dtype=F32) / half)
    ang = pos.astype(F32)[:, None] * inv[None, :]
    cos, sin = jnp.cos(ang), jnp.sin(ang)
    cos2 = jnp.concatenate([cos, cos], axis=-1)
    sin2 = jnp.concatenate([-sin, sin], axis=-1)
    return jnp.tile(cos2, (1, reps)), jnp.tile(sin2, (1, reps))


def _bias_kernel(table_ref, bucket_ref, mask_ref, o_ref, *, shift):
    h = pl.program_id(0)
    bucket = bucket_ref[...]
    acc = jnp.zeros(bucket.shape, F32)
    for b in range(N_BUCKETS):
        acc = jnp.where(bucket == b, table_ref[b, h], acc)
    if shift:
        acc = acc - table_ref[N_BUCKETS - 1, h]
    o_ref[0] = acc + mask_ref[...]


def _expand_bias(table, bucket, addmask, shift):
    r, c = bucket.shape
    return pl.pallas_call(
        functools.partial(_bias_kernel, shift=shift),
        grid=(NSA_HEADS,),
        in_specs=[pl.BlockSpec(memory_space=pltpu.SMEM),
                  pl.BlockSpec((r, c), lambda h: (0, 0)),
                  pl.BlockSpec((r, c), lambda h: (0, 0))],
        out_specs=pl.BlockSpec((1, r, c), lambda h: (h, 0, 0)),
        out_shape=jax.ShapeDtypeStruct((NSA_HEADS, r, c), F32),
        compiler_params=_cparams(("arbitrary",)),
        name="bias_expand",
    )(table, jnp.asarray(bucket), jnp.asarray(addmask))


def _mod_kernel(c_ref, w_ref, b_ref, o_ref):
    c = c_ref[...]
    cond = (c * jax.nn.sigmoid(c)).astype(BF16)
    o_ref[0] = _dot(cond, w_ref[0].astype(BF16)) + b_ref[0]


def _modulation(c_all, ada_w, ada_b):
    depth, d, n = ada_w.shape
    rows = c_all.shape[0]
    tn = 1536
    return pl.pallas_call(
        _mod_kernel,
        grid=(depth, n // tn),
        in_specs=[pl.BlockSpec((rows, d), lambda l, j: (0, 0)),
                  pl.BlockSpec((1, d, tn), lambda l, j: (l, 0, j)),
                  pl.BlockSpec((1, 1, tn), lambda l, j: (l, 0, j))],
        out_specs=pl.BlockSpec((1, rows, tn), lambda l, j: (l, 0, j)),
        out_shape=jax.ShapeDtypeStruct((depth, rows, n), F32),
        compiler_params=_cparams(("arbitrary", "arbitrary")),
        name="adaln_mod",
    )(c_all, ada_w, ada_b.reshape(depth, 1, n))


def _inproj_kernel(x_ref, mod_ref, gn_ref, w_ref, cos_ref, sin_ref, gq_ref, wuq_ref, wuk_ref, gkv_ref,
                   q_ref, kvc_ref, kvs_ref, kvw_ref, kst_ref, ksv_ref, kwt_ref, kwv_ref, gate_ref,
                   qmla_ref, lat_ref, latb_ref, latt_ref, gm_ref):
    x = x_ref[...]
    y = _rms(x, gn_ref[...])
    h = (y * (1.0 + mod_ref[:, 1, :]) + mod_ref[:, 0, :]).astype(BF16)

    def seg(s):
        return _dot(h, w_ref[:, s[0]:s[1]])

    q = (seg(SEG_Q) * NSA_SCALE).astype(BF16)
    for hd in range(NSA_HEADS):
        q_ref[0, hd] = q[:, hd * NSA_DIM:(hd + 1) * NSA_DIM]
    kvc_ref[...] = seg(SEG_CMP)
    half = NSA_KV_HEADS * NSA_DIM
    for s, kv_ref, kt_ref, v_ref in ((SEG_SLC, kvs_ref, kst_ref, ksv_ref), (SEG_WIN, kvw_ref, kwt_ref, kwv_ref)):
        kv = seg(s)
        kv_ref[...] = kv
        kt = kv[:, 0:half].T.astype(BF16)
        for g in range(NSA_KV_HEADS):
            kt_ref[0, g] = kt[g * NSA_DIM:(g + 1) * NSA_DIM]
            v_ref[0, g] = kv[:, half + g * NSA_DIM:half + (g + 1) * NSA_DIM].astype(BF16)
    gate_ref[...] = jax.nn.sigmoid(seg(SEG_GATE))
    gm_ref[...] = jax.nn.sigmoid(seg(SEG_GM))

    cos = cos_ref[...]
    sin = sin_ref[...]
    cqn = _rms(seg(SEG_CQ), gq_ref[...]).astype(BF16)
    qm = _dot(cqn, wuq_ref[...])
    nope_w = MLA_HEADS * MLA_NOPE
    qr = _rope_lanes(qm[:, nope_w:], cos, sin) * MLA_SCALE
    for hd in range(MLA_HEADS):
        qn = qm[:, hd * MLA_NOPE:(hd + 1) * MLA_NOPE].astype(BF16)
        qmla_ref[0, hd, :, 0:KV_LORA] = (_dot(qn, wuk_ref[hd]) * MLA_SCALE).astype(BF16)
        qmla_ref[0, hd, :, KV_LORA:LAT] = qr[:, hd * MLA_ROPE:(hd + 1) * MLA_ROPE].astype(BF16)
    ckv = _rms(seg(SEG_CKV), gkv_ref[...])
    kr = _rope_lanes(seg(SEG_KR), cos[:, 0:128], sin[:, 0:128])
    lat_ref[:, 0:KV_LORA] = ckv
    lat_ref[:, KV_LORA:LAT] = kr[:, 0:MLA_ROPE]
    latb_ref[...] = ckv.astype(BF16)
    latt_ref[0, 0:KV_LORA, :] = ckv.T.astype(BF16)
    latt_ref[0, KV_LORA:LAT, :] = kr.T[0:MLA_ROPE].astype(BF16)


def _inproj(x, mod, per_row_mod, lw, cos_t, sin_t, nb, nt, tm):
    m = nb * nt
    tpb = nt // tm
    d = D_MODEL
    if per_row_mod:
        mod_spec = pl.BlockSpec((tm, 6, d), lambda i: (i, 0, 0))
    else:
        mod_spec = pl.BlockSpec((1, 6, d), lambda i: (i // tpb, 0, 0))
    const2 = lambda i: (0, 0)
    row = lambda i: (i, 0)
    bt = lambda i: (i // tpb, 0, i % tpb, 0)
    btt = lambda i: (i // tpb, 0, 0, i % tpb)
    g = NSA_KV_HEADS
    out_shape = (
        jax.ShapeDtypeStruct((nb, NSA_HEADS, nt, NSA_DIM), BF16),
        jax.ShapeDtypeStruct((m, KV_ROW), F32),
        jax.ShapeDtypeStruct((m, KV_ROW), F32),
        jax.ShapeDtypeStruct((m, KV_ROW), F32),
        jax.ShapeDtypeStruct((nb, g, NSA_DIM, nt), BF16),
        jax.ShapeDtypeStruct((nb, g, nt, NSA_DIM), BF16),
        jax.ShapeDtypeStruct((nb, g, NSA_DIM, nt), BF16),
        jax.ShapeDtypeStruct((nb, g, nt, NSA_DIM), BF16),
        jax.ShapeDtypeStruct((m, g * GATE_LANES), F32),
        jax.ShapeDtypeStruct((nb, MLA_HEADS, nt, LAT), BF16),
        jax.ShapeDtypeStruct((m, LAT), F32),
        jax.ShapeDtypeStruct((m, KV_LORA), BF16),
        jax.ShapeDtypeStruct((nb, LAT, nt), BF16),
        jax.ShapeDtypeStruct((m, 2 * d), F32),
    )
    kt_spec = pl.BlockSpec((1, g, NSA_DIM, tm), btt)
    v_spec = pl.BlockSpec((1, g, tm, NSA_DIM), bt)
    out_specs = (
        pl.BlockSpec((1, NSA_HEADS, tm, NSA_DIM), bt),
        pl.BlockSpec((tm, KV_ROW), row), pl.BlockSpec((tm, KV_ROW), row), pl.BlockSpec((tm, KV_ROW), row),
        kt_spec, v_spec, kt_spec, v_spec,
        pl.BlockSpec((tm, g * GATE_LANES), row),
        pl.BlockSpec((1, MLA_HEADS, tm, LAT), bt),
        pl.BlockSpec((tm, LAT), row), pl.BlockSpec((tm, KV_LORA), row),
        pl.BlockSpec((1, LAT, tm), lambda i: (i // tpb, 0, i % tpb)),
        pl.BlockSpec((tm, 2 * d), row),
    )
    in_specs = [
        pl.BlockSpec((tm, d), row), mod_spec, pl.BlockSpec((1, d), const2),
        pl.BlockSpec((d, D_IN_PACKED), const2),
        pl.BlockSpec((tm, MLA_HEADS * MLA_ROPE), lambda i: (i % tpb, 0)),
        pl.BlockSpec((tm, MLA_HEADS * MLA_ROPE), lambda i: (i % tpb, 0)),
        pl.BlockSpec((1, Q_LORA), const2),
        pl.BlockSpec((Q_LORA, MLA_HEADS * (MLA_NOPE + MLA_ROPE)), const2),
        pl.BlockSpec((MLA_HEADS, MLA_NOPE, KV_LORA), lambda i: (0, 0, 0)),
        pl.BlockSpec((1, KV_LORA), const2),
    ]
    return pl.pallas_call(
        _inproj_kernel, grid=(m // tm,), in_specs=in_specs, out_specs=out_specs, out_shape=out_shape,
        compiler_params=_cparams(("arbitrary",)), name="in_proj",
    )(x, mod, lw["norm_mix_g"], lw["w_in"], cos_t, sin_t, lw["mla_q_norm_g"], lw["w_uq"], lw["w_ukT"],
      lw["mla_kv_norm_g"])


def _compress_kernel(tab_ref, *refs, pages, transposed):
    del tab_ref
    page_refs = refs[:pages]
    pe_ref, w1_ref, b1_ref, w2_ref, o_ref, slabk_ref, slabv_ref = refs[pages:]
    half = KV_ROW // 2
    for k in range(pages):
        if transposed:
            slabk_ref[k * PAGE_SIZE:(k + 1) * PAGE_SIZE, :] = page_refs[k][0, 0:half, :].T
            slabv_ref[k * PAGE_SIZE:(k + 1) * PAGE_SIZE, :] = page_refs[k][0, half:KV_ROW, :].T
        else:
            slabk_ref[k * PAGE_SIZE:(k + 1) * PAGE_SIZE, :] = page_refs[k][0, :, 0:half]
            slabv_ref[k * PAGE_SIZE:(k + 1) * PAGE_SIZE, :] = page_refs[k][0, :, half:KV_ROW]
    per_page = PAGE_SIZE // CMP_BLOCK
    acc = jnp.zeros((per_page * pages, KV_ROW), F32)
    for r in range(CMP_BLOCK):
        rows = [jnp.concatenate([slab[pl.ds(c * CMP_BLOCK + r, pages, stride=PAGE_SIZE), :]
                                 for slab in (slabk_ref, slabv_ref)], axis=1) for c in range(per_page)]
        xr = (jnp.concatenate(rows, axis=0) + pe_ref[r]).astype(BF16)
        acc = acc + _dot(xr, w1_ref[r])
    z = acc + b1_ref[...]
    hmid = 0.5 * z * (1.0 + jnp.tanh(math.sqrt(2.0 / math.pi) * (z + 0.044715 * (z * z * z))))
    o_ref[0] = _dot(hmid.astype(BF16), w2_ref[...])


def _compress(rows3d, page_ids, lw, transposed):
    n_logical = page_ids.shape[0]
    pages = min(CMP_PAGES, n_logical)
    assert n_logical % pages == 0
    steps = n_logical // pages
    per_page = PAGE_SIZE // CMP_BLOCK

    def page_map(k):
        return lambda s, tab: (tab[s * pages + k], 0, 0)

    page_block = (1, KV_ROW, PAGE_SIZE) if transposed else (1, PAGE_SIZE, KV_ROW)
    in_specs = [pl.BlockSpec(page_block, page_map(k)) for k in range(pages)]
    in_specs += [
        pl.BlockSpec((CMP_BLOCK, 1, KV_ROW), lambda s, tab: (0, 0, 0)),
        pl.BlockSpec((CMP_BLOCK, KV_ROW, KV_ROW), lambda s, tab: (0, 0, 0)),
        pl.BlockSpec((1, KV_ROW), lambda s, tab: (0, 0)),
        pl.BlockSpec((KV_ROW, KV_ROW), lambda s, tab: (0, 0)),
    ]
    out = pl.pallas_call(
        functools.partial(_compress_kernel, pages=pages, transposed=transposed),
        grid_spec=pltpu.PrefetchScalarGridSpec(
            num_scalar_prefetch=1, grid=(steps,), in_specs=in_specs,
            out_specs=pl.BlockSpec((1, per_page * pages, KV_ROW), lambda s, tab: (s, 0, 0)),
            scratch_shapes=[pltpu.VMEM((pages * PAGE_SIZE, KV_ROW // 2), F32),
                            pltpu.VMEM((pages * PAGE_SIZE, KV_ROW // 2), F32)]),
        out_shape=jax.ShapeDtypeStruct((steps, per_page * pages, KV_ROW), F32),
        compiler_params=_cparams(("arbitrary",)), name="cmp_compress",
    )(page_ids, *([rows3d] * pages), lw["cmp_pe"], lw["cmp_w1"], lw["cmp_b1"], lw["cmp_w2"])
    out = out.reshape(steps, per_page, pages, KV_ROW).transpose(0, 2, 1, 3)
    return out.reshape(n_logical * per_page, KV_ROW)


def _split_compressed(tok, nb):
    n = tok.shape[0] // nb
    t = tok.reshape(nb, n // 2, 2, 2, NSA_KV_HEADS, NSA_DIM)
    t = t.transpose(3, 0, 4, 2, 1, 5).reshape(2, nb, NSA_KV_HEADS, n, NSA_DIM).astype(BF16)
    return t[0], t[1]


def _cmp_prompt_kernel(q_ref, kc_ref, vc_ref, bias_ref, gate_ref, o_ref, sel_ref, *, tq, n_cmp):
    qi = pl.program_id(2)
    kc = kc_ref[0, 0]
    vc = vc_ref[0, 0]
    maskf = (bias_ref[0] > 0.5 * NEG_BIG).astype(F32)
    gates = gate_ref[0]
    imp = jnp.zeros((tq, n_cmp), F32)
    for hh in range(NSA_GROUP):
        z = _dot_nt(q_ref[0, hh], kc) + bias_ref[hh]
        e = jnp.exp(z - jnp.max(z, axis=-1, keepdims=True)) * maskf
        p = e / jnp.maximum(jnp.sum(e, axis=-1, keepdims=True), 1e-30)
        imp = imp + p
        o = _dot(p.astype(BF16), vc)
        o_ref[0, :, hh * NSA_DIM:(hh + 1) * NSA_DIM] = o * gates[:, hh:hh + 1]
    n_slc = n_cmp // 2
    imp_slc = imp[:, 0:n_slc] + imp[:, n_slc:n_cmp]
    t = qi * tq + lax.broadcasted_iota(jnp.int32, (tq, n_slc), 0)
    j = lax.broadcasted_iota(jnp.int32, (tq, n_slc), 1)
    cur = t // SLC_BLOCK
    forced = (j == 0) | (j == cur) | (j == cur - 1)
    score = jnp.where(forced, jnp.inf, jnp.where(j <= cur, imp_slc, -jnp.inf))
    rank = jnp.zeros((tq, n_slc), F32)
    for i in range(n_slc):
        ci = score[:, i:i + 1]
        beats = (ci > score) | ((ci == score) & (i < j))
        rank = rank + jnp.where(beats, 1.0, 0.0)
    sel_ref[0, 0] = jnp.where(rank < float(min(N_SELECT, n_slc)), 1.0, 0.0)


def _cmp_prompt(q, kc, vc, bias, gates, nb, nt):
    tq = ATT_TILE
    n_cmp = kc.shape[2]
    g = NSA_KV_HEADS
    gw = NSA_GROUP * NSA_DIM
    return pl.pallas_call(
        functools.partial(_cmp_prompt_kernel, tq=tq, n_cmp=n_cmp),
        grid=(nb, g, nt // tq),
        in_specs=[pl.BlockSpec((1, NSA_GROUP, tq, NSA_DIM), lambda b, gi, i: (b, gi, i, 0)),
                  pl.BlockSpec((1, 1, n_cmp, NSA_DIM), lambda b, gi, i: (b, gi, 0, 0)),
                  pl.BlockSpec((1, 1, n_cmp, NSA_DIM), lambda b, gi, i: (b, gi, 0, 0)),
                  pl.BlockSpec((NSA_GROUP, tq, n_cmp), lambda b, gi, i: (gi, i, 0)),
                  pl.BlockSpec((1, tq, GATE_LANES), lambda b, gi, i: (b, i, gi))],
        out_specs=(pl.BlockSpec((1, tq, gw), lambda b, gi, i: (b, i, gi)),
                   pl.BlockSpec((1, 1, tq, n_cmp // 2), lambda b, gi, i: (b, gi, i, 0))),
        out_shape=(jax.ShapeDtypeStruct((nb, nt, D_MODEL), F32),
                   jax.ShapeDtypeStruct((nb, g, nt, n_cmp // 2), F32)),
        compiler_params=_cparams(("arbitrary", "arbitrary", "arbitrary")), name="cmp_attn_select",
    )(q, kc, vc, bias, gates.reshape(nb, nt, g * GATE_LANES))


def _flash_update(s, vs, m_ref, l_ref, acc_ref):
    tk = s.shape[-1]
    m_old = m_ref[...]
    m_new = jnp.maximum(m_old, jnp.max(s, axis=-1, keepdims=True))
    alpha = jnp.exp(m_old - m_new)
    p = jnp.exp(s - pltpu.repeat(m_new, tk // 128, axis=1))
    psum = p[:, 0:128]
    for c in range(1, tk // 128):
        psum = psum + p[:, c * 128:(c + 1) * 128]
    l_ref[...] = alpha * l_ref[...] + psum
    dv = acc_ref.shape[-1]
    a = alpha[:, 0:dv] if dv <= 128 else pltpu.repeat(alpha, dv // 128, axis=1)
    acc_ref[...] = a * acc_ref[...] + _dot(p.astype(BF16), vs)
    m_ref[...] = m_new


def _nsa_flash_kernel(*refs, tq, slc, gate_col):
    if slc:
        q_ref, k_ref, v_ref, bias_ref, gate_ref, sel_ref, exp_ref, o_ref, m_ref, l_ref, acc_ref = refs
    else:
        q_ref, k_ref, v_ref, bias_ref, gate_ref, o_ref, m_ref, l_ref, acc_ref = refs
    qi = pl.program_id(2)
    rows = NSA_GROUP * tq
    m_ref[...] = jnp.full(m_ref.shape, NEG_BIG, F32)
    l_ref[...] = jnp.zeros(l_ref.shape, F32)
    acc_ref[...] = jnp.zeros(acc_ref.shape, F32)
    if slc:
        sel = sel_ref[0, 0].astype(BF16)

    def tile(kt, btype):
        start = pl.multiple_of(kt * tq, tq)
        kt_tile = k_ref[0, 0, :, pl.ds(start, tq)]
        vs = v_ref[0, 0, pl.ds(start, tq), :]
        s = _dot(q_ref[0].reshape(rows, NSA_DIM), kt_tile).reshape(NSA_GROUP, tq, tq)
        if btype is not None:
            s = s + bias_ref[:, btype]
        if slc:
            s = s + ((_dot(sel, exp_ref[kt]) - 1.0) * (-NEG_BIG))[None]
        _flash_update(s.reshape(rows, tq), vs, m_ref, l_ref, acc_ref)

    if slc:
        def far(kt, carry):
            tile(kt, None)
            return carry
        lax.fori_loop(0, jnp.maximum(qi - 1, 0), far, 0)
    else:
        @pl.when(qi >= 2)
        def _():
            tile(qi - 2, 2)

    @pl.when(qi >= 1)
    def _():
        tile(qi - 1, 1)

    tile(qi, 0)
    gates = gate_ref[0]
    o = acc_ref[...] / jnp.sum(l_ref[...], axis=-1, keepdims=True)
    for hh in range(NSA_GROUP):
        c = gate_col * NSA_GROUP + hh
        o_ref[0, :, hh * NSA_DIM:(hh + 1) * NSA_DIM] = o[hh * tq:(hh + 1) * tq] * gates[:, c:c + 1]


def _nsa_flash(q, kt, v, bias_tiles, gates, nb, nt, sel=None, expand=None):
    tq = ATT_TILE
    g = NSA_KV_HEADS
    gw = NSA_GROUP * NSA_DIM
    slc = sel is not None
    in_specs = [pl.BlockSpec((1, NSA_GROUP, tq, NSA_DIM), lambda b, gi, i: (b, gi, i, 0)),
                pl.BlockSpec((1, 1, NSA_DIM, nt), lambda b, gi, i: (b, gi, 0, 0)),
                pl.BlockSpec((1, 1, nt, NSA_DIM), lambda b, gi, i: (b, gi, 0, 0)),
                pl.BlockSpec((NSA_GROUP, 3, tq, tq), lambda b, gi, i: (gi, 0, 0, 0)),
                pl.BlockSpec((1, tq, GATE_LANES), lambda b, gi, i: (b, i, gi))]
    args = [q, kt, v, bias_tiles, gates.reshape(nb, nt, g * GATE_LANES)]
    if slc:
        n_slc = sel.shape[-1]
        in_specs += [pl.BlockSpec((1, 1, tq, n_slc), lambda b, gi, i: (b, gi, i, 0)),
                     pl.BlockSpec((nt // tq, n_slc, tq), lambda b, gi, i: (0, 0, 0))]
        args += [sel, expand]
    return pl.pallas_call(
        functools.partial(_nsa_flash_kernel, tq=tq, slc=slc, gate_col=1 if slc else 2),
        grid=(nb, g, nt // tq),
        in_specs=in_specs,
        out_specs=pl.BlockSpec((1, tq, gw), lambda b, gi, i: (b, i, gi)),
        out_shape=jax.ShapeDtypeStruct((nb, nt, D_MODEL), F32),
        scratch_shapes=[pltpu.VMEM((NSA_GROUP * tq, 128), F32), pltpu.VMEM((NSA_GROUP * tq, 128), F32),
                        pltpu.VMEM((NSA_GROUP * tq, NSA_DIM), F32)],
        compiler_params=_cparams(("arbitrary", "arbitrary", "arbitrary")),
        name="slc_attn" if slc else "win_attn",
    )(*args)


def _mla_prompt_kernel(q_ref, latt_ref, latv_ref, o_ref, m_ref, l_ref, acc_ref, *, tq):
    qi = pl.program_id(1)
    rows = MLA_HEADS * tq
    m_ref[...] = jnp.full(m_ref.shape, NEG_BIG, F32)
    l_ref[...] = jnp.zeros(l_ref.shape, F32)
    acc_ref[...] = jnp.zeros(acc_ref.shape, F32)

    def tile(kt, diag):
        start = pl.multiple_of(kt * tq, tq)
        s = _dot(q_ref[0].reshape(rows, LAT), latt_ref[0, :, pl.ds(start, tq)])
        if diag:
            row = lax.broadcasted_iota(jnp.int32, (tq, tq), 0)
            col = lax.broadcasted_iota(jnp.int32, (tq, tq), 1)
            s = jnp.where((col <= row)[None], s.reshape(MLA_HEADS, tq, tq), NEG_BIG).reshape(rows, tq)
        _flash_update(s, latv_ref[0, pl.ds(start, tq), :], m_ref, l_ref, acc_ref)

    def body(kt, carry):
        tile(kt, False)
        return carry
    lax.fori_loop(0, qi, body, 0)
    tile(qi, True)
    o = acc_ref[...] / jnp.sum(l_ref[...], axis=-1, keepdims=True)
    o_ref[0] = o.reshape(MLA_HEADS, tq, KV_LORA).astype(BF16)


def _mla_prompt(qmla, latt, latv, nb, nt):
    tq = ATT_TILE
    return pl.pallas_call(
        functools.partial(_mla_prompt_kernel, tq=tq),
        grid=(nb, nt // tq),
        in_specs=[pl.BlockSpec((1, MLA_HEADS, tq, LAT), lambda b, i: (b, 0, i, 0)),
                  pl.BlockSpec((1, LAT, nt), lambda b, i: (b, 0, 0)),
                  pl.BlockSpec((1, nt, KV_LORA), lambda b, i: (b, 0, 0))],
        out_specs=pl.BlockSpec((1, MLA_HEADS, tq, KV_LORA), lambda b, i: (b, 0, i, 0)),
        out_shape=jax.ShapeDtypeStruct((nb, MLA_HEADS, nt, KV_LORA), BF16),
        scratch_shapes=[pltpu.VMEM((MLA_HEADS * tq, 128), F32), pltpu.VMEM((MLA_HEADS * tq, 128), F32),
                        pltpu.VMEM((MLA_HEADS * tq, KV_LORA), F32)],
        compiler_params=_cparams(("arbitrary", "arbitrary")), name="mla_attn",
    )(qmla, latt, latv.reshape(nb, nt, KV_LORA))


def _merge_kernel(oc_ref, os_ref, ow_ref, ol_ref, gm_ref, x_ref, mod_ref, wuv_ref, wo_ref, o_ref):
    o_nsa = oc_ref[...] + os_ref[...] + ow_ref[...]
    o_mla = jnp.concatenate([_dot(ol_ref[0, hd], wuv_ref[hd]) for hd in range(MLA_HEADS)], axis=-1)
    gm = gm_ref[...]
    merged = (gm[:, 0:D_MODEL] * o_nsa + gm[:, D_MODEL:] * o_mla).astype(BF16)
    o_ref[...] = x_ref[...] + mod_ref[:, 2, :] * _dot(merged, wo_ref[...])


def _merge(o_cmp, o_slc, o_win, o_lat, gm, x, mod, per_row_mod, lw, nb, nt, tm):
    m = nb * nt
    tpb = nt // tm
    d = D_MODEL
    row = lambda i: (i, 0)
    if per_row_mod:
        mod_spec = pl.BlockSpec((tm, 6, d), lambda i: (i, 0, 0))
    else:
        mod_spec = pl.BlockSpec((1, 6, d), lambda i: (i // tpb, 0, 0))
    return pl.pallas_call(
        _merge_kernel, grid=(m // tm,),
        in_specs=[pl.BlockSpec((tm, d), row), pl.BlockSpec((tm, d), row), pl.BlockSpec((tm, d), row),
                  pl.BlockSpec((1, MLA_HEADS, tm, KV_LORA), lambda i: (i // tpb, 0, i % tpb, 0)),
                  pl.BlockSpec((tm, 2 * d), row), pl.BlockSpec((tm, d), row), mod_spec,
                  pl.BlockSpec((MLA_HEADS, KV_LORA, MLA_V), lambda i: (0, 0, 0)),
                  pl.BlockSpec((d, d), lambda i: (0, 0))],
        out_specs=pl.BlockSpec((tm, d), row),
        out_shape=jax.ShapeDtypeStruct((m, d), F32),
        compiler_params=_cparams(("arbitrary",)), name="merge_out_proj",
    )(o_cmp, o_slc, o_win, o_lat, gm, x, mod, lw["w_uv"], lw["w_out"])


def _ffn_kernel(x_ref, mod_ref, gn_ref, w1_ref, w3_ref, w2_ref, gf_ref, o_ref, h_ref, acc_ref, *, final):
    f = pl.program_id(1)

    @pl.when(f == 0)
    def _():
        y = _rms(x_ref[...], gn_ref[...])
        h_ref[...] = (y * (1.0 + mod_ref[:, 4, :]) + mod_ref[:, 3, :]).astype(BF16)
        acc_ref[...] = jnp.zeros(acc_ref.shape, F32)

    h = h_ref[...]
    a = _dot(h, w1_ref[...])
    b = _dot(h, w3_ref[...])
    act = (a * jax.nn.sigmoid(a) * b).astype(BF16)
    acc_ref[...] += _dot(act, w2_ref[...])

    @pl.when(f == pl.num_programs(1) - 1)
    def _():
        y = x_ref[...] + mod_ref[:, 5, :] * acc_ref[...]
        if final:
            y = _rms(y, gf_ref[...])
        o_ref[...] = y


def _ffn(x, mod, per_row_mod, lw, gfinal, final, nb, nt, tm):
    m = nb * nt
    tpb = nt // tm
    d = D_MODEL
    dff = lw["ffn_w2"].shape[0]
    tf = dff // 2
    nf = dff // tf
    row = lambda i, f: (i, 0)
    if per_row_mod:
        mod_spec = pl.BlockSpec((tm, 6, d), lambda i, f: (i, 0, 0))
    else:
        mod_spec = pl.BlockSpec((1, 6, d), lambda i, f: (i // tpb, 0, 0))
    return pl.pallas_call(
        functools.partial(_ffn_kernel, final=final), grid=(m // tm, nf),
        in_specs=[pl.BlockSpec((tm, d), row), mod_spec, pl.BlockSpec((1, d), lambda i, f: (0, 0)),
                  pl.BlockSpec((d, tf), lambda i, f: (0, f)),
                  pl.BlockSpec((d, tf), lambda i, f: (0, nf + f)),
                  pl.BlockSpec((tf, d), lambda i, f: (f, 0)),
                  pl.BlockSpec((1, d), lambda i, f: (0, 0))],
        out_specs=pl.BlockSpec((tm, d), row),
        out_shape=jax.ShapeDtypeStruct((m, d), F32),
        scratch_shapes=[pltpu.VMEM((tm, d), BF16), pltpu.VMEM((tm, d), F32)],
        compiler_params=_cparams(("arbitrary", "arbitrary")), name="ffn",
    )(x, mod, lw["norm_ffn_g"], lw["ffn_w13"], lw["ffn_w13"], lw["ffn_w2"], gfinal)


def _cmp_sample_kernel(q_ref, kc_ref, vc_ref, bias_ref, gate_ref, o_ref, imp_ref, *, nt, n_cmp):
    rows = nt * NSA_GROUP
    n_past = n_cmp // 2
    for g in range(NSA_KV_HEADS):
        qg = q_ref[0, :, g].reshape(rows, NSA_DIM)
        z = _dot_nt(qg, kc_ref[0, g]) + bias_ref[g]
        maskf = (bias_ref[g] > 0.5 * NEG_BIG).astype(F32)
        e = jnp.exp(z - jnp.max(z, axis=-1, keepdims=True)) * maskf
        p = e / jnp.maximum(jnp.sum(e, axis=-1, keepdims=True), 1e-30)
        o = _dot(p.astype(BF16), vc_ref[0, g]) * gate_ref[0, g, 0]
        o_ref[0, :, g] = o.reshape(nt, NSA_GROUP, NSA_DIM)
        imp = jnp.sum(p.reshape(nt, NSA_GROUP, n_cmp), axis=1)
        imp_ref[0, g] = imp[:, 0:n_past] + imp[:, n_past:n_cmp]


def _topk_sample_kernel(imp_ref, idx_ref, *, n_pick):
    imp = imp_ref[...]
    rows, n_past = imp.shape
    lane = lax.broadcasted_iota(jnp.int32, (rows, n_past), 1)
    score = jnp.where((lane == 0) | (lane == n_past - 1), jnp.inf, imp)

    def body(i, rank):
        ci = jnp.sum(jnp.where(lane == i, score, 0.0), axis=-1, keepdims=True)
        beats = (ci > score) | ((ci == score) & (i < lane))
        return rank + jnp.where(beats, 1, 0)
    rank = lax.fori_loop(0, n_past, body, jnp.zeros((rows, n_past), jnp.int32))
    out_lane = lax.broadcasted_iota(jnp.int32, (rows, 128), 1)
    picked = jnp.zeros((rows, 128), jnp.int32)
    for r in range(n_pick):
        ir = jnp.sum(jnp.where(rank == r, lane, 0), axis=-1, keepdims=True)
        picked = jnp.where(out_lane == r, ir, picked)
    idx_ref[...] = picked


def _topk_sample(imp2d, n_pick):
    rows, n_past = imp2d.shape
    return pl.pallas_call(
        functools.partial(_topk_sample_kernel, n_pick=n_pick),
        grid=(1,),
        in_specs=[pl.BlockSpec((rows, n_past), lambda i: (0, 0))],
        out_specs=pl.BlockSpec((rows, 128), lambda i: (0, 0)),
        out_shape=jax.ShapeDtypeStruct((rows, 128), jnp.int32),
        compiler_params=_cparams(("arbitrary",)), name="topk_blocks_sample",
    )(imp2d)


def _cmp_sample(q5, kc, vc, bias, gate_cols, nb, nt):
    n_cmp = kc.shape[2]
    g = NSA_KV_HEADS
    rows = nt * NSA_GROUP
    return pl.pallas_call(
        functools.partial(_cmp_sample_kernel, nt=nt, n_cmp=n_cmp),
        grid=(nb,),
        in_specs=[pl.BlockSpec((1, nt, g, NSA_GROUP, NSA_DIM), lambda b: (b, 0, 0, 0, 0)),
                  pl.BlockSpec((1, g, n_cmp, NSA_DIM), lambda b: (b, 0, 0, 0)),
                  pl.BlockSpec((1, g, n_cmp, NSA_DIM), lambda b: (b, 0, 0, 0)),
                  pl.BlockSpec((g, rows, n_cmp), lambda b: (0, 0, 0)),
                  pl.BlockSpec((1, g, 1, rows, 1), lambda b: (b, 0, 0, 0, 0))],
        out_specs=(pl.BlockSpec((1, nt, g, NSA_GROUP, NSA_DIM), lambda b: (b, 0, 0, 0, 0)),
                   pl.BlockSpec((1, g, nt, n_cmp // 2), lambda b: (b, 0, 0, 0))),
        out_shape=(jax.ShapeDtypeStruct((nb, nt, g, NSA_GROUP, NSA_DIM), F32),
                   jax.ShapeDtypeStruct((nb, g, nt, n_cmp // 2), F32)),
        compiler_params=_cparams(("arbitrary",)), name="cmp_attn_sample",
    )(q5, kc, vc, bias, gate_cols)


def _slc_sample_kernel(idx_ref, pt_ref, *refs, nt, n_pick, n_past, n_pages):
    del pt_ref, n_pages
    nblk = NSA_KV_HEADS * n_pick
    blk_refs = refs[:nblk]
    q_ref, new_ref, bias_ref, biasc_ref, gate_ref, o_ref = refs[nblk:]
    b = pl.program_id(0)
    t = pl.program_id(1)
    half = NSA_KV_HEADS * NSA_DIM
    lane_half = lax.broadcasted_iota(jnp.int32, (NSA_GROUP, PAGE_SIZE), 1) // SLC_BLOCK
    for g in range(NSA_KV_HEADS):
        qg = q_ref[0, 0, g]
        s_list, vt_list = [], []
        for n in range(n_pick):
            page = blk_refs[g * n_pick + n]
            j = idx_ref[((b * nt + t) * NSA_KV_HEADS + g) * n_pick + n]
            near = jnp.clip(j - (n_past - 3), 0, 2)
            kt = page[0, g * NSA_DIM:(g + 1) * NSA_DIM, :].astype(BF16)
            s_list.append(_dot(qg, kt) + jnp.where(lane_half == (j & 1), bias_ref[0, g, near], NEG_BIG))
            vt_list.append(page[0, half + g * NSA_DIM:half + (g + 1) * NSA_DIM, :].astype(BF16))
        new = new_ref[0]
        s_new = _dot_nt(qg, new[:, g * NSA_DIM:(g + 1) * NSA_DIM].astype(BF16)) + biasc_ref[0, g]
        m = s_new.max(axis=-1, keepdims=True)
        for s in s_list:
            m = jnp.maximum(m, s.max(axis=-1, keepdims=True))
        p = jnp.exp(s_new - m)
        l = jnp.sum(p, axis=-1, keepdims=True)
        acc = _dot(p.astype(BF16), new[:, half + g * NSA_DIM:half + (g + 1) * NSA_DIM].astype(BF16))
        for s, vt in zip(s_list, vt_list):
            p = jnp.exp(s - m)
            l = l + jnp.sum(p, axis=-1, keepdims=True)
            acc = acc + _dot_nt(p.astype(BF16), vt)
        o_ref[0, 0, g] = acc / l * gate_ref[0, g, 0, 0]


def _slc_sample(idx_flat, pt_flat, cache_blocks, q5, new_rows, bias_near, bias_cur, gate_cols, nb, nt, n_pick,
                n_past, n_pages):
    g = NSA_KV_HEADS

    def blk_map(gi, n):
        def f(b, t, idx, pt):
            j = idx[((b * nt + t) * g + gi) * n_pick + n]
            return (pt[b * n_pages + (j >> 1)], 0, 0)
        return f

    in_specs = [pl.BlockSpec((1, KV_ROW, PAGE_SIZE), blk_map(gi, n)) for gi in range(g) for n in range(n_pick)]
    in_specs += [
        pl.BlockSpec((1, 1, g, NSA_GROUP, NSA_DIM), lambda b, t, idx, pt: (b, t, 0, 0, 0)),
        pl.BlockSpec((1, 8, KV_ROW), lambda b, t, idx, pt: (b, 0, 0)),
        pl.BlockSpec((1, g, 3, NSA_GROUP, PAGE_SIZE), lambda b, t, idx, pt: (t, 0, 0, 0, 0)),
        pl.BlockSpec((1, g, NSA_GROUP, 8), lambda b, t, idx, pt: (t, 0, 0, 0)),
        pl.BlockSpec((1, g, 1, 1, NSA_GROUP, 1), lambda b, t, idx, pt: (b, 0, 0, t, 0, 0)),
    ]
    return pl.pallas_call(
        functools.partial(_slc_sample_kernel, nt=nt, n_pick=n_pick, n_past=n_past, n_pages=n_pages),
        grid_spec=pltpu.PrefetchScalarGridSpec(
            num_scalar_prefetch=2, grid=(nb, nt), in_specs=in_specs,
            out_specs=pl.BlockSpec((1, 1, g, NSA_GROUP, NSA_DIM), lambda b, t, idx, pt: (b, t, 0, 0, 0))),
        out_shape=jax.ShapeDtypeStruct((nb, nt, g, NSA_GROUP, NSA_DIM), F32),
        compiler_params=_cparams(("arbitrary", "arbitrary")), name="slc_attn_sample",
    )(idx_flat, pt_flat, *([cache_blocks] * (g * n_pick)), q5, new_rows, bias_near, bias_cur, gate_cols)


def _win_sample_kernel(q_ref, buf_ref, new_ref, bias_ref, biasn_ref, gate_ref, o_ref, *, nt):
    rows = nt * NSA_GROUP
    half = NSA_KV_HEADS * NSA_DIM
    buf = buf_ref[0, 0]
    new = new_ref[0]
    for g in range(NSA_KV_HEADS):
        qg = q_ref[0, :, g].reshape(rows, NSA_DIM)
        s1 = _dot(qg, buf[g * NSA_DIM:(g + 1) * NSA_DIM, :].astype(BF16)) + bias_ref[g]
        s2 = _dot_nt(qg, new[:, g * NSA_DIM:(g + 1) * NSA_DIM].astype(BF16)) + biasn_ref[g]
        m = jnp.maximum(s1.max(axis=-1, keepdims=True), s2.max(axis=-1, keepdims=True))
        p1 = jnp.exp(s1 - m)
        p2 = jnp.exp(s2 - m)
        l = jnp.sum(p1, axis=-1, keepdims=True) + jnp.sum(p2, axis=-1, keepdims=True)
        acc = _dot_nt(p1.astype(BF16), buf[half + g * NSA_DIM:half + (g + 1) * NSA_DIM, :].astype(BF16))
        acc = acc + _dot(p2.astype(BF16), new[:, half + g * NSA_DIM:half + (g + 1) * NSA_DIM].astype(BF16))
        o_ref[0, :, g] = (acc / l * gate_ref[0, g, 0]).reshape(nt, NSA_GROUP, NSA_DIM)


def _win_sample(q5, win_state, layer, new_rows, bias_buf, bias_new, gate_cols, nb, nt):
    g = NSA_KV_HEADS
    rows = nt * NSA_GROUP
    wlen = win_state.shape[3]
    return pl.pallas_call(
        functools.partial(_win_sample_kernel, nt=nt),
        grid=(nb,),
        in_specs=[pl.BlockSpec((1, nt, g, NSA_GROUP, NSA_DIM), lambda b: (b, 0, 0, 0, 0)),
                  pl.BlockSpec((1, 1, KV_ROW, wlen), lambda b: (layer, b, 0, 0)),
                  pl.BlockSpec((1, 8, KV_ROW), lambda b: (b, 0, 0)),
                  pl.BlockSpec((g, rows, wlen), lambda b: (0, 0, 0)),
                  pl.BlockSpec((g, rows, 8), lambda b: (0, 0, 0)),
                  pl.BlockSpec((1, g, 1, rows, 1), lambda b: (b, 0, 2, 0, 0))],
        out_specs=pl.BlockSpec((1, nt, g, NSA_GROUP, NSA_DIM), lambda b: (b, 0, 0, 0, 0)),
        out_shape=jax.ShapeDtypeStruct((nb, nt, g, NSA_GROUP, NSA_DIM), F32),
        compiler_params=_cparams(("arbitrary",)), name="win_attn_sample",
    )(q5, win_state, new_rows, bias_buf, bias_new, gate_cols)


def _mla_sample_kernel(pt_ref, *refs, pages, nt):
    del pt_ref
    page_refs = refs[:pages]
    q_ref, new_ref, o_ref, m_ref, l_ref, acc_ref = refs[pages:]
    step = pl.program_id(1)
    rows = MLA_HEADS * nt

    @pl.when(step == 0)
    def _():
        m_ref[...] = jnp.full(m_ref.shape, NEG_BIG, F32)
        l_ref[...] = jnp.zeros(l_ref.shape, F32)
        acc_ref[...] = jnp.zeros(acc_ref.shape, F32)

    q = q_ref[0]
    ks = [page_refs[k][0].astype(BF16) for k in range(pages)]
    ss = [_dot(q, kk) for kk in ks]
    m_old = m_ref[...]
    m_new = m_old
    for s in ss:
        m_new = jnp.maximum(m_new, s.max(axis=-1, keepdims=True))
    alpha = jnp.exp(m_old - m_new)
    l = alpha * l_ref[...]
    acc = alpha * acc_ref[...]
    for s, kk in zip(ss, ks):
        p = jnp.exp(s - m_new)
        l = l + jnp.sum(p, axis=-1, keepdims=True)
        acc = acc + _dot_nt(p.astype(BF16), kk[0:KV_LORA, :])
    m_ref[...] = m_new
    l_ref[...] = l
    acc_ref[...] = acc

    @pl.when(step == pl.num_programs(1) - 1)
    def _():
        new = new_ref[0].astype(BF16)
        tq = lax.broadcasted_iota(jnp.int32, (rows, 8), 0) % nt
        tk = lax.broadcasted_iota(jnp.int32, (rows, 8), 1)
        s = jnp.where(tk <= tq, _dot_nt(q, new), NEG_BIG)
        m_o = m_ref[...]
        m_n = jnp.maximum(m_o, s.max(axis=-1, keepdims=True))
        a = jnp.exp(m_o - m_n)
        p = jnp.exp(s - m_n)
        lf = a * l_ref[...] + jnp.sum(p, axis=-1, keepdims=True)
        accf = a * acc_ref[...] + _dot(p.astype(BF16), new[:, 0:KV_LORA])
        o_ref[0] = (accf / lf).astype(BF16)


def _mla_sample(pt_flat, cache_pages, q_rows, new_rows, nb, nt, n_pages):
    pages = MLA_PAGES
    steps = n_pages // pages
    rows = MLA_HEADS * nt

    def page_map(k):
        return lambda b, s, pt: (pt[b * n_pages + s * pages + k], 0, 0)

    in_specs = [pl.BlockSpec((1, LAT, PAGE_SIZE), page_map(k)) for k in range(pages)]
    in_specs += [pl.BlockSpec((1, rows, LAT), lambda b, s, pt: (b, 0, 0)),
                 pl.BlockSpec((1, 8, LAT), lambda b, s, pt: (b, 0, 0))]
    return pl.pallas_call(
        functools.partial(_mla_sample_kernel, pages=pages, nt=nt),
        grid_spec=pltpu.PrefetchScalarGridSpec(
            num_scalar_prefetch=1, grid=(nb, steps), in_specs=in_specs,
            out_specs=pl.BlockSpec((1, rows, KV_LORA), lambda b, s, pt: (b, 0, 0)),
            scratch_shapes=[pltpu.VMEM((rows, 1), F32), pltpu.VMEM((rows, 1), F32),
                            pltpu.VMEM((rows, KV_LORA), F32)]),
        out_shape=jax.ShapeDtypeStruct((nb, rows, KV_LORA), BF16),
        compiler_params=_cparams(("arbitrary", "arbitrary")), name="mla_attn_sample",
    )(pt_flat, *([cache_pages] * pages), q_rows, new_rows)


def _pack_layer(l, w):
    d = D_MODEL
    w_in = w["w_in"][l]
    sizes = (NSA_HEADS * NSA_DIM, KV_ROW, KV_ROW, KV_ROW, 3 * NSA_HEADS, Q_LORA, KV_LORA, MLA_ROPE, 2 * d)
    offs = np.concatenate([[0], np.cumsum(sizes)])
    seg = [w_in[:, offs[i]:offs[i + 1]] for i in range(len(sizes))]
    gsrc = seg[4].reshape(d, NSA_KV_HEADS, NSA_GROUP, 3).transpose(0, 1, 3, 2).reshape(d, NSA_KV_HEADS, 3 * NSA_GROUP)
    gates = jnp.pad(gsrc, ((0, 0), (0, 0), (0, GATE_LANES - 3 * NSA_GROUP))).reshape(d, NSA_KV_HEADS * GATE_LANES)
    kr = jnp.pad(seg[7], ((0, 0), (0, 128 - MLA_ROPE)))
    w_packed = jnp.concatenate([seg[0], seg[1], seg[2], seg[3], gates, seg[5], seg[6], kr, seg[8]], axis=1)
    wuq = w["mla_w_uq"][l].reshape(Q_LORA, MLA_HEADS, MLA_NOPE + MLA_ROPE)
    wuq = jnp.concatenate([wuq[:, :, :MLA_NOPE].reshape(Q_LORA, -1), wuq[:, :, MLA_NOPE:].reshape(Q_LORA, -1)], axis=1)
    w1 = w["nsa_cmp_w1"][l].reshape(2, CMP_BLOCK, NSA_DIM, NSA_DIM)
    eye_g = jnp.eye(NSA_KV_HEADS, dtype=F32)
    eye_i = jnp.eye(2, dtype=F32)
    w1big = jnp.einsum("irdo,ij,gh->rigdjho", w1, eye_i, eye_g).reshape(CMP_BLOCK, KV_ROW, KV_ROW)
    w2big = jnp.einsum("ido,ij,gh->igdjho", w["nsa_cmp_w2"][l], eye_i, eye_g).reshape(KV_ROW, KV_ROW)
    pe = w["nsa_cmp_pe"][l]
    pe_big = jnp.broadcast_to(pe.transpose(1, 0, 2)[:, :, None, :], (CMP_BLOCK, 2, NSA_KV_HEADS, NSA_DIM))
    b1big = jnp.broadcast_to(w["nsa_cmp_b1"][l][:, None, :], (2, NSA_KV_HEADS, NSA_DIM))
    return dict(
        norm_mix_g=w["norm_mix_g"][l].reshape(1, d),
        w_in=w_packed.astype(BF16),
        mla_q_norm_g=w["mla_q_norm_g"][l].reshape(1, Q_LORA),
        w_uq=wuq.astype(BF16),
        w_ukT=w["mla_w_uk"][l].transpose(1, 2, 0).astype(BF16),
        mla_kv_norm_g=w["mla_kv_norm_g"][l].reshape(1, KV_LORA),
        w_uv=w["mla_w_uv"][l].transpose(1, 0, 2).astype(BF16),
        w_out=w["w_out"][l].astype(BF16),
        norm_ffn_g=w["norm_ffn_g"][l].reshape(1, d),
        ffn_w13=w["ffn_w13"][l].astype(BF16),
        ffn_w2=w["ffn_w2"][l].astype(BF16),
        cmp_pe=pe_big.reshape(CMP_BLOCK, 1, KV_ROW),
        cmp_w1=w1big.astype(BF16),
        cmp_b1=b1big.reshape(1, KV_ROW),
        cmp_w2=w2big.astype(BF16),
    )


def _cmp_block_ends(n_cmp):
    order = np.concatenate([np.arange(0, n_cmp, 2), np.arange(1, n_cmp, 2)])
    return (order + 1) * CMP_BLOCK - 1


def _prompt_bias_tables(table, nt):
    tq = ATT_TILE
    i = np.arange(tq)[:, None]
    j = np.arange(tq)[None, :]
    d0, d1, d2 = i - j, tq + i - j, 2 * tq + i - j
    bucket = np.concatenate([_t5_bucket_np(d0), _t5_bucket_np(d1), _t5_bucket_np(d2)], axis=0)
    mask = np.concatenate([np.where(d0 >= 0, 0.0, NEG_BIG), np.zeros((tq, tq)),
                           np.where(d2 < WINDOW, 0.0, NEG_BIG)], axis=0).astype(np.float32)
    tiles = _expand_bias(table, bucket, mask, True).reshape(NSA_HEADS, 3, tq, tq)
    n_cmp = nt // CMP_BLOCK
    dist = np.arange(nt)[:, None] - _cmp_block_ends(n_cmp)[None, :]
    cmp_bias = _expand_bias(table, _t5_bucket_np(dist), np.where(dist >= 0, 0.0, NEG_BIG).astype(np.float32), False)
    n_slc = nt // SLC_BLOCK
    expand = (np.arange(n_slc)[None, :, None] ==
              (np.arange(nt // tq)[:, None, None] * tq + np.arange(tq)[None, None, :]) // SLC_BLOCK)
    return tiles, cmp_bias, jnp.asarray(expand.astype(np.float32), dtype=BF16)


def _sample_bias_tables(table, nt, past, wlen):
    g, hg = NSA_KV_HEADS, NSA_GROUP
    rows = nt * hg
    q_pos = past + np.arange(nt)

    def per_group(b, width):
        return b.reshape(g, hg, nt, width).transpose(0, 2, 1, 3).reshape(g, rows, width)

    n_cmp = past // CMP_BLOCK
    dist = q_pos[:, None] - _cmp_block_ends(n_cmp)[None, :]
    cmp_bias = per_group(_expand_bias(table, _t5_bucket_np(dist),
                                      np.where(dist >= 0, 0.0, NEG_BIG).astype(np.float32), False), n_cmp)
    dist = q_pos[:, None] - (past - wlen + np.arange(wlen))[None, :]
    ok = (dist >= 0) & (dist < WINDOW)
    win_bias = per_group(_expand_bias(table, _t5_bucket_np(dist), np.where(ok, 0.0, NEG_BIG).astype(np.float32),
                                      False), wlen)
    dist = q_pos[:, None] - (past + np.arange(8))[None, :]
    ok = (dist >= 0) & (dist < WINDOW) & (np.arange(8)[None, :] < nt)
    new_mask = np.where(ok, 0.0, NEG_BIG).astype(np.float32)
    win_new = per_group(_expand_bias(table, _t5_bucket_np(dist), new_mask, False), 8)
    n_past = past // SLC_BLOCK
    near = np.stack([np.full((nt, SLC_BLOCK), 10 * MAX_DISTANCE),
                     q_pos[:, None] - ((n_past - 2) * SLC_BLOCK + np.arange(SLC_BLOCK))[None, :],
                     q_pos[:, None] - ((n_past - 1) * SLC_BLOCK + np.arange(SLC_BLOCK))[None, :]], axis=1)
    near = near.reshape(nt, 3 * SLC_BLOCK)
    slc_near = _expand_bias(table, _t5_bucket_np(near), np.zeros(near.shape, np.float32), True)
    slc_near = slc_near.reshape(g, hg, nt, 3, SLC_BLOCK).transpose(2, 0, 3, 1, 4)
    slc_near = jnp.tile(slc_near, (1, 1, 1, 1, PAGE_SIZE // SLC_BLOCK))
    slc_cur = _expand_bias(table, _t5_bucket_np(dist), np.where((dist >= 0) & (np.arange(8)[None, :] < nt), 0.0,
                                                                 NEG_BIG).astype(np.float32), True)
    slc_cur = slc_cur.reshape(g, hg, nt, 8).transpose(2, 0, 1, 3)
    return cmp_bias, win_bias, win_new, slc_near, slc_cur


def kernel(x_prompt, x_sample, cache_cmp_kv, cache_slc_kv, cache_mla, state_win_kv, page_table, c_prompt, c_sample,
           rel_bias_table, ada_w, ada_b, norm_mix_g, w_in, nsa_cmp_pe, nsa_cmp_w1, nsa_cmp_b1, nsa_cmp_w2,
           mla_q_norm_g, mla_w_uq, mla_kv_norm_g, mla_w_uk, mla_w_uv, w_out, norm_ffn_g, ffn_w13, ffn_w2,
           final_norm_g):
    weights = dict(norm_mix_g=norm_mix_g, w_in=w_in, nsa_cmp_pe=nsa_cmp_pe, nsa_cmp_w1=nsa_cmp_w1,
                   nsa_cmp_b1=nsa_cmp_b1, nsa_cmp_w2=nsa_cmp_w2, mla_q_norm_g=mla_q_norm_g, mla_w_uq=mla_w_uq,
                   mla_kv_norm_g=mla_kv_norm_g, mla_w_uk=mla_w_uk, mla_w_uv=mla_w_uv, w_out=w_out,
                   norm_ffn_g=norm_ffn_g, ffn_w13=ffn_w13, ffn_w2=ffn_w2)
    depth = ada_w.shape[0]
    d = D_MODEL
    g = NSA_KV_HEADS
    pb, pt_len, _ = x_prompt.shape
    sb, st_len, _ = x_sample.shape
    n_pool = cache_cmp_kv.shape[1]
    n_pages = page_table.shape[1]
    past = n_pages * PAGE_SIZE
    wlen = state_win_kv.shape[2]
    n_past_blocks = past // SLC_BLOCK
    n_pick = N_SELECT - 1
    assert pt_len % ATT_TILE == 0 and pt_len >= WINDOW and st_len <= 8 and past >= wlen
    assert n_past_blocks > n_pick and n_pages % MLA_PAGES == 0

    layers = [_pack_layer(l, weights) for l in range(depth)]
    gfinal = final_norm_g.reshape(1, d)
    mod_all = _modulation(jnp.concatenate([c_prompt, c_sample], axis=0), ada_w, ada_b)

    tiles, cmp_bias_p, expand = _prompt_bias_tables(rel_bias_table, pt_len)
    cmp_bias_s, win_bias_s, win_new_s, slc_near_s, slc_cur_s = _sample_bias_tables(rel_bias_table, st_len, past, wlen)
    cos_p, sin_p = _rope_tables(jnp.arange(pt_len), MLA_HEADS)
    cos_s, sin_s = _rope_tables(jnp.tile(past + jnp.arange(st_len), sb), MLA_HEADS)
    prompt_pages = jnp.arange(pb * pt_len // PAGE_SIZE, dtype=jnp.int32)
    pt_flat = page_table.reshape(-1).astype(jnp.int32)

    x = x_prompt.reshape(pb * pt_len, d)
    st_p = [[], [], [], []]
    tm = 256
    for l in range(depth):
        lw = layers[l]
        mod = mod_all[l, :pb].reshape(pb, 6, d)
        (q, kvc, kvs, kvw, kst, ksv, kwt, kwv, gates, qmla, lat, latv, latt, gm) = _inproj(
            x, mod, False, lw, cos_p, sin_p, pb, pt_len, tm)
        tok = _compress(kvc.reshape(-1, PAGE_SIZE, KV_ROW), prompt_pages, lw, False)
        kc, vc = _split_compressed(tok, pb)
        o_cmp, sel = _cmp_prompt(q, kc, vc, cmp_bias_p, gates, pb, pt_len)
        o_slc = _nsa_flash(q, kst, ksv, tiles, gates, pb, pt_len, sel=sel, expand=expand)
        o_win = _nsa_flash(q, kwt, kwv, tiles, gates, pb, pt_len)
        o_lat = _mla_prompt(qmla, latt, latv, pb, pt_len)
        x = _merge(o_cmp.reshape(-1, d), o_slc.reshape(-1, d), o_win.reshape(-1, d), o_lat, gm, x, mod, False, lw,
                   pb, pt_len, tm)
        x = _ffn(x, mod, False, lw, gfinal, l == depth - 1, pb, pt_len, 512)
        st_p[0].append(kvc.reshape(pb, pt_len, 2, g, NSA_DIM))
        st_p[1].append(kvs.reshape(pb, pt_len, 2, g, NSA_DIM))
        st_p[2].append(lat.reshape(pb, pt_len, LAT))
        st_p[3].append(kvw.reshape(pb, pt_len, 2, g, NSA_DIM)[:, pt_len - min(WINDOW, pt_len):])
    y_prompt = x.reshape(pb, pt_len, d)

    ms = sb * st_len
    x = x_sample.reshape(ms, d)
    st_s = [[], [], [], []]
    cmp_pages = cache_cmp_kv.transpose(0, 1, 3, 4, 5, 2).reshape(depth * n_pool, KV_ROW, PAGE_SIZE)
    slc_pages = cache_slc_kv.transpose(0, 1, 3, 4, 5, 2).reshape(depth * n_pool, KV_ROW, PAGE_SIZE)
    mla_pages = cache_mla.transpose(0, 1, 3, 2).reshape(depth * n_pool, LAT, PAGE_SIZE)
    win_state = state_win_kv.transpose(0, 1, 3, 4, 5, 2).reshape(depth, sb, KV_ROW, wlen)
    for l in range(depth):
        lw = layers[l]
        mod = jnp.repeat(mod_all[l, pb:], st_len, axis=0).reshape(ms, 6, d)
        (q, kvc, kvs, kvw, _, _, _, _, gates, qmla, lat, _, _, gm) = _inproj(
            x, mod, True, lw, cos_s, sin_s, 1, ms, ms)
        pt_l = pt_flat + l * n_pool
        tok = _compress(cmp_pages, pt_l, lw, True)
        kc, vc = _split_compressed(tok, sb)
        q5 = q.reshape(g, NSA_GROUP, sb, st_len, NSA_DIM).transpose(2, 3, 0, 1, 4)
        gate_cols = gates.reshape(sb, st_len, g, GATE_LANES)[..., :3 * NSA_GROUP]
        gate_cols = gate_cols.reshape(sb, st_len, g, 3, NSA_GROUP).transpose(0, 2, 3, 1, 4)
        gate_rows = gate_cols.reshape(sb, g, 3, st_len * NSA_GROUP, 1)
        o_cmp, imp = _cmp_sample(q5, kc, vc, cmp_bias_s, gate_rows, sb, st_len)
        idx = _topk_sample(imp.reshape(sb * g * st_len, n_past_blocks), n_pick)
        idx_flat = idx[:, :n_pick].reshape(sb, g, st_len, n_pick).transpose(0, 2, 1, 3).reshape(-1)
        pad8 = lambda a: jnp.pad(a.reshape(sb, st_len, -1), ((0, 0), (0, 8 - st_len), (0, 0)))
        o_slc = _slc_sample(idx_flat, pt_l, slc_pages, q5, pad8(kvs), slc_near_s, slc_cur_s,
                            gate_cols.reshape(sb, g, 3, st_len, NSA_GROUP, 1)[:, :, 1:2], sb, st_len, n_pick,
                            n_past_blocks, n_pages)
        o_win = _win_sample(q5, win_state, l, pad8(kvw), win_bias_s, win_new_s, gate_rows, sb, st_len)
        q_rows = qmla.reshape(MLA_HEADS, sb, st_len, LAT).transpose(1, 0, 2, 3).reshape(sb, MLA_HEADS * st_len, LAT)
        o_lat = _mla_sample(pt_l, mla_pages, q_rows, pad8(lat), sb, st_len, n_pages)
        o_lat = o_lat.reshape(sb, MLA_HEADS, st_len, KV_LORA).transpose(1, 0, 2, 3).reshape(1, MLA_HEADS, ms, KV_LORA)
        x = _merge(o_cmp.reshape(ms, d), o_slc.reshape(ms, d), o_win.reshape(ms, d), o_lat, gm, x, mod, True, lw,
                   1, ms, ms)
        x = _ffn(x, mod, True, lw, gfinal, l == depth - 1, 1, ms, ms)
        st_s[0].append(kvc.reshape(sb, st_len, 2, g, NSA_DIM))
        st_s[1].append(kvs.reshape(sb, st_len, 2, g, NSA_DIM))
        st_s[2].append(lat.reshape(sb, st_len, LAT))
        win_all = jnp.concatenate([state_win_kv[l], kvw.reshape(sb, st_len, 2, g, NSA_DIM)], axis=1)
        st_s[3].append(win_all[:, st_len:])
    y_sample = x.reshape(sb, st_len, d)

    return (y_prompt, y_sample, jnp.stack(st_p[0]), jnp.stack(st_p[1]), jnp.stack(st_p[2]), jnp.stack(st_p[3]),
            jnp.stack(st_s[0]), jnp.stack(st_s[1]), jnp.stack(st_s[2]), jnp.stack(st_s[3]))
```

```python
import functools
import math

import numpy as np
import jax
import jax.numpy as jnp
from jax import lax
from jax.experimental import pallas as pl
from jax.experimental.pallas import tpu as pltpu

F32 = jnp.float32
BF16 = jnp.bfloat16

D_MODEL = 1024
PAGE_SIZE = 128
NSA_HEADS = 16
NSA_KV_HEADS = 2
NSA_GROUP = NSA_HEADS // NSA_KV_HEADS
NSA_DIM = D_MODEL // NSA_HEADS
CMP_BLOCK = 32
SLC_BLOCK = 64
N_SELECT = 16
WINDOW = 512
MLA_HEADS = 8
MLA_NOPE = 128
MLA_ROPE = 64
MLA_V = D_MODEL // MLA_HEADS
Q_LORA = 384
KV_LORA = 256
LAT = KV_LORA + MLA_ROPE
ROPE_THETA = 10000.0
N_BUCKETS = 32
MAX_DISTANCE = 128
NSA_SCALE = NSA_DIM ** -0.5
MLA_SCALE = (MLA_NOPE + MLA_ROPE) ** -0.5
NEG_BIG = -1e30
RMS_EPS = 1e-6
KV_ROW = 2 * NSA_KV_HEADS * NSA_DIM
GATE_LANES = 128

SEG_Q = (0, 1024)
SEG_CMP = (1024, 1280)
SEG_SLC = (1280, 1536)
SEG_WIN = (1536, 1792)
SEG_GATE = (1792, 2048)
SEG_CQ = (2048, 2432)
SEG_CKV = (2432, 2688)
SEG_KR = (2688, 2816)
SEG_GM = (2816, 4864)
D_IN_PACKED = 4864

ATT_TILE = 256
FLASH_ROWS = 128
SEL_LANES = 128
VMEM_LIMIT = 56 * 1024 * 1024
CMP_PAGES = 32
MLA_PAGES = 16


def _cparams(sem):
    return pltpu.CompilerParams(dimension_semantics=sem, vmem_limit_bytes=VMEM_LIMIT)


def _dot(a, b):
    return jnp.dot(a, b, preferred_element_type=F32)


def _dot_nt(a, b):
    return lax.dot_general(a, b, (((1,), (1,)), ((), ())), preferred_element_type=F32)


def _rms(x, g):
    return x * lax.rsqrt(jnp.mean(x * x, axis=-1, keepdims=True) + RMS_EPS) * g


def _rope_lanes(x, cos2, sin2):
    w = x.shape[-1]
    lane = lax.broadcasted_iota(jnp.int32, x.shape, 1)
    swapped = jnp.where(lane % MLA_ROPE < MLA_ROPE // 2, pltpu.roll(x, w - MLA_ROPE // 2, 1),
                        pltpu.roll(x, MLA_ROPE // 2, 1))
    return x * cos2 + swapped * sin2


def _t5_bucket_np(dist):
    max_exact = N_BUCKETS // 2
    d = np.maximum(dist, 0)
    log_ratio = np.log(np.maximum(d, 1).astype(np.float32) / max_exact) / math.log(MAX_DISTANCE / max_exact)
    large = np.minimum(max_exact + (log_ratio * (N_BUCKETS - max_exact)).astype(np.int32), N_BUCKETS - 1)
    return np.where(d < max_exact, d, large).astype(np.int32)


def _rope_tables(pos, reps):
    half = MLA_ROPE // 2
    inv = ROPE_THETA ** (-jnp.arange(half, dtype=F32) / half)
    ang = pos.astype(F32)[:, None] * inv[None, :]
    cos, sin = jnp.cos(ang), jnp.sin(ang)
    cos2 = jnp.concatenate([cos, cos], axis=-1)
    sin2 = jnp.concatenate([-sin, sin], axis=-1)
    return jnp.tile(cos2, (1, reps)), jnp.tile(sin2, (1, reps))


def _bias_kernel(table_ref, bucket_ref, mask_ref, o_ref, *, shift):
    h = pl.program_id(0)
    bucket = bucket_ref[...]
    acc = jnp.zeros(bucket.shape, F32)
    for b in range(N_BUCKETS):
        acc = jnp.where(bucket == b, table_ref[b, h], acc)
    if shift:
        acc = acc - table_ref[N_BUCKETS - 1, h]
    o_ref[0] = acc + mask_ref[...]


def _expand_bias(table, bucket, addmask, shift):
    r, c = bucket.shape
    return pl.pallas_call(
        functools.partial(_bias_kernel, shift=shift),
        grid=(NSA_HEADS,),
        in_specs=[pl.BlockSpec(memory_space=pltpu.SMEM),
                  pl.BlockSpec((r, c), lambda h: (0, 0)),
                  pl.BlockSpec((r, c), lambda h: (0, 0))],
        out_specs=pl.BlockSpec((1, r, c), lambda h: (h, 0, 0)),
        out_shape=jax.ShapeDtypeStruct((NSA_HEADS, r, c), F32),
        compiler_params=_cparams(("arbitrary",)),
        name="bias_expand",
    )(table, jnp.asarray(bucket), jnp.asarray(addmask))


def _mod_kernel(c_ref, w_ref, b_ref, o_ref):
    c = c_ref[...]
    cond = (c * jax.nn.sigmoid(c)).astype(BF16)
    o_ref[0] = _dot(cond, w_ref[0].astype(BF16)) + b_ref[0]


def _modulation(c_all, ada_w, ada_b):
    depth, d, n = ada_w.shape
    rows = c_all.shape[0]
    tn = 1536
    return pl.pallas_call(
        _mod_kernel,
        grid=(depth, n // tn),
        in_specs=[pl.BlockSpec((rows, d), lambda l, j: (0, 0)),
                  pl.BlockSpec((1, d, tn), lambda l, j: (l, 0, j)),
                  pl.BlockSpec((1, 1, tn), lambda l, j: (l, 0, j))],
        out_specs=pl.BlockSpec((1, rows, tn), lambda l, j: (l, 0, j)),
        out_shape=jax.ShapeDtypeStruct((depth, rows, n), F32),
        compiler_params=_cparams(("arbitrary", "arbitrary")),
        name="adaln_mod",
    )(c_all, ada_w, ada_b.reshape(depth, 1, n))


def _inproj_kernel(x_ref, mod_ref, gn_ref, w_ref, cos_ref, sin_ref, gq_ref, wuq_ref, wuk_ref, gkv_ref,
                   q_ref, kvc_ref, kvs_ref, kvw_ref, kst_ref, kse_ref, kso_ref, kwt_ref, kwe_ref, kwo_ref, gate_ref,
                   qmla_ref, lat_ref, latb_ref, latt_ref, gm_ref):
    x = x_ref[...]
    y = _rms(x, gn_ref[...])
    h = (y * (1.0 + mod_ref[:, 1, :]) + mod_ref[:, 0, :]).astype(BF16)

    def seg(s):
        return _dot(h, w_ref[:, s[0]:s[1]])

    q = (seg(SEG_Q) * NSA_SCALE).astype(BF16)
    for hd in range(NSA_HEADS):
        q_ref[0, hd] = q[:, hd * NSA_DIM:(hd + 1) * NSA_DIM]
    kvc_ref[...] = seg(SEG_CMP)
    half = NSA_KV_HEADS * NSA_DIM
    low = lax.broadcasted_iota(jnp.int32, (x.shape[0], half), 1) < NSA_DIM
    for s, kv_ref, kt_ref, ve_ref, vo_ref in ((SEG_SLC, kvs_ref, kst_ref, kse_ref, kso_ref),
                                              (SEG_WIN, kvw_ref, kwt_ref, kwe_ref, kwo_ref)):
        kv = seg(s)
        kv_ref[...] = kv
        kt = kv[:, 0:half].T.astype(BF16)
        vv = kv[:, half:2 * half]
        vr = pltpu.roll(vv, NSA_DIM, 1)
        for g in range(NSA_KV_HEADS):
            kt_ref[0, g] = kt[g * NSA_DIM:(g + 1) * NSA_DIM]
            ve_ref[0, g] = jnp.where(low, vv if g == 0 else vr, 1.0).astype(BF16)
            vo_ref[0, g] = jnp.where(low, 1.0, vr if g == 0 else vv).astype(BF16)
    gate_ref[...] = jax.nn.sigmoid(seg(SEG_GATE))
    gm_ref[...] = jax.nn.sigmoid(seg(SEG_GM))

    cos = cos_ref[...]
    sin = sin_ref[...]
    cqn = _rms(seg(SEG_CQ), gq_ref[...]).astype(BF16)
    qm = _dot(cqn, wuq_ref[...])
    nope_w = MLA_HEADS * MLA_NOPE
    qr = _rope_lanes(qm[:, nope_w:], cos, sin) * MLA_SCALE
    for hd in range(MLA_HEADS):
        qn = qm[:, hd * MLA_NOPE:(hd + 1) * MLA_NOPE].astype(BF16)
        qmla_ref[0, hd, :, 0:KV_LORA] = (_dot(qn, wuk_ref[hd]) * MLA_SCALE).astype(BF16)
        qmla_ref[0, hd, :, KV_LORA:LAT] = qr[:, hd * MLA_ROPE:(hd + 1) * MLA_ROPE].astype(BF16)
    ckv = _rms(seg(SEG_CKV), gkv_ref[...])
    kr = _rope_lanes(seg(SEG_KR), cos[:, 0:128], sin[:, 0:128])
    lat_ref[:, 0:KV_LORA] = ckv
    lat_ref[:, KV_LORA:LAT] = kr[:, 0:MLA_ROPE]
    latb_ref[...] = ckv.astype(BF16)
    latt_ref[0, 0:KV_LORA, :] = ckv.T.astype(BF16)
    latt_ref[0, KV_LORA:LAT, :] = kr.T[0:MLA_ROPE].astype(BF16)


def _inproj(x, mod, per_row_mod, lw, cos_t, sin_t, nb, nt, tm):
    m = nb * nt
    tpb = nt // tm
    d = D_MODEL
    if per_row_mod:
        mod_spec = pl.BlockSpec((tm, 6, d), lambda i: (i, 0, 0))
    else:
        mod_spec = pl.BlockSpec((1, 6, d), lambda i: (i // tpb, 0, 0))
    const2 = lambda i: (0, 0)
    row = lambda i: (i, 0)
    bt = lambda i: (i // tpb, 0, i % tpb, 0)
    btt = lambda i: (i // tpb, 0, 0, i % tpb)
    g = NSA_KV_HEADS
    out_shape = (
        jax.ShapeDtypeStruct((nb, NSA_HEADS, nt, NSA_DIM), BF16),
        jax.ShapeDtypeStruct((m, KV_ROW), F32),
        jax.ShapeDtypeStruct((m, KV_ROW), F32),
        jax.ShapeDtypeStruct((m, KV_ROW), F32),
        jax.ShapeDtypeStruct((nb, g, NSA_DIM, nt), BF16),
        jax.ShapeDtypeStruct((nb, g, nt, 2 * NSA_DIM), BF16),
        jax.ShapeDtypeStruct((nb, g, nt, 2 * NSA_DIM), BF16),
        jax.ShapeDtypeStruct((nb, g, NSA_DIM, nt), BF16),
        jax.ShapeDtypeStruct((nb, g, nt, 2 * NSA_DIM), BF16),
        jax.ShapeDtypeStruct((nb, g, nt, 2 * NSA_DIM), BF16),
        jax.ShapeDtypeStruct((m, g * GATE_LANES), F32),
        jax.ShapeDtypeStruct((nb, MLA_HEADS, nt, LAT), BF16),
        jax.ShapeDtypeStruct((m, LAT), F32),
        jax.ShapeDtypeStruct((m, KV_LORA), BF16),
        jax.ShapeDtypeStruct((nb, LAT, nt), BF16),
        jax.ShapeDtypeStruct((m, 2 * d), F32),
    )
    kt_spec = pl.BlockSpec((1, g, NSA_DIM, tm), btt)
    v_spec = pl.BlockSpec((1, g, tm, 2 * NSA_DIM), bt)
    out_specs = (
        pl.BlockSpec((1, NSA_HEADS, tm, NSA_DIM), bt),
        pl.BlockSpec((tm, KV_ROW), row), pl.BlockSpec((tm, KV_ROW), row), pl.BlockSpec((tm, KV_ROW), row),
        kt_spec, v_spec, v_spec, kt_spec, v_spec, v_spec,
        pl.BlockSpec((tm, g * GATE_LANES), row),
        pl.BlockSpec((1, MLA_HEADS, tm, LAT), bt),
        pl.BlockSpec((tm, LAT), row), pl.BlockSpec((tm, KV_LORA), row),
        pl.BlockSpec((1, LAT, tm), lambda i: (i // tpb, 0, i % tpb)),
        pl.BlockSpec((tm, 2 * d), row),
    )
    in_specs = [
        pl.BlockSpec((tm, d), row), mod_spec, pl.BlockSpec((1, d), const2),
        pl.BlockSpec((d, D_IN_PACKED), const2),
        pl.BlockSpec((tm, MLA_HEADS * MLA_ROPE), lambda i: (i % tpb, 0)),
        pl.BlockSpec((tm, MLA_HEADS * MLA_ROPE), lambda i: (i % tpb, 0)),
        pl.BlockSpec((1, Q_LORA), const2),
        pl.BlockSpec((Q_LORA, MLA_HEADS * (MLA_NOPE + MLA_ROPE)), const2),
        pl.BlockSpec((MLA_HEADS, MLA_NOPE, KV_LORA), lambda i: (0, 0, 0)),
        pl.BlockSpec((1, KV_LORA), const2),
    ]
    return pl.pallas_call(
        _inproj_kernel, grid=(m // tm,), in_specs=in_specs, out_specs=out_specs, out_shape=out_shape,
        compiler_params=_cparams(("arbitrary",)), name="in_proj",
    )(x, mod, lw["norm_mix_g"], lw["w_in"], cos_t, sin_t, lw["mla_q_norm_g"], lw["w_uq"], lw["w_ukT"],
      lw["mla_kv_norm_g"])


def _gelu_tanh(z):
    return 0.5 * z * (1.0 + jnp.tanh(math.sqrt(2.0 / math.pi) * (z + 0.044715 * (z * z * z))))


def _compress_t_kernel(tab_ref, *refs, pages):
    del tab_ref
    page_refs = refs[:pages]
    pet_ref, sel_ref, w1_ref, b1_ref, w2_ref, o_ref, slab_ref = refs[pages:]
    pairs = pages // 2
    rows_per_pair = 2 * (PAGE_SIZE // CMP_BLOCK)
    sel = sel_ref[...]
    pet = pet_ref[...]
    for pr in range(pairs):
        xt2 = jnp.concatenate([page_refs[2 * pr][0], page_refs[2 * pr + 1][0]], axis=1)
        slab_ref[pr] = _dot_nt(sel, (xt2 + pet).astype(BF16))
    acc = jnp.zeros((pairs * rows_per_pair, KV_ROW), F32)
    for r in range(CMP_BLOCK):
        xr = jnp.concatenate([slab_ref[pr, r * rows_per_pair:(r + 1) * rows_per_pair, :] for pr in range(pairs)],
                             axis=0)
        acc = acc + _dot(xr.astype(BF16), w1_ref[r])
    hmid = _gelu_tanh(acc + b1_ref[...])
    o_ref[0] = _dot(hmid.astype(BF16), w2_ref[...])


def _compress_t(pages3d, page_ids, lw):
    n_logical = page_ids.shape[0]
    pages = min(CMP_PAGES, n_logical)
    assert n_logical % pages == 0 and pages % 2 == 0
    steps = n_logical // pages
    per_page = PAGE_SIZE // CMP_BLOCK

    def page_map(k):
        return lambda s, tab: (tab[s * pages + k], 0, 0)

    in_specs = [pl.BlockSpec((1, KV_ROW, PAGE_SIZE), page_map(k)) for k in range(pages)]
    in_specs += [
        pl.BlockSpec((KV_ROW, 2 * PAGE_SIZE), lambda s, tab: (0, 0)),
        pl.BlockSpec((2 * PAGE_SIZE, 2 * PAGE_SIZE), lambda s, tab: (0, 0)),
        pl.BlockSpec((CMP_BLOCK, KV_ROW, KV_ROW), lambda s, tab: (0, 0, 0)),
        pl.BlockSpec((1, KV_ROW), lambda s, tab: (0, 0)),
        pl.BlockSpec((KV_ROW, KV_ROW), lambda s, tab: (0, 0)),
    ]
    r, pg, c = np.meshgrid(np.arange(CMP_BLOCK), np.arange(2), np.arange(per_page), indexing="ij")
    src = (pg * PAGE_SIZE + c * CMP_BLOCK + r).reshape(-1)
    sel = jnp.asarray((src[:, None] == np.arange(2 * PAGE_SIZE)[None, :]).astype(np.float32), dtype=BF16)
    out = pl.pallas_call(
        functools.partial(_compress_t_kernel, pages=pages),
        grid_spec=pltpu.PrefetchScalarGridSpec(
            num_scalar_prefetch=1, grid=(steps,), in_specs=in_specs,
            out_specs=pl.BlockSpec((1, per_page * pages, KV_ROW), lambda s, tab: (s, 0, 0)),
            scratch_shapes=[pltpu.VMEM((pages // 2, 2 * PAGE_SIZE, KV_ROW), F32)]),
        out_shape=jax.ShapeDtypeStruct((steps, per_page * pages, KV_ROW), F32),
        compiler_params=_cparams(("arbitrary",)), name="cmp_compress_paged",
    )(page_ids, *([pages3d] * pages), lw["cmp_pe_t"], sel, lw["cmp_w1"], lw["cmp_b1"], lw["cmp_w2"])
    return out.reshape(n_logical * per_page, KV_ROW)


def _compress_kernel(tab_ref, *refs, pages):
    del tab_ref
    page_refs = refs[:pages]
    pe_ref, w1_ref, b1_ref, w2_ref, o_ref, slabk_ref, slabv_ref = refs[pages:]
    half = KV_ROW // 2
    for k in range(pages):
        slabk_ref[k * PAGE_SIZE:(k + 1) * PAGE_SIZE, :] = page_refs[k][0, :, 0:half]
        slabv_ref[k * PAGE_SIZE:(k + 1) * PAGE_SIZE, :] = page_refs[k][0, :, half:KV_ROW]
    per_page = PAGE_SIZE // CMP_BLOCK
    acc = jnp.zeros((per_page * pages, KV_ROW), F32)
    for r in range(CMP_BLOCK):
        rows = [jnp.concatenate([slab[pl.ds(c * CMP_BLOCK + r, pages, stride=PAGE_SIZE), :]
                                 for slab in (slabk_ref, slabv_ref)], axis=1) for c in range(per_page)]
        xr = (jnp.concatenate(rows, axis=0) + pe_ref[r]).astype(BF16)
        acc = acc + _dot(xr, w1_ref[r])
    hmid = _gelu_tanh(acc + b1_ref[...])
    o_ref[0] = _dot(hmid.astype(BF16), w2_ref[...])


def _compress(rows3d, page_ids, lw):
    n_logical = page_ids.shape[0]
    pages = min(CMP_PAGES, n_logical)
    assert n_logical % pages == 0
    steps = n_logical // pages
    per_page = PAGE_SIZE // CMP_BLOCK

    def page_map(k):
        return lambda s, tab: (tab[s * pages + k], 0, 0)

    in_specs = [pl.BlockSpec((1, PAGE_SIZE, KV_ROW), page_map(k)) for k in range(pages)]
    in_specs += [
        pl.BlockSpec((CMP_BLOCK, 1, KV_ROW), lambda s, tab: (0, 0, 0)),
        pl.BlockSpec((CMP_BLOCK, KV_ROW, KV_ROW), lambda s, tab: (0, 0, 0)),
        pl.BlockSpec((1, KV_ROW), lambda s, tab: (0, 0)),
        pl.BlockSpec((KV_ROW, KV_ROW), lambda s, tab: (0, 0)),
    ]
    out = pl.pallas_call(
        functools.partial(_compress_kernel, pages=pages),
        grid_spec=pltpu.PrefetchScalarGridSpec(
            num_scalar_prefetch=1, grid=(steps,), in_specs=in_specs,
            out_specs=pl.BlockSpec((1, per_page * pages, KV_ROW), lambda s, tab: (s, 0, 0)),
            scratch_shapes=[pltpu.VMEM((pages * PAGE_SIZE, KV_ROW // 2), F32),
                            pltpu.VMEM((pages * PAGE_SIZE, KV_ROW // 2), F32)]),
        out_shape=jax.ShapeDtypeStruct((steps, per_page * pages, KV_ROW), F32),
        compiler_params=_cparams(("arbitrary",)), name="cmp_compress",
    )(page_ids, *([rows3d] * pages), lw["cmp_pe"], lw["cmp_w1"], lw["cmp_b1"], lw["cmp_w2"])
    out = out.reshape(steps, per_page, pages, KV_ROW).transpose(0, 2, 1, 3)
    return out.reshape(n_logical * per_page, KV_ROW)


def _split_compressed(tok, nb):
    n = tok.shape[0] // nb
    t = tok.reshape(nb, n // 2, 2, 2, NSA_KV_HEADS, NSA_DIM)
    t = t.transpose(3, 0, 4, 2, 1, 5).reshape(2, nb, NSA_KV_HEADS, n, NSA_DIM).astype(BF16)
    return t[0], t[1]


def _cmp_prompt_kernel(q_ref, kc_ref, vct_ref, bias_ref, gate_ref, o_ref, sel_ref, *, tq, n_cmp):
    qi = pl.program_id(2)
    kc = kc_ref[0, 0]
    vct = vct_ref[0, 0]
    maskf = (bias_ref[0] > 0.5 * NEG_BIG).astype(F32)
    gates_t = gate_ref[0].T
    imp = jnp.zeros((n_cmp, tq), F32)
    outs = []
    for hh in range(NSA_GROUP):
        z = _dot_nt(kc, q_ref[0, hh]) + bias_ref[hh]
        e = jnp.exp(z - jnp.max(z, axis=0, keepdims=True)) * maskf
        p = e / jnp.maximum(jnp.sum(e, axis=0, keepdims=True), 1e-30)
        imp = imp + p
        outs.append(_dot(vct, p.astype(BF16)) * gates_t[hh:hh + 1, :])
    o_ref[0] = jnp.concatenate(outs, axis=0).T
    n_slc = n_cmp // 2
    imp_slc = imp[0:n_slc] + imp[n_slc:n_cmp]
    t = qi * tq + lax.broadcasted_iota(jnp.int32, (n_slc, tq), 1)
    j = lax.broadcasted_iota(jnp.int32, (n_slc, tq), 0)
    cur = t // SLC_BLOCK
    forced = (j == 0) | (j == cur) | (j == cur - 1)
    score = jnp.where(forced, jnp.inf, jnp.where(j <= cur, imp_slc, -jnp.inf))
    rank = jnp.zeros((n_slc, tq), F32)
    for i in range(n_slc):
        ci = score[i:i + 1, :]
        rank = rank + jnp.where(j > i, jnp.where(ci >= score, 1.0, 0.0), jnp.where(ci > score, 1.0, 0.0))
    sel_t = jnp.where(rank < float(min(N_SELECT, n_slc)), 1.0, 0.0)
    sel_pad = jnp.concatenate([sel_t, jnp.zeros((SEL_LANES - n_slc, tq), F32)], axis=0)
    sel_ref[0, 0] = sel_pad.T


def _cmp_prompt(q, kc, vct, bias_t, gates, nb, nt):
    tq = ATT_TILE
    n_cmp = kc.shape[2]
    g = NSA_KV_HEADS
    gw = NSA_GROUP * NSA_DIM
    return pl.pallas_call(
        functools.partial(_cmp_prompt_kernel, tq=tq, n_cmp=n_cmp),
        grid=(nb, g, nt // tq),
        in_specs=[pl.BlockSpec((1, NSA_GROUP, tq, NSA_DIM), lambda b, gi, i: (b, gi, i, 0)),
                  pl.BlockSpec((1, 1, n_cmp, NSA_DIM), lambda b, gi, i: (b, gi, 0, 0)),
                  pl.BlockSpec((1, 1, NSA_DIM, n_cmp), lambda b, gi, i: (b, gi, 0, 0)),
                  pl.BlockSpec((NSA_GROUP, n_cmp, tq), lambda b, gi, i: (gi, 0, i)),
                  pl.BlockSpec((1, tq, GATE_LANES), lambda b, gi, i: (b, i, gi))],
        out_specs=(pl.BlockSpec((1, tq, gw), lambda b, gi, i: (b, i, gi)),
                   pl.BlockSpec((1, 1, tq, SEL_LANES), lambda b, gi, i: (b, gi, i, 0))),
        out_shape=(jax.ShapeDtypeStruct((nb, nt, D_MODEL), F32),
                   jax.ShapeDtypeStruct((nb, g, nt, SEL_LANES), F32)),
        compiler_params=_cparams(("arbitrary", "arbitrary", "arbitrary")), name="cmp_attn_select",
    )(q, kc, vct, bias_t, gates.reshape(nb, nt, g * GATE_LANES))


def _flash_update(s, vs, m_ref, l_ref, acc_ref):
    tk = s.shape[-1]
    m_old = m_ref[...]
    m_new = jnp.maximum(m_old, jnp.max(s, axis=-1, keepdims=True))
    alpha = jnp.exp(m_old - m_new)
    p = jnp.exp(s - jnp.concatenate([m_new] * (tk // 128), axis=1))
    psum = p[:, 0:128]
    for c in range(1, tk // 128):
        psum = psum + p[:, c * 128:(c + 1) * 128]
    l_ref[...] = alpha * l_ref[...] + psum
    dv = acc_ref.shape[-1]
    acc_ref[...] = jnp.concatenate([alpha] * (dv // 128), axis=1) * acc_ref[...] + _dot(p.astype(BF16), vs)
    m_ref[...] = m_new


def _nsa_flash_kernel(*refs, tq, slc):
    if slc:
        q_ref, k_ref, ve_ref, vo_ref, bias_ref, gate_ref, ge_ref, sel_ref, exp_ref, o_ref, m_ref, acc_ref = refs
    else:
        q_ref, k_ref, ve_ref, vo_ref, bias_ref, gate_ref, ge_ref, o_ref, m_ref, acc_ref = refs
    qi = pl.program_id(2)
    ch = FLASH_ROWS
    nsub = tq // ch
    m_ref[...] = jnp.full(m_ref.shape, NEG_BIG, F32)
    acc_ref[...] = jnp.zeros(acc_ref.shape, F32)
    if slc:
        sel = sel_ref[0, 0].astype(BF16)

    def tile(kt, btype):
        start = pl.multiple_of(kt * tq, tq)
        kt_tile = k_ref[0, 0, :, pl.ds(start, tq)]
        vs = (ve_ref[0, 0, pl.ds(start, tq), :], vo_ref[0, 0, pl.ds(start, tq), :])
        if slc:
            maskadd = (_dot(sel, exp_ref[kt]) - 1.0) * (-NEG_BIG)
        for c in range(NSA_GROUP * nsub):
            hh, qs = divmod(c, nsub)
            qrows = slice(qs * ch, (qs + 1) * ch)
            srows = slice(c * ch, (c + 1) * ch)
            s = _dot(q_ref[0, hh, qrows, :], kt_tile)
            if btype is not None:
                s = s + bias_ref[hh, btype, qrows, :]
            if slc:
                s = s + maskadd[qrows]
            m_old = m_ref[srows]
            m_new = jnp.maximum(m_old, jnp.max(s, axis=-1, keepdims=True))
            alpha = jnp.exp(m_old - m_new)
            p = jnp.exp(s - jnp.concatenate([m_new] * (tq // 128), axis=1))
            acc_ref[srows] = alpha * acc_ref[srows] + _dot(p.astype(BF16), vs[hh % 2])
            m_ref[srows] = m_new

    if slc:
        def far(kt, carry):
            tile(kt, None)
            return carry
        lax.fori_loop(0, jnp.maximum(qi - 1, 0), far, 0)
    else:
        @pl.when(qi >= 2)
        def _():
            tile(qi - 2, 2)

    @pl.when(qi >= 1)
    def _():
        tile(qi - 1, 1)

    tile(qi, 0)
    gexp = _expand_gates(gate_ref[0], ge_ref[...])
    lane = lax.broadcasted_iota(jnp.int32, (tq, 2 * NSA_DIM), 1)
    for j in range(NSA_GROUP // 2):
        a_e = acc_ref[(2 * j) * tq:(2 * j + 1) * tq]
        a_o = acc_ref[(2 * j + 1) * tq:(2 * j + 2) * tq]
        num = jnp.where(lane < NSA_DIM, a_e, a_o)
        den = pltpu.roll(jnp.where(lane < NSA_DIM, a_o, a_e), NSA_DIM, 1)
        cols = slice(j * 2 * NSA_DIM, (j + 1) * 2 * NSA_DIM)
        o_ref[0, :, cols] = num / den * gexp[:, cols]


def _expand_gates(g, expand):
    g1 = g.astype(BF16)
    r1 = g - g1.astype(F32)
    g2 = r1.astype(BF16)
    g3 = (r1 - g2.astype(F32)).astype(BF16)
    return _dot(g1, expand) + _dot(g2, expand) + _dot(g3, expand)


def _gate_expand_matrix(branch):
    k = np.arange(GATE_LANES)[:, None]
    n = np.arange(NSA_GROUP * NSA_DIM)[None, :]
    return jnp.asarray((k == branch * NSA_GROUP + n // NSA_DIM).astype(np.float32), dtype=BF16)


def _nsa_flash(q, kt, ve, vo, bias_tiles, gates, nb, nt, sel=None, expand=None):
    tq = ATT_TILE
    g = NSA_KV_HEADS
    gw = NSA_GROUP * NSA_DIM
    slc = sel is not None
    v_spec = pl.BlockSpec((1, 1, nt, 2 * NSA_DIM), lambda b, gi, i: (b, gi, 0, 0))
    in_specs = [pl.BlockSpec((1, NSA_GROUP, tq, NSA_DIM), lambda b, gi, i: (b, gi, i, 0)),
                pl.BlockSpec((1, 1, NSA_DIM, nt), lambda b, gi, i: (b, gi, 0, 0)),
                v_spec, v_spec,
                pl.BlockSpec((NSA_GROUP, 3, tq, tq), lambda b, gi, i: (gi, 0, 0, 0)),
                pl.BlockSpec((1, tq, GATE_LANES), lambda b, gi, i: (b, i, gi)),
                pl.BlockSpec((GATE_LANES, gw), lambda b, gi, i: (0, 0))]
    args = [q, kt, ve, vo, bias_tiles, gates.reshape(nb, nt, g * GATE_LANES), _gate_expand_matrix(1 if slc else 2)]
    if slc:
        in_specs += [pl.BlockSpec((1, 1, tq, SEL_LANES), lambda b, gi, i: (b, gi, i, 0)),
                     pl.BlockSpec((nt // tq, SEL_LANES, tq), lambda b, gi, i: (0, 0, 0))]
        args += [sel, expand]
    return pl.pallas_call(
        functools.partial(_nsa_flash_kernel, tq=tq, slc=slc),
        grid=(nb, g, nt // tq),
        in_specs=in_specs,
        out_specs=pl.BlockSpec((1, tq, gw), lambda b, gi, i: (b, i, gi)),
        out_shape=jax.ShapeDtypeStruct((nb, nt, D_MODEL), F32),
        scratch_shapes=[pltpu.VMEM((NSA_GROUP * tq, 128), F32), pltpu.VMEM((NSA_GROUP * tq, 2 * NSA_DIM), F32)],
        compiler_params=_cparams(("arbitrary", "arbitrary", "arbitrary")),
        name="slc_attn" if slc else "win_attn",
    )(*args)


def _mla_prompt_kernel(q_ref, latt_ref, latv_ref, o_ref, m_ref, l_ref, acc_ref, *, tq):
    qi = pl.program_id(1)
    rows = MLA_HEADS * tq
    m_ref[...] = jnp.full(m_ref.shape, NEG_BIG, F32)
    l_ref[...] = jnp.zeros(l_ref.shape, F32)
    acc_ref[...] = jnp.zeros(acc_ref.shape, F32)

    def tile(kt, diag):
        start = pl.multiple_of(kt * tq, tq)
        s = _dot(q_ref[0].reshape(rows, LAT), latt_ref[0, :, pl.ds(start, tq)])
        if diag:
            row = lax.broadcasted_iota(jnp.int32, (tq, tq), 0)
            col = lax.broadcasted_iota(jnp.int32, (tq, tq), 1)
            s = jnp.where((col <= row)[None], s.reshape(MLA_HEADS, tq, tq), NEG_BIG).reshape(rows, tq)
        _flash_update(s, latv_ref[0, pl.ds(start, tq), :], m_ref, l_ref, acc_ref)

    def body(kt, carry):
        tile(kt, False)
        return carry
    lax.fori_loop(0, qi, body, 0)
    tile(qi, True)
    o = acc_ref[...] / jnp.sum(l_ref[...], axis=-1, keepdims=True)
    o_ref[0] = o.reshape(MLA_HEADS, tq, KV_LORA).astype(BF16)


def _mla_prompt(qmla, latt, latv, nb, nt):
    tq = ATT_TILE
    return pl.pallas_call(
        functools.partial(_mla_prompt_kernel, tq=tq),
        grid=(nb, nt // tq),
        in_specs=[pl.BlockSpec((1, MLA_HEADS, tq, LAT), lambda b, i: (b, 0, i, 0)),
                  pl.BlockSpec((1, LAT, nt), lambda b, i: (b, 0, 0)),
                  pl.BlockSpec((1, nt, KV_LORA), lambda b, i: (b, 0, 0))],
        out_specs=pl.BlockSpec((1, MLA_HEADS, tq, KV_LORA), lambda b, i: (b, 0, i, 0)),
        out_shape=jax.ShapeDtypeStruct((nb, MLA_HEADS, nt, KV_LORA), BF16),
        scratch_shapes=[pltpu.VMEM((MLA_HEADS * tq, 128), F32), pltpu.VMEM((MLA_HEADS * tq, 128), F32),
                        pltpu.VMEM((MLA_HEADS * tq, KV_LORA), F32)],
        compiler_params=_cparams(("arbitrary", "arbitrary")), name="mla_attn",
    )(qmla, latt, latv.reshape(nb, nt, KV_LORA))


def _merge_kernel(oc_ref, os_ref, ow_ref, ol_ref, gm_ref, x_ref, mod_ref, wuv_ref, wo_ref, o_ref):
    o_nsa = oc_ref[...] + os_ref[...] + ow_ref[...]
    o_mla = jnp.concatenate([_dot(ol_ref[0, hd], wuv_ref[hd]) for hd in range(MLA_HEADS)], axis=-1)
    gm = gm_ref[...]
    merged = (gm[:, 0:D_MODEL] * o_nsa + gm[:, D_MODEL:] * o_mla).astype(BF16)
    o_ref[...] = x_ref[...] + mod_ref[:, 2, :] * _dot(merged, wo_ref[...])


def _merge(o_cmp, o_slc, o_win, o_lat, gm, x, mod, per_row_mod, lw, nb, nt, tm):
    m = nb * nt
    tpb = nt // tm
    d = D_MODEL
    row = lambda i: (i, 0)
    if per_row_mod:
        mod_spec = pl.BlockSpec((tm, 6, d), lambda i: (i, 0, 0))
    else:
        mod_spec = pl.BlockSpec((1, 6, d), lambda i: (i // tpb, 0, 0))
    return pl.pallas_call(
        _merge_kernel, grid=(m // tm,),
        in_specs=[pl.BlockSpec((tm, d), row), pl.BlockSpec((tm, d), row), pl.BlockSpec((tm, d), row),
                  pl.BlockSpec((1, MLA_HEADS, tm, KV_LORA), lambda i: (i // tpb, 0, i % tpb, 0)),
                  pl.BlockSpec((tm, 2 * d), row), pl.BlockSpec((tm, d), row), mod_spec,
                  pl.BlockSpec((MLA_HEADS, KV_LORA, MLA_V), lambda i: (0, 0, 0)),
                  pl.BlockSpec((d, d), lambda i: (0, 0))],
        out_specs=pl.BlockSpec((tm, d), row),
        out_shape=jax.ShapeDtypeStruct((m, d), F32),
        compiler_params=_cparams(("arbitrary",)), name="merge_out_proj",
    )(o_cmp, o_slc, o_win, o_lat, gm, x, mod, lw["w_uv"], lw["w_out"])


def _ffn_kernel(x_ref, mod_ref, gn_ref, w1_ref, w3_ref, w2_ref, gf_ref, o_ref, h_ref, acc_ref, *, final):
    f = pl.program_id(1)

    @pl.when(f == 0)
    def _():
        y = _rms(x_ref[...], gn_ref[...])
        h_ref[...] = (y * (1.0 + mod_ref[:, 4, :]) + mod_ref[:, 3, :]).astype(BF16)
        acc_ref[...] = jnp.zeros(acc_ref.shape, F32)

    h = h_ref[...]
    a = _dot(h, w1_ref[...])
    b = _dot(h, w3_ref[...])
    act = (a * jax.nn.sigmoid(a) * b).astype(BF16)
    acc_ref[...] += _dot(act, w2_ref[...])

    @pl.when(f == pl.num_programs(1) - 1)
    def _():
        y = x_ref[...] + mod_ref[:, 5, :] * acc_ref[...]
        if final:
            y = _rms(y, gf_ref[...])
        o_ref[...] = y


def _ffn(x, mod, per_row_mod, lw, gfinal, final, nb, nt, tm):
    m = nb * nt
    tpb = nt // tm
    d = D_MODEL
    dff = lw["ffn_w2"].shape[0]
    tf = dff // 2
    nf = dff // tf
    row = lambda i, f: (i, 0)
    if per_row_mod:
        mod_spec = pl.BlockSpec((tm, 6, d), lambda i, f: (i, 0, 0))
    else:
        mod_spec = pl.BlockSpec((1, 6, d), lambda i, f: (i // tpb, 0, 0))
    return pl.pallas_call(
        functools.partial(_ffn_kernel, final=final), grid=(m // tm, nf),
        in_specs=[pl.BlockSpec((tm, d), row), mod_spec, pl.BlockSpec((1, d), lambda i, f: (0, 0)),
                  pl.BlockSpec((d, tf), lambda i, f: (0, f)),
                  pl.BlockSpec((d, tf), lambda i, f: (0, nf + f)),
                  pl.BlockSpec((tf, d), lambda i, f: (f, 0)),
                  pl.BlockSpec((1, d), lambda i, f: (0, 0))],
        out_specs=pl.BlockSpec((tm, d), row),
        out_shape=jax.ShapeDtypeStruct((m, d), F32),
        scratch_shapes=[pltpu.VMEM((tm, d), BF16), pltpu.VMEM((tm, d), F32)],
        compiler_params=_cparams(("arbitrary", "arbitrary")), name="ffn",
    )(x, mod, lw["norm_ffn_g"], lw["ffn_w13"], lw["ffn_w13"], lw["ffn_w2"], gfinal)


def _cmp_sample_kernel(q_ref, kc_ref, vc_ref, bias_ref, gate_ref, o_ref, imp_ref, *, nt, n_cmp):
    rows = nt * NSA_GROUP
    n_past = n_cmp // 2
    for g in range(NSA_KV_HEADS):
        qg = q_ref[0, :, g].reshape(rows, NSA_DIM)
        z = _dot_nt(qg, kc_ref[0, g]) + bias_ref[g]
        maskf = (bias_ref[g] > 0.5 * NEG_BIG).astype(F32)
        e = jnp.exp(z - jnp.max(z, axis=-1, keepdims=True)) * maskf
        p = e / jnp.maximum(jnp.sum(e, axis=-1, keepdims=True), 1e-30)
        o = _dot(p.astype(BF16), vc_ref[0, g]) * gate_ref[0, g, 0]
        o_ref[0, :, g] = o.reshape(nt, NSA_GROUP, NSA_DIM)
        imp = jnp.sum(p.reshape(nt, NSA_GROUP, n_cmp), axis=1)
        imp_ref[0, g] = imp[:, 0:n_past] + imp[:, n_past:n_cmp]


def _topk_sample_kernel(imp_ref, idx_ref, *, n_pick):
    imp = imp_ref[...]
    rows, n_past = imp.shape
    lane = lax.broadcasted_iota(jnp.int32, (rows, n_past), 1)
    score = jnp.where((lane == 0) | (lane == n_past - 1), jnp.inf, imp)

    def body(i, rank):
        ci = jnp.sum(jnp.where(lane == i, score, 0.0), axis=-1, keepdims=True)
        beats = (ci > score) | ((ci == score) & (i < lane))
        return rank + jnp.where(beats, 1, 0)
    rank = lax.fori_loop(0, n_past, body, jnp.zeros((rows, n_past), jnp.int32))
    out_lane = lax.broadcasted_iota(jnp.int32, (rows, 128), 1)
    picked = jnp.zeros((rows, 128), jnp.int32)
    for r in range(n_pick):
        ir = jnp.sum(jnp.where(rank == r, lane, 0), axis=-1, keepdims=True)
        picked = jnp.where(out_lane == r, ir, picked)
    idx_ref[...] = picked


def _topk_sample(imp2d, n_pick):
    rows, n_past = imp2d.shape
    return pl.pallas_call(
        functools.partial(_topk_sample_kernel, n_pick=n_pick),
        grid=(1,),
        in_specs=[pl.BlockSpec((rows, n_past), lambda i: (0, 0))],
        out_specs=pl.BlockSpec((rows, 128), lambda i: (0, 0)),
        out_shape=jax.ShapeDtypeStruct((rows, 128), jnp.int32),
        compiler_params=_cparams(("arbitrary",)), name="topk_blocks_sample",
    )(imp2d)


def _cmp_sample(q5, kc, vc, bias, gate_cols, nb, nt):
    n_cmp = kc.shape[2]
    g = NSA_KV_HEADS
    rows = nt * NSA_GROUP
    return pl.pallas_call(
        functools.partial(_cmp_sample_kernel, nt=nt, n_cmp=n_cmp),
        grid=(nb,),
        in_specs=[pl.BlockSpec((1, nt, g, NSA_GROUP, NSA_DIM), lambda b: (b, 0, 0, 0, 0)),
                  pl.BlockSpec((1, g, n_cmp, NSA_DIM), lambda b: (b, 0, 0, 0)),
                  pl.BlockSpec((1, g, n_cmp, NSA_DIM), lambda b: (b, 0, 0, 0)),
                  pl.BlockSpec((g, rows, n_cmp), lambda b: (0, 0, 0)),
                  pl.BlockSpec((1, g, 1, rows, 1), lambda b: (b, 0, 0, 0, 0))],
        out_specs=(pl.BlockSpec((1, nt, g, NSA_GROUP, NSA_DIM), lambda b: (b, 0, 0, 0, 0)),
                   pl.BlockSpec((1, g, nt, n_cmp // 2), lambda b: (b, 0, 0, 0))),
        out_shape=(jax.ShapeDtypeStruct((nb, nt, g, NSA_GROUP, NSA_DIM), F32),
                   jax.ShapeDtypeStruct((nb, g, nt, n_cmp // 2), F32)),
        compiler_params=_cparams(("arbitrary",)), name="cmp_attn_sample",
    )(q5, kc, vc, bias, gate_cols)


def _slc_sample_kernel(idx_ref, pt_ref, *refs, nt, n_pick, n_past, n_pages):
    del pt_ref, n_pages
    nblk = NSA_KV_HEADS * n_pick
    blk_refs = refs[:nblk]
    q_ref, new_ref, bias_ref, biasc_ref, gate_ref, o_ref = refs[nblk:]
    b = pl.program_id(0)
    t = pl.program_id(1)
    half = NSA_KV_HEADS * NSA_DIM
    lane_half = lax.broadcasted_iota(jnp.int32, (NSA_GROUP, PAGE_SIZE), 1) // SLC_BLOCK
    for g in range(NSA_KV_HEADS):
        qg = q_ref[0, 0, g]
        s_list, vt_list = [], []
        for n in range(n_pick):
            page = blk_refs[g * n_pick + n]
            j = idx_ref[((b * nt + t) * NSA_KV_HEADS + g) * n_pick + n]
            near = jnp.clip(j - (n_past - 3), 0, 2)
            kt = page[0, g * NSA_DIM:(g + 1) * NSA_DIM, :].astype(BF16)
            s_list.append(_dot(qg, kt) + jnp.where(lane_half == (j & 1), bias_ref[0, g, near], NEG_BIG))
            vt_list.append(page[0, half + g * NSA_DIM:half + (g + 1) * NSA_DIM, :].astype(BF16))
        new = new_ref[0]
        s_new = _dot_nt(qg, new[:, g * NSA_DIM:(g + 1) * NSA_DIM].astype(BF16)) + biasc_ref[0, g]
        m = s_new.max(axis=-1, keepdims=True)
        for s in s_list:
            m = jnp.maximum(m, s.max(axis=-1, keepdims=True))
        p = jnp.exp(s_new - m)
        l = jnp.sum(p, axis=-1, keepdims=True)
        acc = _dot(p.astype(BF16), new[:, half + g * NSA_DIM:half + (g + 1) * NSA_DIM].astype(BF16))
        for s, vt in zip(s_list, vt_list):
            p = jnp.exp(s - m)
            l = l + jnp.sum(p, axis=-1, keepdims=True)
            acc = acc + _dot_nt(p.astype(BF16), vt)
        o_ref[0, 0, g] = acc / l * gate_ref[0, g, 0, 0]


def _slc_sample(idx_flat, pt_flat, cache_blocks, q5, new_rows, bias_near, bias_cur, gate_cols, nb, nt, n_pick,
                n_past, n_pages):
    g = NSA_KV_HEADS

    def blk_map(gi, n):
        def f(b, t, idx, pt):
            j = idx[((b * nt + t) * g + gi) * n_pick + n]
            return (pt[b * n_pages + (j >> 1)], 0, 0)
        return f

    in_specs = [pl.BlockSpec((1, KV_ROW, PAGE_SIZE), blk_map(gi, n)) for gi in range(g) for n in range(n_pick)]
    in_specs += [
        pl.BlockSpec((1, 1, g, NSA_GROUP, NSA_DIM), lambda b, t, idx, pt: (b, t, 0, 0, 0)),
        pl.BlockSpec((1, 8, KV_ROW), lambda b, t, idx, pt: (b, 0, 0)),
        pl.BlockSpec((1, g, 3, NSA_GROUP, PAGE_SIZE), lambda b, t, idx, pt: (t, 0, 0, 0, 0)),
        pl.BlockSpec((1, g, NSA_GROUP, 8), lambda b, t, idx, pt: (t, 0, 0, 0)),
        pl.BlockSpec((1, g, 1, 1, NSA_GROUP, 1), lambda b, t, idx, pt: (b, 0, 0, t, 0, 0)),
    ]
    return pl.pallas_call(
        functools.partial(_slc_sample_kernel, nt=nt, n_pick=n_pick, n_past=n_past, n_pages=n_pages),
        grid_spec=pltpu.PrefetchScalarGridSpec(
            num_scalar_prefetch=2, grid=(nb, nt), in_specs=in_specs,
            out_specs=pl.BlockSpec((1, 1, g, NSA_GROUP, NSA_DIM), lambda b, t, idx, pt: (b, t, 0, 0, 0))),
        out_shape=jax.ShapeDtypeStruct((nb, nt, g, NSA_GROUP, NSA_DIM), F32),
        compiler_params=_cparams(("arbitrary", "arbitrary")), name="slc_attn_sample",
    )(idx_flat, pt_flat, *([cache_blocks] * (g * n_pick)), q5, new_rows, bias_near, bias_cur, gate_cols)


def _win_sample_kernel(q_ref, buf_ref, new_ref, bias_ref, biasn_ref, gate_ref, o_ref, *, nt):
    rows = nt * NSA_GROUP
    half = NSA_KV_HEADS * NSA_DIM
    buf = buf_ref[0, 0]
    new = new_ref[0]
    for g in range(NSA_KV_HEADS):
        qg = q_ref[0, :, g].reshape(rows, NSA_DIM)
        s1 = _dot(qg, buf[g * NSA_DIM:(g + 1) * NSA_DIM, :].astype(BF16)) + bias_ref[g]
        s2 = _dot_nt(qg, new[:, g * NSA_DIM:(g + 1) * NSA_DIM].astype(BF16)) + biasn_ref[g]
        m = jnp.maximum(s1.max(axis=-1, keepdims=True), s2.max(axis=-1, keepdims=True))
        p1 = jnp.exp(s1 - m)
        p2 = jnp.exp(s2 - m)
        l = jnp.sum(p1, axis=-1, keepdims=True) + jnp.sum(p2, axis=-1, keepdims=True)
        acc = _dot_nt(p1.astype(BF16), buf[half + g * NSA_DIM:half + (g + 1) * NSA_DIM, :].astype(BF16))
        acc = acc + _dot(p2.astype(BF16), new[:, half + g * NSA_DIM:half + (g + 1) * NSA_DIM].astype(BF16))
        o_ref[0, :, g] = (acc / l * gate_ref[0, g, 0]).reshape(nt, NSA_GROUP, NSA_DIM)


def _win_sample(q5, win_state, layer, new_rows, bias_buf, bias_new, gate_cols, nb, nt):
    g = NSA_KV_HEADS
    rows = nt * NSA_GROUP
    wlen = win_state.shape[3]
    return pl.pallas_call(
        functools.partial(_win_sample_kernel, nt=nt),
        grid=(nb,),
        in_specs=[pl.BlockSpec((1, nt, g, NSA_GROUP, NSA_DIM), lambda b: (b, 0, 0, 0, 0)),
                  pl.BlockSpec((1, 1, KV_ROW, wlen), lambda b: (layer, b, 0, 0)),
                  pl.BlockSpec((1, 8, KV_ROW), lambda b: (b, 0, 0)),
                  pl.BlockSpec((g, rows, wlen), lambda b: (0, 0, 0)),
                  pl.BlockSpec((g, rows, 8), lambda b: (0, 0, 0)),
                  pl.BlockSpec((1, g, 1, rows, 1), lambda b: (b, 0, 2, 0, 0))],
        out_specs=pl.BlockSpec((1, nt, g, NSA_GROUP, NSA_DIM), lambda b: (b, 0, 0, 0, 0)),
        out_shape=jax.ShapeDtypeStruct((nb, nt, g, NSA_GROUP, NSA_DIM), F32),
        compiler_params=_cparams(("arbitrary",)), name="win_attn_sample",
    )(q5, win_state, new_rows, bias_buf, bias_new, gate_cols)


def _mla_sample_kernel(pt_ref, *refs, pages, nt):
    del pt_ref
    page_refs = refs[:pages]
    q_ref, new_ref, o_ref, m_ref, l_ref, acc_ref = refs[pages:]
    step = pl.program_id(1)
    rows = MLA_HEADS * nt

    @pl.when(step == 0)
    def _():
        m_ref[...] = jnp.full(m_ref.shape, NEG_BIG, F32)
        l_ref[...] = jnp.zeros(l_ref.shape, F32)
        acc_ref[...] = jnp.zeros(acc_ref.shape, F32)

    q = q_ref[0]
    kt = jnp.concatenate([page_refs[k][0].astype(BF16) for k in range(pages)], axis=1)
    s = _dot(q, kt)
    m_old = m_ref[...]
    m_new = jnp.maximum(m_old, s.max(axis=-1, keepdims=True))
    alpha = jnp.exp(m_old - m_new)
    p = jnp.exp(s - m_new)
    m_ref[...] = m_new
    l_ref[...] = alpha * l_ref[...] + jnp.sum(p, axis=-1, keepdims=True)
    acc_ref[...] = alpha * acc_ref[...] + _dot_nt(p.astype(BF16), kt[0:KV_LORA, :])

    @pl.when(step == pl.num_programs(1) - 1)
    def _():
        new = new_ref[0].astype(BF16)
        tq = lax.broadcasted_iota(jnp.int32, (rows, 8), 0) % nt
        tk = lax.broadcasted_iota(jnp.int32, (rows, 8), 1)
        s = jnp.where(tk <= tq, _dot_nt(q, new), NEG_BIG)
        m_o = m_ref[...]
        m_n = jnp.maximum(m_o, s.max(axis=-1, keepdims=True))
        a = jnp.exp(m_o - m_n)
        p = jnp.exp(s - m_n)
        lf = a * l_ref[...] + jnp.sum(p, axis=-1, keepdims=True)
        accf = a * acc_ref[...] + _dot(p.astype(BF16), new[:, 0:KV_LORA])
        o_ref[0] = (accf / lf).astype(BF16)


def _mla_sample(pt_flat, cache_pages, q_rows, new_rows, nb, nt, n_pages):
    pages = MLA_PAGES
    steps = n_pages // pages
    rows = MLA_HEADS * nt

    def page_map(k):
        return lambda b, s, pt: (pt[b * n_pages + s * pages + k], 0, 0)

    in_specs = [pl.BlockSpec((1, LAT, PAGE_SIZE), page_map(k)) for k in range(pages)]
    in_specs += [pl.BlockSpec((1, rows, LAT), lambda b, s, pt: (b, 0, 0)),
                 pl.BlockSpec((1, 8, LAT), lambda b, s, pt: (b, 0, 0))]
    return pl.pallas_call(
        functools.partial(_mla_sample_kernel, pages=pages, nt=nt),
        grid_spec=pltpu.PrefetchScalarGridSpec(
            num_scalar_prefetch=1, grid=(nb, steps), in_specs=in_specs,
            out_specs=pl.BlockSpec((1, rows, KV_LORA), lambda b, s, pt: (b, 0, 0)),
            scratch_shapes=[pltpu.VMEM((rows, 1), F32), pltpu.VMEM((rows, 1), F32),
                            pltpu.VMEM((rows, KV_LORA), F32)]),
        out_shape=jax.ShapeDtypeStruct((nb, rows, KV_LORA), BF16),
        compiler_params=_cparams(("arbitrary", "arbitrary")), name="mla_attn_sample",
    )(pt_flat, *([cache_pages] * pages), q_rows, new_rows)


def _pack_layer(l, w):
    d = D_MODEL
    w_in = w["w_in"][l]
    sizes = (NSA_HEADS * NSA_DIM, KV_ROW, KV_ROW, KV_ROW, 3 * NSA_HEADS, Q_LORA, KV_LORA, MLA_ROPE, 2 * d)
    offs = np.concatenate([[0], np.cumsum(sizes)])
    seg = [w_in[:, offs[i]:offs[i + 1]] for i in range(len(sizes))]
    gsrc = seg[4].reshape(d, NSA_KV_HEADS, NSA_GROUP, 3).transpose(0, 1, 3, 2).reshape(d, NSA_KV_HEADS, 3 * NSA_GROUP)
    gates = jnp.pad(gsrc, ((0, 0), (0, 0), (0, GATE_LANES - 3 * NSA_GROUP))).reshape(d, NSA_KV_HEADS * GATE_LANES)
    kr = jnp.pad(seg[7], ((0, 0), (0, 128 - MLA_ROPE)))
    w_packed = jnp.concatenate([seg[0], seg[1], seg[2], seg[3], gates, seg[5], seg[6], kr, seg[8]], axis=1)
    wuq = w["mla_w_uq"][l].reshape(Q_LORA, MLA_HEADS, MLA_NOPE + MLA_ROPE)
    wuq = jnp.concatenate([wuq[:, :, :MLA_NOPE].reshape(Q_LORA, -1), wuq[:, :, MLA_NOPE:].reshape(Q_LORA, -1)], axis=1)
    w1 = w["nsa_cmp_w1"][l].reshape(2, CMP_BLOCK, NSA_DIM, NSA_DIM)
    eye_g = jnp.eye(NSA_KV_HEADS, dtype=F32)
    eye_i = jnp.eye(2, dtype=F32)
    w1big = jnp.einsum("irdo,ij,gh->rigdjho", w1, eye_i, eye_g).reshape(CMP_BLOCK, KV_ROW, KV_ROW)
    w2big = jnp.einsum("ido,ij,gh->igdjho", w["nsa_cmp_w2"][l], eye_i, eye_g).reshape(KV_ROW, KV_ROW)
    pe = w["nsa_cmp_pe"][l]
    pe_big = jnp.broadcast_to(pe.transpose(1, 0, 2)[:, :, None, :], (CMP_BLOCK, 2, NSA_KV_HEADS, NSA_DIM))
    b1big = jnp.broadcast_to(w["nsa_cmp_b1"][l][:, None, :], (2, NSA_KV_HEADS, NSA_DIM))
    return dict(
        norm_mix_g=w["norm_mix_g"][l].reshape(1, d),
        w_in=w_packed.astype(BF16),
        mla_q_norm_g=w["mla_q_norm_g"][l].reshape(1, Q_LORA),
        w_uq=wuq.astype(BF16),
        w_ukT=w["mla_w_uk"][l].transpose(1, 2, 0).astype(BF16),
        mla_kv_norm_g=w["mla_kv_norm_g"][l].reshape(1, KV_LORA),
        w_uv=w["mla_w_uv"][l].transpose(1, 0, 2).astype(BF16),
        w_out=w["w_out"][l].astype(BF16),
        norm_ffn_g=w["norm_ffn_g"][l].reshape(1, d),
        ffn_w13=w["ffn_w13"][l].astype(BF16),
        ffn_w2=w["ffn_w2"][l].astype(BF16),
        cmp_pe=pe_big.reshape(CMP_BLOCK, 1, KV_ROW),
        cmp_pe_t=jnp.tile(pe_big.reshape(CMP_BLOCK, KV_ROW).T, (1, 2 * PAGE_SIZE // CMP_BLOCK)),
        cmp_w1=w1big.astype(BF16),
        cmp_b1=b1big.reshape(1, KV_ROW),
        cmp_w2=w2big.astype(BF16),
    )


def _cmp_block_ends(n_cmp):
    order = np.concatenate([np.arange(0, n_cmp, 2), np.arange(1, n_cmp, 2)])
    return (order + 1) * CMP_BLOCK - 1


def _prompt_bias_tables(table, nt):
    tq = ATT_TILE
    i = np.arange(tq)[:, None]
    j = np.arange(tq)[None, :]
    d0, d1, d2 = i - j, tq + i - j, 2 * tq + i - j
    bucket = np.concatenate([_t5_bucket_np(d0), _t5_bucket_np(d1), _t5_bucket_np(d2)], axis=0)
    mask = np.concatenate([np.where(d0 >= 0, 0.0, NEG_BIG), np.zeros((tq, tq)),
                           np.where(d2 < WINDOW, 0.0, NEG_BIG)], axis=0).astype(np.float32)
    tiles = _expand_bias(table, bucket, mask, True).reshape(NSA_HEADS, 3, tq, tq)
    n_cmp = nt // CMP_BLOCK
    dist = np.arange(nt)[None, :] - _cmp_block_ends(n_cmp)[:, None]
    cmp_bias = _expand_bias(table, _t5_bucket_np(dist), np.where(dist >= 0, 0.0, NEG_BIG).astype(np.float32), False)
    expand = (np.arange(SEL_LANES)[None, :, None] ==
              (np.arange(nt // tq)[:, None, None] * tq + np.arange(tq)[None, None, :]) // SLC_BLOCK)
    return tiles, cmp_bias, jnp.asarray(expand.astype(np.float32), dtype=BF16)


def _sample_bias_tables(table, nt, past, wlen):
    g, hg = NSA_KV_HEADS, NSA_GROUP
    rows = nt * hg
    q_pos = past + np.arange(nt)

    def per_group(b, width):
        return b.reshape(g, hg, nt, width).transpose(0, 2, 1, 3).reshape(g, rows, width)

    n_cmp = past // CMP_BLOCK
    dist = q_pos[:, None] - _cmp_block_ends(n_cmp)[None, :]
    cmp_bias = per_group(_expand_bias(table, _t5_bucket_np(dist),
                                      np.where(dist >= 0, 0.0, NEG_BIG).astype(np.float32), False), n_cmp)
    dist = q_pos[:, None] - (past - wlen + np.arange(wlen))[None, :]
    ok = (dist >= 0) & (dist < WINDOW)
    win_bias = per_group(_expand_bias(table, _t5_bucket_np(dist), np.where(ok, 0.0, NEG_BIG).astype(np.float32),
                                      False), wlen)
    dist = q_pos[:, None] - (past + np.arange(8))[None, :]
    ok = (dist >= 0) & (dist < WINDOW) & (np.arange(8)[None, :] < nt)
    new_mask = np.where(ok, 0.0, NEG_BIG).astype(np.float32)
    win_new = per_group(_expand_bias(table, _t5_bucket_np(dist), new_mask, False), 8)
    n_past = past // SLC_BLOCK
    near = np.stack([np.full((nt, SLC_BLOCK), 10 * MAX_DISTANCE),
                     q_pos[:, None] - ((n_past - 2) * SLC_BLOCK + np.arange(SLC_BLOCK))[None, :],
                     q_pos[:, None] - ((n_past - 1) * SLC_BLOCK + np.arange(SLC_BLOCK))[None, :]], axis=1)
    near = near.reshape(nt, 3 * SLC_BLOCK)
    slc_near = _expand_bias(table, _t5_bucket_np(near), np.zeros(near.shape, np.float32), True)
    slc_near = slc_near.reshape(g, hg, nt, 3, SLC_BLOCK).transpose(2, 0, 3, 1, 4)
    slc_near = jnp.tile(slc_near, (1, 1, 1, 1, PAGE_SIZE // SLC_BLOCK))
    slc_cur = _expand_bias(table, _t5_bucket_np(dist), np.where((dist >= 0) & (np.arange(8)[None, :] < nt), 0.0,
                                                                 NEG_BIG).astype(np.float32), True)
    slc_cur = slc_cur.reshape(g, hg, nt, 8).transpose(2, 0, 1, 3)
    return cmp_bias, win_bias, win_new, slc_near, slc_cur


def kernel(x_prompt, x_sample, cache_cmp_kv, cache_slc_kv, cache_mla, state_win_kv, page_table, c_prompt, c_sample,
           rel_bias_table, ada_w, ada_b, norm_mix_g, w_in, nsa_cmp_pe, nsa_cmp_w1, nsa_cmp_b1, nsa_cmp_w2,
           mla_q_norm_g, mla_w_uq, mla_kv_norm_g, mla_w_uk, mla_w_uv, w_out, norm_ffn_g, ffn_w13, ffn_w2,
           final_norm_g):
    weights = dict(norm_mix_g=norm_mix_g, w_in=w_in, nsa_cmp_pe=nsa_cmp_pe, nsa_cmp_w1=nsa_cmp_w1,
                   nsa_cmp_b1=nsa_cmp_b1, nsa_cmp_w2=nsa_cmp_w2, mla_q_norm_g=mla_q_norm_g, mla_w_uq=mla_w_uq,
                   mla_kv_norm_g=mla_kv_norm_g, mla_w_uk=mla_w_uk, mla_w_uv=mla_w_uv, w_out=w_out,
                   norm_ffn_g=norm_ffn_g, ffn_w13=ffn_w13, ffn_w2=ffn_w2)
    depth = ada_w.shape[0]
    d = D_MODEL
    g = NSA_KV_HEADS
    pb, pt_len, _ = x_prompt.shape
    sb, st_len, _ = x_sample.shape
    n_pool = cache_cmp_kv.shape[1]
    n_pages = page_table.shape[1]
    past = n_pages * PAGE_SIZE
    wlen = state_win_kv.shape[2]
    n_past_blocks = past // SLC_BLOCK
    n_pick = N_SELECT - 1
    assert pt_len % ATT_TILE == 0 and pt_len >= WINDOW and st_len <= 8 and past >= wlen
    assert n_past_blocks > n_pick and n_pages % MLA_PAGES == 0

    layers = [_pack_layer(l, weights) for l in range(depth)]
    gfinal = final_norm_g.reshape(1, d)
    mod_all = _modulation(jnp.concatenate([c_prompt, c_sample], axis=0), ada_w, ada_b)

    tiles, cmp_bias_p, expand = _prompt_bias_tables(rel_bias_table, pt_len)
    cmp_bias_s, win_bias_s, win_new_s, slc_near_s, slc_cur_s = _sample_bias_tables(rel_bias_table, st_len, past, wlen)
    cos_p, sin_p = _rope_tables(jnp.arange(pt_len), MLA_HEADS)
    cos_s, sin_s = _rope_tables(jnp.tile(past + jnp.arange(st_len), sb), MLA_HEADS)
    prompt_pages = jnp.arange(pb * pt_len // PAGE_SIZE, dtype=jnp.int32)
    pt_flat = page_table.reshape(-1).astype(jnp.int32)

    x = x_prompt.reshape(pb * pt_len, d)
    st_p = [[], [], [], []]
    tm = 256
    for l in range(depth):
        lw = layers[l]
        mod = mod_all[l, :pb].reshape(pb, 6, d)
        (q, kvc, kvs, kvw, kst, kse, kso, kwt, kwe, kwo, gates, qmla, lat, latv, latt, gm) = _inproj(
            x, mod, False, lw, cos_p, sin_p, pb, pt_len, tm)
        tok = _compress(kvc.reshape(-1, PAGE_SIZE, KV_ROW), prompt_pages, lw)
        kc, vc = _split_compressed(tok, pb)
        o_cmp, sel = _cmp_prompt(q, kc, vc.transpose(0, 1, 3, 2), cmp_bias_p, gates, pb, pt_len)
        o_slc = _nsa_flash(q, kst, kse, kso, tiles, gates, pb, pt_len, sel=sel, expand=expand)
        o_win = _nsa_flash(q, kwt, kwe, kwo, tiles, gates, pb, pt_len)
        o_lat = _mla_prompt(qmla, latt, latv, pb, pt_len)
        x = _merge(o_cmp.reshape(-1, d), o_slc.reshape(-1, d), o_win.reshape(-1, d), o_lat, gm, x, mod, False, lw,
                   pb, pt_len, tm)
        x = _ffn(x, mod, False, lw, gfinal, l == depth - 1, pb, pt_len, 512)
        st_p[0].append(kvc.reshape(pb, pt_len, 2, g, NSA_DIM))
        st_p[1].append(kvs.reshape(pb, pt_len, 2, g, NSA_DIM))
        st_p[2].append(lat.reshape(pb, pt_len, LAT))
        st_p[3].append(kvw.reshape(pb, pt_len, 2, g, NSA_DIM)[:, pt_len - min(WINDOW, pt_len):])
    y_prompt = x.reshape(pb, pt_len, d)

    ms = sb * st_len
    x = x_sample.reshape(ms, d)
    st_s = [[], [], [], []]
    cmp_pages = cache_cmp_kv.transpose(0, 1, 3, 4, 5, 2).reshape(depth * n_pool, KV_ROW, PAGE_SIZE)
    slc_pages = cache_slc_kv.transpose(0, 1, 3, 4, 5, 2).reshape(depth * n_pool, KV_ROW, PAGE_SIZE)
    mla_pages = cache_mla.transpose(0, 1, 3, 2).reshape(depth * n_pool, LAT, PAGE_SIZE)
    win_state = state_win_kv.transpose(0, 1, 3, 4, 5, 2).reshape(depth, sb, KV_ROW, wlen)
    for l in range(depth):
        lw = layers[l]
        mod = jnp.repeat(mod_all[l, pb:], st_len, axis=0).reshape(ms, 6, d)
        (q, kvc, kvs, kvw, _, _, _, _, _, _, gates, qmla, lat, _, _, gm) = _inproj(
            x, mod, True, lw, cos_s, sin_s, 1, ms, ms)
        pt_l = pt_flat + l * n_pool
        tok = _compress_t(cmp_pages, pt_l, lw)
        kc, vc = _split_compressed(tok, sb)
        q5 = q.reshape(g, NSA_GROUP, sb, st_len, NSA_DIM).transpose(2, 3, 0, 1, 4)
        gate_cols = gates.reshape(sb, st_len, g, GATE_LANES)[..., :3 * NSA_GROUP]
        gate_cols = gate_cols.reshape(sb, st_len, g, 3, NSA_GROUP).transpose(0, 2, 3, 1, 4)
        gate_rows = gate_cols.reshape(sb, g, 3, st_len * NSA_GROUP, 1)
        o_cmp, imp = _cmp_sample(q5, kc, vc, cmp_bias_s, gate_rows, sb, st_len)
        idx = _topk_sample(imp.reshape(sb * g * st_len, n_past_blocks), n_pick)
        idx_flat = idx[:, :n_pick].reshape(sb, g, st_len, n_pick).transpose(0, 2, 1, 3).reshape(-1)
        pad8 = lambda a: jnp.pad(a.reshape(sb, st_len, -1), ((0, 0), (0, 8 - st_len), (0, 0)))
        o_slc = _slc_sample(idx_flat, pt_l, slc_pages, q5, pad8(kvs), slc_near_s, slc_cur_s,
                            gate_cols.reshape(sb, g, 3, st_len, NSA_GROUP, 1)[:, :, 1:2], sb, st_len, n_pick,
                            n_past_blocks, n_pages)
        o_win = _win_sample(q5, win_state, l, pad8(kvw), win_bias_s, win_new_s, gate_rows, sb, st_len)
        q_rows = qmla.reshape(MLA_HEADS, sb, st_len, LAT).transpose(1, 0, 2, 3).reshape(sb, MLA_HEADS * st_len, LAT)
        o_lat = _mla_sample(pt_l, mla_pages, q_rows, pad8(lat), sb, st_len, n_pages)
        o_lat = o_lat.reshape(sb, MLA_HEADS, st_len, KV_LORA).transpose(1, 0, 2, 3).reshape(1, MLA_HEADS, ms, KV_LORA)
        x = _merge(o_cmp.reshape(ms, d), o_slc.reshape(ms, d), o_win.reshape(ms, d), o_lat, gm, x, mod, True, lw,
                   1, ms, ms)
        x = _ffn(x, mod, True, lw, gfinal, l == depth - 1, 1, ms, ms)
        st_s[0].append(kvc.reshape(sb, st_len, 2, g, NSA_DIM))
        st_s[1].append(kvs.reshape(sb, st_len, 2, g, NSA_DIM))
        st_s[2].append(lat.reshape(sb, st_len, LAT))
        win_all = jnp.concatenate([state_win_kv[l], kvw.reshape(sb, st_len, 2, g, NSA_DIM)], axis=1)
        st_s[3].append(win_all[:, st_len:])
    y_sample = x.reshape(sb, st_len, d)

    return (y_prompt, y_sample, jnp.stack(st_p[0]), jnp.stack(st_p[1]), jnp.stack(st_p[2]), jnp.stack(st_p[3]),
            jnp.stack(st_s[0]), jnp.stack(st_s[1]), jnp.stack(st_s[2]), jnp.stack(st_s[3]))
```

```python
import functools
import math

import numpy as np
import jax
import jax.numpy as jnp
from jax import lax
from jax.experimental import pallas as pl
from jax.experimental.pallas import tpu as pltpu

F32 = jnp.float32
BF16 = jnp.bfloat16

D_MODEL = 1024
PAGE_SIZE = 128
NSA_HEADS = 16
NSA_KV_HEADS = 2
NSA_GROUP = NSA_HEADS // NSA_KV_HEADS
NSA_DIM = D_MODEL // NSA_HEADS
CMP_BLOCK = 32
SLC_BLOCK = 64
N_SELECT = 16
WINDOW = 512
MLA_HEADS = 8
MLA_NOPE = 128
MLA_ROPE = 64
MLA_V = D_MODEL // MLA_HEADS
Q_LORA = 384
KV_LORA = 256
LAT = KV_LORA + MLA_ROPE
ROPE_THETA = 10000.0
N_BUCKETS = 32
MAX_DISTANCE = 128
NSA_SCALE = NSA_DIM ** -0.5
MLA_SCALE = (MLA_NOPE + MLA_ROPE) ** -0.5
NEG_BIG = -1e30
RMS_EPS = 1e-6
KV_ROW = 2 * NSA_KV_HEADS * NSA_DIM
GATE_LANES = 128

SEG_Q = (0, 1024)
SEG_CMP = (1024, 1280)
SEG_SLC = (1280, 1536)
SEG_WIN = (1536, 1792)
SEG_GATE = (1792, 2048)
SEG_CQ = (2048, 2432)
SEG_CKV = (2432, 2688)
SEG_KR = (2688, 2816)
SEG_GM = (2816, 4864)
D_IN_PACKED = 4864

ATT_TILE = 256
FLASH_ROWS = 128
MLA_FLASH_ROWS = 512
SEL_LANES = 128
VMEM_LIMIT = 56 * 1024 * 1024
CMP_PAGES = 64
MLA_PAGES = 32


def _cparams(sem):
    return pltpu.CompilerParams(dimension_semantics=sem, vmem_limit_bytes=VMEM_LIMIT)


def _dot(a, b):
    return jnp.dot(a, b, preferred_element_type=F32)


def _dot_nt(a, b):
    return lax.dot_general(a, b, (((1,), (1,)), ((), ())), preferred_element_type=F32)


def _rms(x, g):
    return x * lax.rsqrt(jnp.mean(x * x, axis=-1, keepdims=True) + RMS_EPS) * g


def _rope_lanes(x, cos2, sin2):
    w = x.shape[-1]
    lane = lax.broadcasted_iota(jnp.int32, x.shape, 1)
    swapped = jnp.where(lane % MLA_ROPE < MLA_ROPE // 2, pltpu.roll(x, w - MLA_ROPE // 2, 1),
                        pltpu.roll(x, MLA_ROPE // 2, 1))
    return x * cos2 + swapped * sin2


def _t5_bucket_np(dist):
    max_exact = N_BUCKETS // 2
    d = np.maximum(dist, 0)
    log_ratio = np.log(np.maximum(d, 1).astype(np.float32) / max_exact) / math.log(MAX_DISTANCE / max_exact)
    large = np.minimum(max_exact + (log_ratio * (N_BUCKETS - max_exact)).astype(np.int32), N_BUCKETS - 1)
    return np.where(d < max_exact, d, large).astype(np.int32)


def _rope_tables(pos, reps):
    half = MLA_ROPE // 2
    inv = ROPE_THETA ** (-jnp.arange(half, dtype=F32) / half)
    ang = pos.astype(F32)[:, None] * inv[None, :]
    cos, sin = jnp.cos(ang), jnp.sin(ang)
    cos2 = jnp.concatenate([cos, cos], axis=-1)
    sin2 = jnp.concatenate([-sin, sin], axis=-1)
    return jnp.tile(cos2, (1, reps)), jnp.tile(sin2, (1, reps))


def _bias_kernel(table_ref, bucket_ref, mask_ref, o_ref, *, shift):
    h = pl.program_id(0)
    bucket = bucket_ref[...]
    acc = jnp.zeros(bucket.shape, F32)
    for b in range(N_BUCKETS):
        acc = jnp.where(bucket == b, table_ref[b, h], acc)
    if shift:
        acc = acc - table_ref[N_BUCKETS - 1, h]
    o_ref[0] = acc + mask_ref[...]


def _expand_bias(table, bucket, addmask, shift):
    r, c = bucket.shape
    return pl.pallas_call(
        functools.partial(_bias_kernel, shift=shift),
        grid=(NSA_HEADS,),
        in_specs=[pl.BlockSpec(memory_space=pltpu.SMEM),
                  pl.BlockSpec((r, c), lambda h: (0, 0)),
                  pl.BlockSpec((r, c), lambda h: (0, 0))],
        out_specs=pl.BlockSpec((1, r, c), lambda h: (h, 0, 0)),
        out_shape=jax.ShapeDtypeStruct((NSA_HEADS, r, c), F32),
        compiler_params=_cparams(("arbitrary",)),
        name="bias_expand",
    )(table, jnp.asarray(bucket), jnp.asarray(addmask))


def _mod_kernel(c_ref, w_ref, b_ref, o_ref):
    c = c_ref[...]
    cond = (c * jax.nn.sigmoid(c)).astype(BF16)
    o_ref[0] = _dot(cond, w_ref[0].astype(BF16)) + b_ref[0]


def _modulation(c_all, ada_w, ada_b):
    depth, d, n = ada_w.shape
    rows = c_all.shape[0]
    tn = 1536
    return pl.pallas_call(
        _mod_kernel,
        grid=(depth, n // tn),
        in_specs=[pl.BlockSpec((rows, d), lambda l, j: (0, 0)),
                  pl.BlockSpec((1, d, tn), lambda l, j: (l, 0, j)),
                  pl.BlockSpec((1, 1, tn), lambda l, j: (l, 0, j))],
        out_specs=pl.BlockSpec((1, rows, tn), lambda l, j: (l, 0, j)),
        out_shape=jax.ShapeDtypeStruct((depth, rows, n), F32),
        compiler_params=_cparams(("arbitrary", "arbitrary")),
        name="adaln_mod",
    )(c_all, ada_w, ada_b.reshape(depth, 1, n))


def _inproj_kernel(x_ref, mod_ref, gn_ref, w_ref, cos_ref, sin_ref, gq_ref, wuq_ref, wuk_ref, gkv_ref,
                   q_ref, kvc_ref, kvs_ref, kvw_ref, kst_ref, kse_ref, kso_ref, kwt_ref, kwe_ref, kwo_ref, gate_ref,
                   qmla_ref, lat_ref, latb_ref, latt_ref, gm_ref):
    x = x_ref[...]
    y = _rms(x, gn_ref[...])
    h = (y * (1.0 + mod_ref[:, 1, :]) + mod_ref[:, 0, :]).astype(BF16)

    def seg(s):
        return _dot(h, w_ref[:, s[0]:s[1]])

    q = (seg(SEG_Q) * NSA_SCALE).astype(BF16)
    for hd in range(NSA_HEADS):
        q_ref[0, hd] = q[:, hd * NSA_DIM:(hd + 1) * NSA_DIM]
    kvc_ref[...] = seg(SEG_CMP)
    half = NSA_KV_HEADS * NSA_DIM
    low = lax.broadcasted_iota(jnp.int32, (x.shape[0], half), 1) < NSA_DIM
    for s, kv_ref, kt_ref, ve_ref, vo_ref in ((SEG_SLC, kvs_ref, kst_ref, kse_ref, kso_ref),
                                              (SEG_WIN, kvw_ref, kwt_ref, kwe_ref, kwo_ref)):
        kv = seg(s)
        kv_ref[...] = kv
        kt = kv[:, 0:half].T.astype(BF16)
        vv = kv[:, half:2 * half]
        vr = pltpu.roll(vv, NSA_DIM, 1)
        for g in range(NSA_KV_HEADS):
            kt_ref[0, g] = kt[g * NSA_DIM:(g + 1) * NSA_DIM]
            ve_ref[0, g] = jnp.where(low, vv if g == 0 else vr, 1.0).astype(BF16)
            vo_ref[0, g] = jnp.where(low, 1.0, vr if g == 0 else vv).astype(BF16)
    gate_ref[...] = jax.nn.sigmoid(seg(SEG_GATE))
    gm_ref[...] = jax.nn.sigmoid(seg(SEG_GM))

    cos = cos_ref[...]
    sin = sin_ref[...]
    cqn = _rms(seg(SEG_CQ), gq_ref[...]).astype(BF16)
    qm = _dot(cqn, wuq_ref[...])
    nope_w = MLA_HEADS * MLA_NOPE
    qr = _rope_lanes(qm[:, nope_w:], cos, sin) * MLA_SCALE
    for hd in range(MLA_HEADS):
        qn = qm[:, hd * MLA_NOPE:(hd + 1) * MLA_NOPE].astype(BF16)
        qmla_ref[0, hd, :, 0:KV_LORA] = (_dot(qn, wuk_ref[hd]) * MLA_SCALE).astype(BF16)
        qmla_ref[0, hd, :, KV_LORA:LAT] = qr[:, hd * MLA_ROPE:(hd + 1) * MLA_ROPE].astype(BF16)
    ckv = _rms(seg(SEG_CKV), gkv_ref[...])
    kr = _rope_lanes(seg(SEG_KR), cos[:, 0:128], sin[:, 0:128])
    lat_ref[:, 0:KV_LORA] = ckv
    lat_ref[:, KV_LORA:LAT] = kr[:, 0:MLA_ROPE]
    latb_ref[...] = ckv.astype(BF16)
    latt_ref[0, 0:KV_LORA, :] = ckv.T.astype(BF16)
    latt_ref[0, KV_LORA:LAT, :] = kr.T[0:MLA_ROPE].astype(BF16)


def _inproj(x, mod, per_row_mod, lw, cos_t, sin_t, nb, nt, tm):
    m = nb * nt
    tpb = nt // tm
    d = D_MODEL
    if per_row_mod:
        mod_spec = pl.BlockSpec((tm, 6, d), lambda i: (i, 0, 0))
    else:
        mod_spec = pl.BlockSpec((1, 6, d), lambda i: (i // tpb, 0, 0))
    const2 = lambda i: (0, 0)
    row = lambda i: (i, 0)
    bt = lambda i: (i // tpb, 0, i % tpb, 0)
    btt = lambda i: (i // tpb, 0, 0, i % tpb)
    g = NSA_KV_HEADS
    out_shape = (
        jax.ShapeDtypeStruct((nb, NSA_HEADS, nt, NSA_DIM), BF16),
        jax.ShapeDtypeStruct((m, KV_ROW), F32),
        jax.ShapeDtypeStruct((m, KV_ROW), F32),
        jax.ShapeDtypeStruct((m, KV_ROW), F32),
        jax.ShapeDtypeStruct((nb, g, NSA_DIM, nt), BF16),
        jax.ShapeDtypeStruct((nb, g, nt, 2 * NSA_DIM), BF16),
        jax.ShapeDtypeStruct((nb, g, nt, 2 * NSA_DIM), BF16),
        jax.ShapeDtypeStruct((nb, g, NSA_DIM, nt), BF16),
        jax.ShapeDtypeStruct((nb, g, nt, 2 * NSA_DIM), BF16),
        jax.ShapeDtypeStruct((nb, g, nt, 2 * NSA_DIM), BF16),
        jax.ShapeDtypeStruct((m, g * GATE_LANES), F32),
        jax.ShapeDtypeStruct((nb, MLA_HEADS, nt, LAT), BF16),
        jax.ShapeDtypeStruct((m, LAT), F32),
        jax.ShapeDtypeStruct((m, KV_LORA), BF16),
        jax.ShapeDtypeStruct((nb, LAT, nt), BF16),
        jax.ShapeDtypeStruct((m, 2 * d), F32),
    )
    kt_spec = pl.BlockSpec((1, g, NSA_DIM, tm), btt)
    v_spec = pl.BlockSpec((1, g, tm, 2 * NSA_DIM), bt)
    out_specs = (
        pl.BlockSpec((1, NSA_HEADS, tm, NSA_DIM), bt),
        pl.BlockSpec((tm, KV_ROW), row), pl.BlockSpec((tm, KV_ROW), row), pl.BlockSpec((tm, KV_ROW), row),
        kt_spec, v_spec, v_spec, kt_spec, v_spec, v_spec,
        pl.BlockSpec((tm, g * GATE_LANES), row),
        pl.BlockSpec((1, MLA_HEADS, tm, LAT), bt),
        pl.BlockSpec((tm, LAT), row), pl.BlockSpec((tm, KV_LORA), row),
        pl.BlockSpec((1, LAT, tm), lambda i: (i // tpb, 0, i % tpb)),
        pl.BlockSpec((tm, 2 * d), row),
    )
    in_specs = [
        pl.BlockSpec((tm, d), row), mod_spec, pl.BlockSpec((1, d), const2),
        pl.BlockSpec((d, D_IN_PACKED), const2),
        pl.BlockSpec((tm, MLA_HEADS * MLA_ROPE), lambda i: (i % tpb, 0)),
        pl.BlockSpec((tm, MLA_HEADS * MLA_ROPE), lambda i: (i % tpb, 0)),
        pl.BlockSpec((1, Q_LORA), const2),
        pl.BlockSpec((Q_LORA, MLA_HEADS * (MLA_NOPE + MLA_ROPE)), const2),
        pl.BlockSpec((MLA_HEADS, MLA_NOPE, KV_LORA), lambda i: (0, 0, 0)),
        pl.BlockSpec((1, KV_LORA), const2),
    ]
    return pl.pallas_call(
        _inproj_kernel, grid=(m // tm,), in_specs=in_specs, out_specs=out_specs, out_shape=out_shape,
        compiler_params=_cparams(("arbitrary",)), name="in_proj",
    )(x, mod, lw["norm_mix_g"], lw["w_in"], cos_t, sin_t, lw["mla_q_norm_g"], lw["w_uq"], lw["w_ukT"],
      lw["mla_kv_norm_g"])


def _gelu_tanh(z):
    return 0.5 * z * (1.0 + jnp.tanh(math.sqrt(2.0 / math.pi) * (z + 0.044715 * (z * z * z))))


def _compress_t_kernel(tab_ref, *refs, pages):
    del tab_ref
    page_refs = refs[:pages]
    pet_ref, sel_ref, w1_ref, b1_ref, w2_ref, o_ref, slab_ref = refs[pages:]
    pairs = pages // 2
    rows_per_pair = 2 * (PAGE_SIZE // CMP_BLOCK)
    sel = sel_ref[...]
    pet = pet_ref[...]
    for pr in range(pairs):
        xt2 = jnp.concatenate([page_refs[2 * pr][0], page_refs[2 * pr + 1][0]], axis=1)
        slab_ref[pr] = _dot_nt(sel, (xt2 + pet).astype(BF16))
    acc = jnp.zeros((pairs * rows_per_pair, KV_ROW), F32)
    for r in range(CMP_BLOCK):
        xr = jnp.concatenate([slab_ref[pr, r * rows_per_pair:(r + 1) * rows_per_pair, :] for pr in range(pairs)],
                             axis=0)
        acc = acc + _dot(xr.astype(BF16), w1_ref[r])
    hmid = _gelu_tanh(acc + b1_ref[...])
    o_ref[0] = _dot(hmid.astype(BF16), w2_ref[...])


def _compress_t(pages3d, page_ids, lw):
    n_logical = page_ids.shape[0]
    pages = min(CMP_PAGES, n_logical)
    assert n_logical % pages == 0 and pages % 2 == 0
    steps = n_logical // pages
    per_page = PAGE_SIZE // CMP_BLOCK

    def page_map(k):
        return lambda s, tab: (tab[s * pages + k], 0, 0)

    in_specs = [pl.BlockSpec((1, KV_ROW, PAGE_SIZE), page_map(k)) for k in range(pages)]
    in_specs += [
        pl.BlockSpec((KV_ROW, 2 * PAGE_SIZE), lambda s, tab: (0, 0)),
        pl.BlockSpec((2 * PAGE_SIZE, 2 * PAGE_SIZE), lambda s, tab: (0, 0)),
        pl.BlockSpec((CMP_BLOCK, KV_ROW, KV_ROW), lambda s, tab: (0, 0, 0)),
        pl.BlockSpec((1, KV_ROW), lambda s, tab: (0, 0)),
        pl.BlockSpec((KV_ROW, KV_ROW), lambda s, tab: (0, 0)),
    ]
    r, pg, c = np.meshgrid(np.arange(CMP_BLOCK), np.arange(2), np.arange(per_page), indexing="ij")
    src = (pg * PAGE_SIZE + c * CMP_BLOCK + r).reshape(-1)
    sel = jnp.asarray((src[:, None] == np.arange(2 * PAGE_SIZE)[None, :]).astype(np.float32), dtype=BF16)
    out = pl.pallas_call(
        functools.partial(_compress_t_kernel, pages=pages),
        grid_spec=pltpu.PrefetchScalarGridSpec(
            num_scalar_prefetch=1, grid=(steps,), in_specs=in_specs,
            out_specs=pl.BlockSpec((1, per_page * pages, KV_ROW), lambda s, tab: (s, 0, 0)),
            scratch_shapes=[pltpu.VMEM((pages // 2, 2 * PAGE_SIZE, KV_ROW), F32)]),
        out_shape=jax.ShapeDtypeStruct((steps, per_page * pages, KV_ROW), F32),
        compiler_params=_cparams(("arbitrary",)), name="cmp_compress_paged",
    )(page_ids, *([pages3d] * pages), lw["cmp_pe_t"], sel, lw["cmp_w1"], lw["cmp_b1"], lw["cmp_w2"])
    return out.reshape(n_logical * per_page, KV_ROW)


def _compress_kernel(tab_ref, *refs, pages):
    del tab_ref
    page_refs = refs[:pages]
    pe_ref, w1_ref, b1_ref, w2_ref, o_ref, slabk_ref, slabv_ref = refs[pages:]
    half = KV_ROW // 2
    for k in range(pages):
        slabk_ref[k * PAGE_SIZE:(k + 1) * PAGE_SIZE, :] = page_refs[k][0, :, 0:half]
        slabv_ref[k * PAGE_SIZE:(k + 1) * PAGE_SIZE, :] = page_refs[k][0, :, half:KV_ROW]
    per_page = PAGE_SIZE // CMP_BLOCK
    acc = jnp.zeros((per_page * pages, KV_ROW), F32)
    for r in range(CMP_BLOCK):
        rows = [jnp.concatenate([slab[pl.ds(c * CMP_BLOCK + r, pages, stride=PAGE_SIZE), :]
                                 for slab in (slabk_ref, slabv_ref)], axis=1) for c in range(per_page)]
        xr = (jnp.concatenate(rows, axis=0) + pe_ref[r]).astype(BF16)
        acc = acc + _dot(xr, w1_ref[r])
    hmid = _gelu_tanh(acc + b1_ref[...])
    o_ref[0] = _dot(hmid.astype(BF16), w2_ref[...])


def _compress(rows3d, page_ids, lw):
    n_logical = page_ids.shape[0]
    pages = min(CMP_PAGES, n_logical)
    assert n_logical % pages == 0
    steps = n_logical // pages
    per_page = PAGE_SIZE // CMP_BLOCK

    def page_map(k):
        return lambda s, tab: (tab[s * pages + k], 0, 0)

    in_specs = [pl.BlockSpec((1, PAGE_SIZE, KV_ROW), page_map(k)) for k in range(pages)]
    in_specs += [
        pl.BlockSpec((CMP_BLOCK, 1, KV_ROW), lambda s, tab: (0, 0, 0)),
        pl.BlockSpec((CMP_BLOCK, KV_ROW, KV_ROW), lambda s, tab: (0, 0, 0)),
        pl.BlockSpec((1, KV_ROW), lambda s, tab: (0, 0)),
        pl.BlockSpec((KV_ROW, KV_ROW), lambda s, tab: (0, 0)),
    ]
    out = pl.pallas_call(
        functools.partial(_compress_kernel, pages=pages),
        grid_spec=pltpu.PrefetchScalarGridSpec(
            num_scalar_prefetch=1, grid=(steps,), in_specs=in_specs,
            out_specs=pl.BlockSpec((1, per_page * pages, KV_ROW), lambda s, tab: (s, 0, 0)),
            scratch_shapes=[pltpu.VMEM((pages * PAGE_SIZE, KV_ROW // 2), F32),
                            pltpu.VMEM((pages * PAGE_SIZE, KV_ROW // 2), F32)]),
        out_shape=jax.ShapeDtypeStruct((steps, per_page * pages, KV_ROW), F32),
        compiler_params=_cparams(("arbitrary",)), name="cmp_compress",
    )(page_ids, *([rows3d] * pages), lw["cmp_pe"], lw["cmp_w1"], lw["cmp_b1"], lw["cmp_w2"])
    out = out.reshape(steps, per_page, pages, KV_ROW).transpose(0, 2, 1, 3)
    return out.reshape(n_logical * per_page, KV_ROW)


def _split_compressed(tok, nb):
    n = tok.shape[0] // nb
    t = tok.reshape(nb, n // 2, 2, 2, NSA_KV_HEADS, NSA_DIM)
    t = t.transpose(3, 0, 4, 2, 1, 5).reshape(2, nb, NSA_KV_HEADS, n, NSA_DIM).astype(BF16)
    return t[0], t[1]


def _cmp_prompt_kernel(q_ref, kc_ref, vct_ref, bias_ref, gate_ref, o_ref, sel_ref, *, tq, n_cmp):
    qi = pl.program_id(2)
    kc = kc_ref[0, 0]
    vct = vct_ref[0, 0]
    maskf = (bias_ref[0] > 0.5 * NEG_BIG).astype(F32)
    gates_t = gate_ref[0].T
    zs = [_dot_nt(kc, q_ref[0, hh]) + bias_ref[hh] for hh in range(NSA_GROUP)]
    es = [jnp.exp(z - jnp.max(z, axis=0, keepdims=True)) * maskf for z in zs]
    ps = [e / jnp.maximum(jnp.sum(e, axis=0, keepdims=True), 1e-30) for e in es]
    outs = [_dot(vct, ps[hh].astype(BF16)) * gates_t[hh:hh + 1, :] for hh in range(NSA_GROUP)]
    imp = ps[0]
    for p in ps[1:]:
        imp = imp + p
    o_ref[0] = jnp.concatenate(outs, axis=0).T
    n_slc = n_cmp // 2
    imp_slc = imp[0:n_slc] + imp[n_slc:n_cmp]
    t = qi * tq + lax.broadcasted_iota(jnp.int32, (n_slc, tq), 1)
    j = lax.broadcasted_iota(jnp.int32, (n_slc, tq), 0)
    cur = t // SLC_BLOCK
    forced = (j == 0) | (j == cur) | (j == cur - 1)
    score = jnp.where(forced, jnp.inf, jnp.where(j <= cur, imp_slc, -jnp.inf))
    rank = jnp.zeros((n_slc, tq), F32)
    for i in range(n_slc):
        ci = score[i:i + 1, :]
        rank = rank + jnp.where(j > i, jnp.where(ci >= score, 1.0, 0.0), jnp.where(ci > score, 1.0, 0.0))
    sel_t = jnp.where(rank < float(min(N_SELECT, n_slc)), 1.0, 0.0)
    sel_pad = jnp.concatenate([sel_t, jnp.zeros((SEL_LANES - n_slc, tq), F32)], axis=0)
    sel_ref[0, 0] = sel_pad.T


def _cmp_prompt(q, kc, vct, bias_t, gates, nb, nt):
    tq = ATT_TILE
    n_cmp = kc.shape[2]
    g = NSA_KV_HEADS
    gw = NSA_GROUP * NSA_DIM
    return pl.pallas_call(
        functools.partial(_cmp_prompt_kernel, tq=tq, n_cmp=n_cmp),
        grid=(nb, g, nt // tq),
        in_specs=[pl.BlockSpec((1, NSA_GROUP, tq, NSA_DIM), lambda b, gi, i: (b, gi, i, 0)),
                  pl.BlockSpec((1, 1, n_cmp, NSA_DIM), lambda b, gi, i: (b, gi, 0, 0)),
                  pl.BlockSpec((1, 1, NSA_DIM, n_cmp), lambda b, gi, i: (b, gi, 0, 0)),
                  pl.BlockSpec((NSA_GROUP, n_cmp, tq), lambda b, gi, i: (gi, 0, i)),
                  pl.BlockSpec((1, tq, GATE_LANES), lambda b, gi, i: (b, i, gi))],
        out_specs=(pl.BlockSpec((1, tq, gw), lambda b, gi, i: (b, i, gi)),
                   pl.BlockSpec((1, 1, tq, SEL_LANES), lambda b, gi, i: (b, gi, i, 0))),
        out_shape=(jax.ShapeDtypeStruct((nb, nt, D_MODEL), F32),
                   jax.ShapeDtypeStruct((nb, g, nt, SEL_LANES), F32)),
        compiler_params=_cparams(("arbitrary", "arbitrary", "arbitrary")), name="cmp_attn_select",
    )(q, kc, vct, bias_t, gates.reshape(nb, nt, g * GATE_LANES))


def _flash_update(s, vs, m_ref, l_ref, acc_ref):
    tk = s.shape[-1]
    m_old = m_ref[...]
    m_new = jnp.maximum(m_old, jnp.max(s, axis=-1, keepdims=True))
    alpha = jnp.exp(m_old - m_new)
    p = jnp.exp(s - jnp.concatenate([m_new] * (tk // 128), axis=1))
    psum = p[:, 0:128]
    for c in range(1, tk // 128):
        psum = psum + p[:, c * 128:(c + 1) * 128]
    l_ref[...] = alpha * l_ref[...] + psum
    dv = acc_ref.shape[-1]
    acc_ref[...] = jnp.concatenate([alpha] * (dv // 128), axis=1) * acc_ref[...] + _dot(p.astype(BF16), vs)
    m_ref[...] = m_new


def _nsa_flash_kernel(*refs, tq, slc):
    if slc:
        q_ref, k_ref, ve_ref, vo_ref, bias_ref, gate_ref, ge_ref, sel_ref, exp_ref, o_ref, m_ref, acc_ref = refs
    else:
        q_ref, k_ref, ve_ref, vo_ref, bias_ref, gate_ref, ge_ref, o_ref, m_ref, acc_ref = refs
    qi = pl.program_id(2)
    ch = FLASH_ROWS
    nsub = tq // ch
    m_ref[...] = jnp.full(m_ref.shape, NEG_BIG, F32)
    acc_ref[...] = jnp.zeros(acc_ref.shape, F32)
    if slc:
        sel = sel_ref[0, 0].astype(BF16)

    def tile(kt, btype):
        start = pl.multiple_of(kt * tq, tq)
        kt_tile = k_ref[0, 0, :, pl.ds(start, tq)]
        vs = (ve_ref[0, 0, pl.ds(start, tq), :], vo_ref[0, 0, pl.ds(start, tq), :])
        if slc:
            maskadd = (_dot(sel, exp_ref[kt]) - 1.0) * (-NEG_BIG)
        for c in range(NSA_GROUP * nsub):
            hh, qs = divmod(c, nsub)
            qrows = slice(qs * ch, (qs + 1) * ch)
            srows = slice(c * ch, (c + 1) * ch)
            s = _dot(q_ref[0, hh, qrows, :], kt_tile)
            if btype is not None:
                s = s + bias_ref[hh, btype, qrows, :]
            if slc:
                s = s + maskadd[qrows]
            m_old = m_ref[srows]
            m_new = jnp.maximum(m_old, jnp.max(s, axis=-1, keepdims=True))
            alpha = jnp.exp(m_old - m_new)
            p = jnp.exp(s - jnp.concatenate([m_new] * (tq // 128), axis=1))
            acc_ref[srows] = alpha * acc_ref[srows] + _dot(p.astype(BF16), vs[hh % 2])
            m_ref[srows] = m_new

    if slc:
        def far(kt, carry):
            tile(kt, None)
            return carry
        lax.fori_loop(0, jnp.maximum(qi - 1, 0), far, 0)
    else:
        @pl.when(qi >= 2)
        def _():
            tile(qi - 2, 2)

    @pl.when(qi >= 1)
    def _():
        tile(qi - 1, 1)

    tile(qi, 0)
    gexp = _expand_gates(gate_ref[0], ge_ref[...])
    lane = lax.broadcasted_iota(jnp.int32, (tq, 2 * NSA_DIM), 1)
    for j in range(NSA_GROUP // 2):
        a_e = acc_ref[(2 * j) * tq:(2 * j + 1) * tq]
        a_o = acc_ref[(2 * j + 1) * tq:(2 * j + 2) * tq]
        num = jnp.where(lane < NSA_DIM, a_e, a_o)
        den = pltpu.roll(jnp.where(lane < NSA_DIM, a_o, a_e), NSA_DIM, 1)
        cols = slice(j * 2 * NSA_DIM, (j + 1) * 2 * NSA_DIM)
        o_ref[0, :, cols] = num / den * gexp[:, cols]


def _expand_gates(g, expand):
    g1 = g.astype(BF16)
    r1 = g - g1.astype(F32)
    g2 = r1.astype(BF16)
    g3 = (r1 - g2.astype(F32)).astype(BF16)
    return _dot(g1, expand) + _dot(g2, expand) + _dot(g3, expand)


def _gate_expand_matrix(branch):
    k = np.arange(GATE_LANES)[:, None]
    n = np.arange(NSA_GROUP * NSA_DIM)[None, :]
    return jnp.asarray((k == branch * NSA_GROUP + n // NSA_DIM).astype(np.float32), dtype=BF16)


def _nsa_flash(q, kt, ve, vo, bias_tiles, gates, nb, nt, sel=None, expand=None):
    tq = ATT_TILE
    g = NSA_KV_HEADS
    gw = NSA_GROUP * NSA_DIM
    slc = sel is not None
    v_spec = pl.BlockSpec((1, 1, nt, 2 * NSA_DIM), lambda b, gi, i: (b, gi, 0, 0))
    in_specs = [pl.BlockSpec((1, NSA_GROUP, tq, NSA_DIM), lambda b, gi, i: (b, gi, i, 0)),
                pl.BlockSpec((1, 1, NSA_DIM, nt), lambda b, gi, i: (b, gi, 0, 0)),
                v_spec, v_spec,
                pl.BlockSpec((NSA_GROUP, 3, tq, tq), lambda b, gi, i: (gi, 0, 0, 0)),
                pl.BlockSpec((1, tq, GATE_LANES), lambda b, gi, i: (b, i, gi)),
                pl.BlockSpec((GATE_LANES, gw), lambda b, gi, i: (0, 0))]
    args = [q, kt, ve, vo, bias_tiles, gates.reshape(nb, nt, g * GATE_LANES), _gate_expand_matrix(1 if slc else 2)]
    if slc:
        in_specs += [pl.BlockSpec((1, 1, tq, SEL_LANES), lambda b, gi, i: (b, gi, i, 0)),
                     pl.BlockSpec((nt // tq, SEL_LANES, tq), lambda b, gi, i: (0, 0, 0))]
        args += [sel, expand]
    return pl.pallas_call(
        functools.partial(_nsa_flash_kernel, tq=tq, slc=slc),
        grid=(nb, g, nt // tq),
        in_specs=in_specs,
        out_specs=pl.BlockSpec((1, tq, gw), lambda b, gi, i: (b, i, gi)),
        out_shape=jax.ShapeDtypeStruct((nb, nt, D_MODEL), F32),
        scratch_shapes=[pltpu.VMEM((NSA_GROUP * tq, 128), F32), pltpu.VMEM((NSA_GROUP * tq, 2 * NSA_DIM), F32)],
        compiler_params=_cparams(("arbitrary", "arbitrary", "arbitrary")),
        name="slc_attn" if slc else "win_attn",
    )(*args)


def _mla_prompt_kernel(q_ref, latt_ref, latv_ref, o_ref, m_ref, l_ref, acc_ref, *, tq):
    qi = pl.program_id(1)
    rows = MLA_HEADS * tq
    m_ref[...] = jnp.full(m_ref.shape, NEG_BIG, F32)
    l_ref[...] = jnp.zeros(l_ref.shape, F32)
    acc_ref[...] = jnp.zeros(acc_ref.shape, F32)

    heads_per_chunk = MLA_FLASH_ROWS // tq
    n_chunks = MLA_HEADS // heads_per_chunk

    def tile(kt, diag):
        start = pl.multiple_of(kt * tq, tq)
        kt_tile = latt_ref[0, :, pl.ds(start, tq)]
        vs = latv_ref[0, pl.ds(start, tq), :]

        def scores(c):
            qc = q_ref[0, c * heads_per_chunk:(c + 1) * heads_per_chunk].reshape(MLA_FLASH_ROWS, LAT)
            s = _dot(qc, kt_tile)
            if diag:
                row = lax.broadcasted_iota(jnp.int32, (tq, tq), 0)
                col = lax.broadcasted_iota(jnp.int32, (tq, tq), 1)
                s = jnp.where((col <= row)[None], s.reshape(heads_per_chunk, tq, tq), NEG_BIG)
                s = s.reshape(MLA_FLASH_ROWS, tq)
            return s

        s_next = scores(0)
        for c in range(n_chunks):
            s_cur = s_next
            if c + 1 < n_chunks:
                s_next = scores(c + 1)
            srows = slice(c * MLA_FLASH_ROWS, (c + 1) * MLA_FLASH_ROWS)
            _flash_update(s_cur, vs, m_ref.at[srows], l_ref.at[srows], acc_ref.at[srows])

    def body(kt, carry):
        tile(kt, False)
        return carry
    lax.fori_loop(0, qi, body, 0)
    tile(qi, True)
    o = acc_ref[...] / jnp.sum(l_ref[...], axis=-1, keepdims=True)
    o_ref[0] = o.reshape(MLA_HEADS, tq, KV_LORA).astype(BF16)


def _mla_prompt(qmla, latt, latv, nb, nt):
    tq = ATT_TILE
    return pl.pallas_call(
        functools.partial(_mla_prompt_kernel, tq=tq),
        grid=(nb, nt // tq),
        in_specs=[pl.BlockSpec((1, MLA_HEADS, tq, LAT), lambda b, i: (b, 0, i, 0)),
                  pl.BlockSpec((1, LAT, nt), lambda b, i: (b, 0, 0)),
                  pl.BlockSpec((1, nt, KV_LORA), lambda b, i: (b, 0, 0))],
        out_specs=pl.BlockSpec((1, MLA_HEADS, tq, KV_LORA), lambda b, i: (b, 0, i, 0)),
        out_shape=jax.ShapeDtypeStruct((nb, MLA_HEADS, nt, KV_LORA), BF16),
        scratch_shapes=[pltpu.VMEM((MLA_HEADS * tq, 128), F32), pltpu.VMEM((MLA_HEADS * tq, 128), F32),
                        pltpu.VMEM((MLA_HEADS * tq, KV_LORA), F32)],
        compiler_params=_cparams(("arbitrary", "arbitrary")), name="mla_attn",
    )(qmla, latt, latv.reshape(nb, nt, KV_LORA))


def _merge_kernel(oc_ref, os_ref, ow_ref, ol_ref, gm_ref, x_ref, mod_ref, wuv_ref, wo_ref, o_ref):
    o_nsa = oc_ref[...] + os_ref[...] + ow_ref[...]
    o_mla = jnp.concatenate([_dot(ol_ref[0, hd], wuv_ref[hd]) for hd in range(MLA_HEADS)], axis=-1)
    gm = gm_ref[...]
    merged = (gm[:, 0:D_MODEL] * o_nsa + gm[:, D_MODEL:] * o_mla).astype(BF16)
    o_ref[...] = x_ref[...] + mod_ref[:, 2, :] * _dot(merged, wo_ref[...])


def _merge(o_cmp, o_slc, o_win, o_lat, gm, x, mod, per_row_mod, lw, nb, nt, tm):
    m = nb * nt
    tpb = nt // tm
    d = D_MODEL
    row = lambda i: (i, 0)
    if per_row_mod:
        mod_spec = pl.BlockSpec((tm, 6, d), lambda i: (i, 0, 0))
    else:
        mod_spec = pl.BlockSpec((1, 6, d), lambda i: (i // tpb, 0, 0))
    return pl.pallas_call(
        _merge_kernel, grid=(m // tm,),
        in_specs=[pl.BlockSpec((tm, d), row), pl.BlockSpec((tm, d), row), pl.BlockSpec((tm, d), row),
                  pl.BlockSpec((1, MLA_HEADS, tm, KV_LORA), lambda i: (i // tpb, 0, i % tpb, 0)),
                  pl.BlockSpec((tm, 2 * d), row), pl.BlockSpec((tm, d), row), mod_spec,
                  pl.BlockSpec((MLA_HEADS, KV_LORA, MLA_V), lambda i: (0, 0, 0)),
                  pl.BlockSpec((d, d), lambda i: (0, 0))],
        out_specs=pl.BlockSpec((tm, d), row),
        out_shape=jax.ShapeDtypeStruct((m, d), F32),
        compiler_params=_cparams(("arbitrary",)), name="merge_out_proj",
    )(o_cmp, o_slc, o_win, o_lat, gm, x, mod, lw["w_uv"], lw["w_out"])


def _ffn_kernel(x_ref, mod_ref, gn_ref, w1_ref, w3_ref, w2_ref, gf_ref, o_ref, h_ref, acc_ref, *, final):
    f = pl.program_id(1)

    @pl.when(f == 0)
    def _():
        y = _rms(x_ref[...], gn_ref[...])
        h_ref[...] = (y * (1.0 + mod_ref[:, 4, :]) + mod_ref[:, 3, :]).astype(BF16)
        acc_ref[...] = jnp.zeros(acc_ref.shape, F32)

    h = h_ref[...]
    a = _dot(h, w1_ref[...])
    b = _dot(h, w3_ref[...])
    act = (a * jax.nn.sigmoid(a) * b).astype(BF16)
    acc_ref[...] += _dot(act, w2_ref[...])

    @pl.when(f == pl.num_programs(1) - 1)
    def _():
        y = x_ref[...] + mod_ref[:, 5, :] * acc_ref[...]
        if final:
            y = _rms(y, gf_ref[...])
        o_ref[...] = y


def _ffn(x, mod, per_row_mod, lw, gfinal, final, nb, nt, tm):
    m = nb * nt
    tpb = nt // tm
    d = D_MODEL
    dff = lw["ffn_w2"].shape[0]
    tf = dff // 2
    nf = dff // tf
    row = lambda i, f: (i, 0)
    if per_row_mod:
        mod_spec = pl.BlockSpec((tm, 6, d), lambda i, f: (i, 0, 0))
    else:
        mod_spec = pl.BlockSpec((1, 6, d), lambda i, f: (i // tpb, 0, 0))
    return pl.pallas_call(
        functools.partial(_ffn_kernel, final=final), grid=(m // tm, nf),
        in_specs=[pl.BlockSpec((tm, d), row), mod_spec, pl.BlockSpec((1, d), lambda i, f: (0, 0)),
                  pl.BlockSpec((d, tf), lambda i, f: (0, f)),
                  pl.BlockSpec((d, tf), lambda i, f: (0, nf + f)),
                  pl.BlockSpec((tf, d), lambda i, f: (f, 0)),
                  pl.BlockSpec((1, d), lambda i, f: (0, 0))],
        out_specs=pl.BlockSpec((tm, d), row),
        out_shape=jax.ShapeDtypeStruct((m, d), F32),
        scratch_shapes=[pltpu.VMEM((tm, d), BF16), pltpu.VMEM((tm, d), F32)],
        compiler_params=_cparams(("arbitrary", "arbitrary")), name="ffn",
    )(x, mod, lw["norm_ffn_g"], lw["ffn_w13"], lw["ffn_w13"], lw["ffn_w2"], gfinal)


def _cmp_sample_kernel(q_ref, kc_ref, vc_ref, bias_ref, gate_ref, o_ref, imp_ref, *, nt, n_cmp):
    rows = nt * NSA_GROUP
    n_past = n_cmp // 2
    for g in range(NSA_KV_HEADS):
        qg = q_ref[0, :, g].reshape(rows, NSA_DIM)
        z = _dot_nt(qg, kc_ref[0, g]) + bias_ref[g]
        maskf = (bias_ref[g] > 0.5 * NEG_BIG).astype(F32)
        e = jnp.exp(z - jnp.max(z, axis=-1, keepdims=True)) * maskf
        p = e / jnp.maximum(jnp.sum(e, axis=-1, keepdims=True), 1e-30)
        o = _dot(p.astype(BF16), vc_ref[0, g]) * gate_ref[0, g, 0]
        o_ref[0, :, g] = o.reshape(nt, NSA_GROUP, NSA_DIM)
        imp = jnp.sum(p.reshape(nt, NSA_GROUP, n_cmp), axis=1)
        imp_ref[0, g] = imp[:, 0:n_past] + imp[:, n_past:n_cmp]


def _topk_sample_kernel(imp_ref, idx_ref, *, n_pick):
    imp = imp_ref[...]
    rows, n_past = imp.shape
    lane = lax.broadcasted_iota(jnp.int32, (rows, n_past), 1)
    score = jnp.where((lane == 0) | (lane == n_past - 1), jnp.inf, imp)

    def body(i, rank):
        ci = jnp.sum(jnp.where(lane == i, score, 0.0), axis=-1, keepdims=True)
        beats = (ci > score) | ((ci == score) & (i < lane))
        return rank + jnp.where(beats, 1, 0)
    rank = lax.fori_loop(0, n_past, body, jnp.zeros((rows, n_past), jnp.int32))
    out_lane = lax.broadcasted_iota(jnp.int32, (rows, 128), 1)
    picked = jnp.zeros((rows, 128), jnp.int32)
    for r in range(n_pick):
        ir = jnp.sum(jnp.where(rank == r, lane, 0), axis=-1, keepdims=True)
        picked = jnp.where(out_lane == r, ir, picked)
    idx_ref[...] = picked


def _topk_sample(imp2d, n_pick):
    rows, n_past = imp2d.shape
    return pl.pallas_call(
        functools.partial(_topk_sample_kernel, n_pick=n_pick),
        grid=(1,),
        in_specs=[pl.BlockSpec((rows, n_past), lambda i: (0, 0))],
        out_specs=pl.BlockSpec((rows, 128), lambda i: (0, 0)),
        out_shape=jax.ShapeDtypeStruct((rows, 128), jnp.int32),
        compiler_params=_cparams(("arbitrary",)), name="topk_blocks_sample",
    )(imp2d)


def _cmp_sample(q5, kc, vc, bias, gate_cols, nb, nt):
    n_cmp = kc.shape[2]
    g = NSA_KV_HEADS
    rows = nt * NSA_GROUP
    return pl.pallas_call(
        functools.partial(_cmp_sample_kernel, nt=nt, n_cmp=n_cmp),
        grid=(nb,),
        in_specs=[pl.BlockSpec((1, nt, g, NSA_GROUP, NSA_DIM), lambda b: (b, 0, 0, 0, 0)),
                  pl.BlockSpec((1, g, n_cmp, NSA_DIM), lambda b: (b, 0, 0, 0)),
                  pl.BlockSpec((1, g, n_cmp, NSA_DIM), lambda b: (b, 0, 0, 0)),
                  pl.BlockSpec((g, rows, n_cmp), lambda b: (0, 0, 0)),
                  pl.BlockSpec((1, g, 1, rows, 1), lambda b: (b, 0, 0, 0, 0))],
        out_specs=(pl.BlockSpec((1, nt, g, NSA_GROUP, NSA_DIM), lambda b: (b, 0, 0, 0, 0)),
                   pl.BlockSpec((1, g, nt, n_cmp // 2), lambda b: (b, 0, 0, 0))),
        out_shape=(jax.ShapeDtypeStruct((nb, nt, g, NSA_GROUP, NSA_DIM), F32),
                   jax.ShapeDtypeStruct((nb, g, nt, n_cmp // 2), F32)),
        compiler_params=_cparams(("arbitrary",)), name="cmp_attn_sample",
    )(q5, kc, vc, bias, gate_cols)


def _slc_sample_kernel(idx_ref, pt_ref, *refs, nt, n_pick, n_past, n_pages):
    del pt_ref, n_pages
    nblk = NSA_KV_HEADS * n_pick
    blk_refs = refs[:nblk]
    q_ref, new_ref, bias_ref, biasc_ref, gate_ref, o_ref = refs[nblk:]
    b = pl.program_id(0)
    t = pl.program_id(1)
    half = NSA_KV_HEADS * NSA_DIM
    lane_half = lax.broadcasted_iota(jnp.int32, (NSA_GROUP, PAGE_SIZE), 1) // SLC_BLOCK
    new = new_ref[0]
    groups = range(NSA_KV_HEADS)
    s_all, sn_all, vt_all = [], [], []
    for g in groups:
        qg = q_ref[0, 0, g]
        kt_list, vt_list, bias_list = [], [], []
        for n in range(n_pick):
            page = blk_refs[g * n_pick + n]
            j = idx_ref[((b * nt + t) * NSA_KV_HEADS + g) * n_pick + n]
            near = jnp.clip(j - (n_past - 3), 0, 2)
            kt_list.append(page[0, g * NSA_DIM:(g + 1) * NSA_DIM, :].astype(BF16))
            vt_list.append(page[0, half + g * NSA_DIM:half + (g + 1) * NSA_DIM, :].astype(BF16))
            bias_list.append(jnp.where(lane_half == (j & 1), bias_ref[0, g, near], NEG_BIG))
        s_all.append(_dot(qg, jnp.concatenate(kt_list, axis=1)) + jnp.concatenate(bias_list, axis=1))
        sn_all.append(_dot_nt(qg, new[:, g * NSA_DIM:(g + 1) * NSA_DIM].astype(BF16)) + biasc_ref[0, g])
        vt_all.append(jnp.concatenate(vt_list, axis=1))
    m_all = [jnp.maximum(s_all[g].max(axis=-1, keepdims=True), sn_all[g].max(axis=-1, keepdims=True)) for g in groups]
    p_all = [jnp.exp(s_all[g] - m_all[g]) for g in groups]
    pn_all = [jnp.exp(sn_all[g] - m_all[g]) for g in groups]
    for g in groups:
        l = jnp.sum(p_all[g], axis=-1, keepdims=True) + jnp.sum(pn_all[g], axis=-1, keepdims=True)
        acc = _dot_nt(p_all[g].astype(BF16), vt_all[g])
        acc = acc + _dot(pn_all[g].astype(BF16), new[:, half + g * NSA_DIM:half + (g + 1) * NSA_DIM].astype(BF16))
        o_ref[0, 0, g] = acc / l * gate_ref[0, g, 0, 0]


def _slc_sample(idx_flat, pt_flat, cache_blocks, q5, new_rows, bias_near, bias_cur, gate_cols, nb, nt, n_pick,
                n_past, n_pages):
    g = NSA_KV_HEADS

    def blk_map(gi, n):
        def f(b, t, idx, pt):
            j = idx[((b * nt + t) * g + gi) * n_pick + n]
            return (pt[b * n_pages + (j >> 1)], 0, 0)
        return f

    in_specs = [pl.BlockSpec((1, KV_ROW, PAGE_SIZE), blk_map(gi, n)) for gi in range(g) for n in range(n_pick)]
    in_specs += [
        pl.BlockSpec((1, 1, g, NSA_GROUP, NSA_DIM), lambda b, t, idx, pt: (b, t, 0, 0, 0)),
        pl.BlockSpec((1, 8, KV_ROW), lambda b, t, idx, pt: (b, 0, 0)),
        pl.BlockSpec((1, g, 3, NSA_GROUP, PAGE_SIZE), lambda b, t, idx, pt: (t, 0, 0, 0, 0)),
        pl.BlockSpec((1, g, NSA_GROUP, 8), lambda b, t, idx, pt: (t, 0, 0, 0)),
        pl.BlockSpec((1, g, 1, 1, NSA_GROUP, 1), lambda b, t, idx, pt: (b, 0, 0, t, 0, 0)),
    ]
    return pl.pallas_call(
        functools.partial(_slc_sample_kernel, nt=nt, n_pick=n_pick, n_past=n_past, n_pages=n_pages),
        grid_spec=pltpu.PrefetchScalarGridSpec(
            num_scalar_prefetch=2, grid=(nb, nt), in_specs=in_specs,
            out_specs=pl.BlockSpec((1, 1, g, NSA_GROUP, NSA_DIM), lambda b, t, idx, pt: (b, t, 0, 0, 0))),
        out_shape=jax.ShapeDtypeStruct((nb, nt, g, NSA_GROUP, NSA_DIM), F32),
        compiler_params=_cparams(("arbitrary", "arbitrary")), name="slc_attn_sample",
    )(idx_flat, pt_flat, *([cache_blocks] * (g * n_pick)), q5, new_rows, bias_near, bias_cur, gate_cols)


def _win_sample_kernel(q_ref, buf_ref, new_ref, bias_ref, biasn_ref, gate_ref, o_ref, *, nt):
    rows = nt * NSA_GROUP
    half = NSA_KV_HEADS * NSA_DIM
    buf = buf_ref[0, 0]
    new = new_ref[0]
    for g in range(NSA_KV_HEADS):
        qg = q_ref[0, :, g].reshape(rows, NSA_DIM)
        s1 = _dot(qg, buf[g * NSA_DIM:(g + 1) * NSA_DIM, :].astype(BF16)) + bias_ref[g]
        s2 = _dot_nt(qg, new[:, g * NSA_DIM:(g + 1) * NSA_DIM].astype(BF16)) + biasn_ref[g]
        m = jnp.maximum(s1.max(axis=-1, keepdims=True), s2.max(axis=-1, keepdims=True))
        p1 = jnp.exp(s1 - m)
        p2 = jnp.exp(s2 - m)
        l = jnp.sum(p1, axis=-1, keepdims=True) + jnp.sum(p2, axis=-1, keepdims=True)
        acc = _dot_nt(p1.astype(BF16), buf[half + g * NSA_DIM:half + (g + 1) * NSA_DIM, :].astype(BF16))
        acc = acc + _dot(p2.astype(BF16), new[:, half + g * NSA_DIM:half + (g + 1) * NSA_DIM].astype(BF16))
        o_ref[0, :, g] = (acc / l * gate_ref[0, g, 0]).reshape(nt, NSA_GROUP, NSA_DIM)


def _win_sample(q5, win_state, layer, new_rows, bias_buf, bias_new, gate_cols, nb, nt):
    g = NSA_KV_HEADS
    rows = nt * NSA_GROUP
    wlen = win_state.shape[3]
    return pl.pallas_call(
        functools.partial(_win_sample_kernel, nt=nt),
        grid=(nb,),
        in_specs=[pl.BlockSpec((1, nt, g, NSA_GROUP, NSA_DIM), lambda b: (b, 0, 0, 0, 0)),
                  pl.BlockSpec((1, 1, KV_ROW, wlen), lambda b: (layer, b, 0, 0)),
                  pl.BlockSpec((1, 8, KV_ROW), lambda b: (b, 0, 0)),
                  pl.BlockSpec((g, rows, wlen), lambda b: (0, 0, 0)),
                  pl.BlockSpec((g, rows, 8), lambda b: (0, 0, 0)),
                  pl.BlockSpec((1, g, 1, rows, 1), lambda b: (b, 0, 2, 0, 0))],
        out_specs=pl.BlockSpec((1, nt, g, NSA_GROUP, NSA_DIM), lambda b: (b, 0, 0, 0, 0)),
        out_shape=jax.ShapeDtypeStruct((nb, nt, g, NSA_GROUP, NSA_DIM), F32),
        compiler_params=_cparams(("arbitrary",)), name="win_attn_sample",
    )(q5, win_state, new_rows, bias_buf, bias_new, gate_cols)


def _mla_sample_kernel(pt_ref, *refs, pages, nt):
    del pt_ref
    page_refs = refs[:pages]
    q_ref, new_ref, o_ref, m_ref, l_ref, acc_ref = refs[pages:]
    step = pl.program_id(1)
    rows = MLA_HEADS * nt

    @pl.when(step == 0)
    def _():
        m_ref[...] = jnp.full(m_ref.shape, NEG_BIG, F32)
        l_ref[...] = jnp.zeros(l_ref.shape, F32)
        acc_ref[...] = jnp.zeros(acc_ref.shape, F32)

    q = q_ref[0]
    kt = jnp.concatenate([page_refs[k][0].astype(BF16) for k in range(pages)], axis=1)
    s = _dot(q, kt)
    m_old = m_ref[...]
    m_new = jnp.maximum(m_old, s.max(axis=-1, keepdims=True))
    alpha = jnp.exp(m_old - m_new)
    p = jnp.exp(s - m_new)
    m_ref[...] = m_new
    l_ref[...] = alpha * l_ref[...] + jnp.sum(p, axis=-1, keepdims=True)
    acc_ref[...] = alpha * acc_ref[...] + _dot_nt(p.astype(BF16), kt[0:KV_LORA, :])

    @pl.when(step == pl.num_programs(1) - 1)
    def _():
        new = new_ref[0].astype(BF16)
        tq = lax.broadcasted_iota(jnp.int32, (rows, 8), 0) % nt
        tk = lax.broadcasted_iota(jnp.int32, (rows, 8), 1)
        s = jnp.where(tk <= tq, _dot_nt(q, new), NEG_BIG)
        m_o = m_ref[...]
        m_n = jnp.maximum(m_o, s.max(axis=-1, keepdims=True))
        a = jnp.exp(m_o - m_n)
        p = jnp.exp(s - m_n)
        lf = a * l_ref[...] + jnp.sum(p, axis=-1, keepdims=True)
        accf = a * acc_ref[...] + _dot(p.astype(BF16), new[:, 0:KV_LORA])
        o_ref[0] = (accf / lf).astype(BF16)


def _mla_sample(pt_flat, cache_pages, q_rows, new_rows, nb, nt, n_pages):
    pages = MLA_PAGES
    steps = n_pages // pages
    rows = MLA_HEADS * nt

    def page_map(k):
        return lambda b, s, pt: (pt[b * n_pages + s * pages + k], 0, 0)

    in_specs = [pl.BlockSpec((1, LAT, PAGE_SIZE), page_map(k)) for k in range(pages)]
    in_specs += [pl.BlockSpec((1, rows, LAT), lambda b, s, pt: (b, 0, 0)),
                 pl.BlockSpec((1, 8, LAT), lambda b, s, pt: (b, 0, 0))]
    return pl.pallas_call(
        functools.partial(_mla_sample_kernel, pages=pages, nt=nt),
        grid_spec=pltpu.PrefetchScalarGridSpec(
            num_scalar_prefetch=1, grid=(nb, steps), in_specs=in_specs,
            out_specs=pl.BlockSpec((1, rows, KV_LORA), lambda b, s, pt: (b, 0, 0)),
            scratch_shapes=[pltpu.VMEM((rows, 1), F32), pltpu.VMEM((rows, 1), F32),
                            pltpu.VMEM((rows, KV_LORA), F32)]),
        out_shape=jax.ShapeDtypeStruct((nb, rows, KV_LORA), BF16),
        compiler_params=_cparams(("arbitrary", "arbitrary")), name="mla_attn_sample",
    )(pt_flat, *([cache_pages] * pages), q_rows, new_rows)


def _pack_layer(l, w):
    d = D_MODEL
    w_in = w["w_in"][l]
    sizes = (NSA_HEADS * NSA_DIM, KV_ROW, KV_ROW, KV_ROW, 3 * NSA_HEADS, Q_LORA, KV_LORA, MLA_ROPE, 2 * d)
    offs = np.concatenate([[0], np.cumsum(sizes)])
    seg = [w_in[:, offs[i]:offs[i + 1]] for i in range(len(sizes))]
    gsrc = seg[4].reshape(d, NSA_KV_HEADS, NSA_GROUP, 3).transpose(0, 1, 3, 2).reshape(d, NSA_KV_HEADS, 3 * NSA_GROUP)
    gates = jnp.pad(gsrc, ((0, 0), (0, 0), (0, GATE_LANES - 3 * NSA_GROUP))).reshape(d, NSA_KV_HEADS * GATE_LANES)
    kr = jnp.pad(seg[7], ((0, 0), (0, 128 - MLA_ROPE)))
    w_packed = jnp.concatenate([seg[0], seg[1], seg[2], seg[3], gates, seg[5], seg[6], kr, seg[8]], axis=1)
    wuq = w["mla_w_uq"][l].reshape(Q_LORA, MLA_HEADS, MLA_NOPE + MLA_ROPE)
    wuq = jnp.concatenate([wuq[:, :, :MLA_NOPE].reshape(Q_LORA, -1), wuq[:, :, MLA_NOPE:].reshape(Q_LORA, -1)], axis=1)
    w1 = w["nsa_cmp_w1"][l].reshape(2, CMP_BLOCK, NSA_DIM, NSA_DIM)
    eye_g = jnp.eye(NSA_KV_HEADS, dtype=F32)
    eye_i = jnp.eye(2, dtype=F32)
    w1big = jnp.einsum("irdo,ij,gh->rigdjho", w1, eye_i, eye_g).reshape(CMP_BLOCK, KV_ROW, KV_ROW)
    w2big = jnp.einsum("ido,ij,gh->igdjho", w["nsa_cmp_w2"][l], eye_i, eye_g).reshape(KV_ROW, KV_ROW)
    pe = w["nsa_cmp_pe"][l]
    pe_big = jnp.broadcast_to(pe.transpose(1, 0, 2)[:, :, None, :], (CMP_BLOCK, 2, NSA_KV_HEADS, NSA_DIM))
    b1big = jnp.broadcast_to(w["nsa_cmp_b1"][l][:, None, :], (2, NSA_KV_HEADS, NSA_DIM))
    return dict(
        norm_mix_g=w["norm_mix_g"][l].reshape(1, d),
        w_in=w_packed.astype(BF16),
        mla_q_norm_g=w["mla_q_norm_g"][l].reshape(1, Q_LORA),
        w_uq=wuq.astype(BF16),
        w_ukT=w["mla_w_uk"][l].transpose(1, 2, 0).astype(BF16),
        mla_kv_norm_g=w["mla_kv_norm_g"][l].reshape(1, KV_LORA),
        w_uv=w["mla_w_uv"][l].transpose(1, 0, 2).astype(BF16),
        w_out=w["w_out"][l].astype(BF16),
        norm_ffn_g=w["norm_ffn_g"][l].reshape(1, d),
        ffn_w13=w["ffn_w13"][l].astype(BF16),
        ffn_w2=w["ffn_w2"][l].astype(BF16),
        cmp_pe=pe_big.reshape(CMP_BLOCK, 1, KV_ROW),
        cmp_pe_t=jnp.tile(pe_big.reshape(CMP_BLOCK, KV_ROW).T, (1, 2 * PAGE_SIZE // CMP_BLOCK)),
        cmp_w1=w1big.astype(BF16),
        cmp_b1=b1big.reshape(1, KV_ROW),
        cmp_w2=w2big.astype(BF16),
    )


def _cmp_block_ends(n_cmp):
    order = np.concatenate([np.arange(0, n_cmp, 2), np.arange(1, n_cmp, 2)])
    return (order + 1) * CMP_BLOCK - 1


def _prompt_bias_tables(table, nt):
    tq = ATT_TILE
    i = np.arange(tq)[:, None]
    j = np.arange(tq)[None, :]
    d0, d1, d2 = i - j, tq + i - j, 2 * tq + i - j
    bucket = np.concatenate([_t5_bucket_np(d0), _t5_bucket_np(d1), _t5_bucket_np(d2)], axis=0)
    mask = np.concatenate([np.where(d0 >= 0, 0.0, NEG_BIG), np.zeros((tq, tq)),
                           np.where(d2 < WINDOW, 0.0, NEG_BIG)], axis=0).astype(np.float32)
    tiles = _expand_bias(table, bucket, mask, True).reshape(NSA_HEADS, 3, tq, tq)
    n_cmp = nt // CMP_BLOCK
    dist = np.arange(nt)[None, :] - _cmp_block_ends(n_cmp)[:, None]
    cmp_bias = _expand_bias(table, _t5_bucket_np(dist), np.where(dist >= 0, 0.0, NEG_BIG).astype(np.float32), False)
    expand = (np.arange(SEL_LANES)[None, :, None] ==
              (np.arange(nt // tq)[:, None, None] * tq + np.arange(tq)[None, None, :]) // SLC_BLOCK)
    return tiles, cmp_bias, jnp.asarray(expand.astype(np.float32), dtype=BF16)


def _sample_bias_tables(table, nt, past, wlen):
    g, hg = NSA_KV_HEADS, NSA_GROUP
    rows = nt * hg
    q_pos = past + np.arange(nt)

    def per_group(b, width):
        return b.reshape(g, hg, nt, width).transpose(0, 2, 1, 3).reshape(g, rows, width)

    n_cmp = past // CMP_BLOCK
    dist = q_pos[:, None] - _cmp_block_ends(n_cmp)[None, :]
    cmp_bias = per_group(_expand_bias(table, _t5_bucket_np(dist),
                                      np.where(dist >= 0, 0.0, NEG_BIG).astype(np.float32), False), n_cmp)
    dist = q_pos[:, None] - (past - wlen + np.arange(wlen))[None, :]
    ok = (dist >= 0) & (dist < WINDOW)
    win_bias = per_group(_expand_bias(table, _t5_bucket_np(dist), np.where(ok, 0.0, NEG_BIG).astype(np.float32),
                                      False), wlen)
    dist = q_pos[:, None] - (past + np.arange(8))[None, :]
    ok = (dist >= 0) & (dist < WINDOW) & (np.arange(8)[None, :] < nt)
    new_mask = np.where(ok, 0.0, NEG_BIG).astype(np.float32)
    win_new = per_group(_expand_bias(table, _t5_bucket_np(dist), new_mask, False), 8)
    n_past = past // SLC_BLOCK
    near = np.stack([np.full((nt, SLC_BLOCK), 10 * MAX_DISTANCE),
                     q_pos[:, None] - ((n_past - 2) * SLC_BLOCK + np.arange(SLC_BLOCK))[None, :],
                     q_pos[:, None] - ((n_past - 1) * SLC_BLOCK + np.arange(SLC_BLOCK))[None, :]], axis=1)
    near = near.reshape(nt, 3 * SLC_BLOCK)
    slc_near = _expand_bias(table, _t5_bucket_np(near), np.zeros(near.shape, np.float32), True)
    slc_near = slc_near.reshape(g, hg, nt, 3, SLC_BLOCK).transpose(2, 0, 3, 1, 4)
    slc_near = jnp.tile(slc_near, (1, 1, 1, 1, PAGE_SIZE // SLC_BLOCK))
    slc_cur = _expand_bias(table, _t5_bucket_np(dist), np.where((dist >= 0) & (np.arange(8)[None, :] < nt), 0.0,
                                                                 NEG_BIG).astype(np.float32), True)
    slc_cur = slc_cur.reshape(g, hg, nt, 8).transpose(2, 0, 1, 3)
    return cmp_bias, win_bias, win_new, slc_near, slc_cur


def kernel(x_prompt, x_sample, cache_cmp_kv, cache_slc_kv, cache_mla, state_win_kv, page_table, c_prompt, c_sample,
           rel_bias_table, ada_w, ada_b, norm_mix_g, w_in, nsa_cmp_pe, nsa_cmp_w1, nsa_cmp_b1, nsa_cmp_w2,
           mla_q_norm_g, mla_w_uq, mla_kv_norm_g, mla_w_uk, mla_w_uv, w_out, norm_ffn_g, ffn_w13, ffn_w2,
           final_norm_g):
    weights = dict(norm_mix_g=norm_mix_g, w_in=w_in, nsa_cmp_pe=nsa_cmp_pe, nsa_cmp_w1=nsa_cmp_w1,
                   nsa_cmp_b1=nsa_cmp_b1, nsa_cmp_w2=nsa_cmp_w2, mla_q_norm_g=mla_q_norm_g, mla_w_uq=mla_w_uq,
                   mla_kv_norm_g=mla_kv_norm_g, mla_w_uk=mla_w_uk, mla_w_uv=mla_w_uv, w_out=w_out,
                   norm_ffn_g=norm_ffn_g, ffn_w13=ffn_w13, ffn_w2=ffn_w2)
    depth = ada_w.shape[0]
    d = D_MODEL
    g = NSA_KV_HEADS
    pb, pt_len, _ = x_prompt.shape
    sb, st_len, _ = x_sample.shape
    n_pool = cache_cmp_kv.shape[1]
    n_pages = page_table.shape[1]
    past = n_pages * PAGE_SIZE
    wlen = state_win_kv.shape[2]
    n_past_blocks = past // SLC_BLOCK
    n_pick = N_SELECT - 1
    assert pt_len % ATT_TILE == 0 and pt_len >= WINDOW and st_len <= 8 and past >= wlen
    assert n_past_blocks > n_pick and n_pages % MLA_PAGES == 0

    layers = [_pack_layer(l, weights) for l in range(depth)]
    gfinal = final_norm_g.reshape(1, d)
    mod_all = _modulation(jnp.concatenate([c_prompt, c_sample], axis=0), ada_w, ada_b)

    tiles, cmp_bias_p, expand = _prompt_bias_tables(rel_bias_table, pt_len)
    cmp_bias_s, win_bias_s, win_new_s, slc_near_s, slc_cur_s = _sample_bias_tables(rel_bias_table, st_len, past, wlen)
    cos_p, sin_p = _rope_tables(jnp.arange(pt_len), MLA_HEADS)
    cos_s, sin_s = _rope_tables(jnp.tile(past + jnp.arange(st_len), sb), MLA_HEADS)
    prompt_pages = jnp.arange(pb * pt_len // PAGE_SIZE, dtype=jnp.int32)
    pt_flat = page_table.reshape(-1).astype(jnp.int32)

    x = x_prompt.reshape(pb * pt_len, d)
    st_p = [[], [], [], []]
    tm = 256
    for l in range(depth):
        lw = layers[l]
        mod = mod_all[l, :pb].reshape(pb, 6, d)
        (q, kvc, kvs, kvw, kst, kse, kso, kwt, kwe, kwo, gates, qmla, lat, latv, latt, gm) = _inproj(
            x, mod, False, lw, cos_p, sin_p, pb, pt_len, tm)
        tok = _compress(kvc.reshape(-1, PAGE_SIZE, KV_ROW), prompt_pages, lw)
        kc, vc = _split_compressed(tok, pb)
        o_cmp, sel = _cmp_prompt(q, kc, vc.transpose(0, 1, 3, 2), cmp_bias_p, gates, pb, pt_len)
        o_slc = _nsa_flash(q, kst, kse, kso, tiles, gates, pb, pt_len, sel=sel, expand=expand)
        o_win = _nsa_flash(q, kwt, kwe, kwo, tiles, gates, pb, pt_len)
        o_lat = _mla_prompt(qmla, latt, latv, pb, pt_len)
        x = _merge(o_cmp.reshape(-1, d), o_slc.reshape(-1, d), o_win.reshape(-1, d), o_lat, gm, x, mod, False, lw,
                   pb, pt_len, tm)
        x = _ffn(x, mod, False, lw, gfinal, l == depth - 1, pb, pt_len, 512)
        st_p[0].append(kvc.reshape(pb, pt_len, 2, g, NSA_DIM))
        st_p[1].append(kvs.reshape(pb, pt_len, 2, g, NSA_DIM))
        st_p[2].append(lat.reshape(pb, pt_len, LAT))
        st_p[3].append(kvw.reshape(pb, pt_len, 2, g, NSA_DIM)[:, pt_len - min(WINDOW, pt_len):])
    y_prompt = x.reshape(pb, pt_len, d)

    ms = sb * st_len
    x = x_sample.reshape(ms, d)
    st_s = [[], [], [], []]
    cmp_pages = cache_cmp_kv.transpose(0, 1, 3, 4, 5, 2).reshape(depth * n_pool, KV_ROW, PAGE_SIZE)
    slc_pages = cache_slc_kv.transpose(0, 1, 3, 4, 5, 2).reshape(depth * n_pool, KV_ROW, PAGE_SIZE)
    mla_pages = cache_mla.transpose(0, 1, 3, 2).reshape(depth * n_pool, LAT, PAGE_SIZE)
    win_state = state_win_kv.transpose(0, 1, 3, 4, 5, 2).reshape(depth, sb, KV_ROW, wlen)
    for l in range(depth):
        lw = layers[l]
        mod = jnp.repeat(mod_all[l, pb:], st_len, axis=0).reshape(ms, 6, d)
        (q, kvc, kvs, kvw, _, _, _, _, _, _, gates, qmla, lat, _, _, gm) = _inproj(
            x, mod, True, lw, cos_s, sin_s, 1, ms, ms)
        pt_l = pt_flat + l * n_pool
        tok = _compress_t(cmp_pages, pt_l, lw)
        kc, vc = _split_compressed(tok, sb)
        q5 = q.reshape(g, NSA_GROUP, sb, st_len, NSA_DIM).transpose(2, 3, 0, 1, 4)
        gate_cols = gates.reshape(sb, st_len, g, GATE_LANES)[..., :3 * NSA_GROUP]
        gate_cols = gate_cols.reshape(sb, st_len, g, 3, NSA_GROUP).transpose(0, 2, 3, 1, 4)
        gate_rows = gate_cols.reshape(sb, g, 3, st_len * NSA_GROUP, 1)
        o_cmp, imp = _cmp_sample(q5, kc, vc, cmp_bias_s, gate_rows, sb, st_len)
        idx = _topk_sample(imp.reshape(sb * g * st_len, n_past_blocks), n_pick)
        idx_flat = idx[:, :n_pick].reshape(sb, g, st_len, n_pick).transpose(0, 2, 1, 3).reshape(-1)
        pad8 = lambda a: jnp.pad(a.reshape(sb, st_len, -1), ((0, 0), (0, 8 - st_len), (0, 0)))
        o_slc = _slc_sample(idx_flat, pt_l, slc_pages, q5, pad8(kvs), slc_near_s, slc_cur_s,
                            gate_cols.reshape(sb, g, 3, st_len, NSA_GROUP, 1)[:, :, 1:2], sb, st_len, n_pick,
                            n_past_blocks, n_pages)
        o_win = _win_sample(q5, win_state, l, pad8(kvw), win_bias_s, win_new_s, gate_rows, sb, st_len)
        q_rows = qmla.reshape(MLA_HEADS, sb, st_len, LAT).transpose(1, 0, 2, 3).reshape(sb, MLA_HEADS * st_len, LAT)
        o_lat = _mla_sample(pt_l, mla_pages, q_rows, pad8(lat), sb, st_len, n_pages)
        o_lat = o_lat.reshape(sb, MLA_HEADS, st_len, KV_LORA).transpose(1, 0, 2, 3).reshape(1, MLA_HEADS, ms, KV_LORA)
        x = _merge(o_cmp.reshape(ms, d), o_slc.reshape(ms, d), o_win.reshape(ms, d), o_lat, gm, x, mod, True, lw,
                   1, ms, ms)
        x = _ffn(x, mod, True, lw, gfinal, l == depth - 1, 1, ms, ms)
        st_s[0].append(kvc.reshape(sb, st_len, 2, g, NSA_DIM))
        st_s[1].append(kvs.reshape(sb, st_len, 2, g, NSA_DIM))
        st_s[2].append(lat.reshape(sb, st_len, LAT))
        win_all = jnp.concatenate([state_win_kv[l], kvw.reshape(sb, st_len, 2, g, NSA_DIM)], axis=1)
        st_s[3].append(win_all[:, st_len:])
    y_sample = x.reshape(sb, st_len, d)

    return (y_prompt, y_sample, jnp.stack(st_p[0]), jnp.stack(st_p[1]), jnp.stack(st_p[2]), jnp.stack(st_p[3]),
            jnp.stack(st_s[0]), jnp.stack(st_s[1]), jnp.stack(st_s[2]), jnp.stack(st_s[3]))
```

```python
import functools
import math

import numpy as np
import jax
import jax.numpy as jnp
from jax import lax
from jax.experimental import pallas as pl
from jax.experimental.pallas import tpu as pltpu

F32 = jnp.float32
BF16 = jnp.bfloat16

D_MODEL = 1024
PAGE_SIZE = 128
NSA_HEADS = 16
NSA_KV_HEADS = 2
NSA_GROUP = NSA_HEADS // NSA_KV_HEADS
NSA_DIM = D_MODEL // NSA_HEADS
CMP_BLOCK = 32
SLC_BLOCK = 64
N_SELECT = 16
WINDOW = 512
MLA_HEADS = 8
MLA_NOPE = 128
MLA_ROPE = 64
MLA_V = D_MODEL // MLA_HEADS
Q_LORA = 384
KV_LORA = 256
LAT = KV_LORA + MLA_ROPE
ROPE_THETA = 10000.0
N_BUCKETS = 32
MAX_DISTANCE = 128
NSA_SCALE = NSA_DIM ** -0.5
MLA_SCALE = (MLA_NOPE + MLA_ROPE) ** -0.5
NEG_BIG = -1e30
RMS_EPS = 1e-6
KV_ROW = 2 * NSA_KV_HEADS * NSA_DIM
GATE_LANES = 128

SEG_Q = (0, 1024)
SEG_CMP = (1024, 1280)
SEG_SLC = (1280, 1536)
SEG_WIN = (1536, 1792)
SEG_GATE = (1792, 2048)
SEG_CQ = (2048, 2432)
SEG_CKV = (2432, 2688)
SEG_KR = (2688, 2816)
SEG_GM = (2816, 4864)
D_IN_PACKED = 4864

ATT_TILE = 256
FLASH_ROWS = 128
MLA_FLASH_ROWS = 512
SEL_LANES = 128
VMEM_LIMIT = 56 * 1024 * 1024
CMP_PAGES = 64
MLA_PAGES = 32


def _cparams(sem):
    return pltpu.CompilerParams(dimension_semantics=sem, vmem_limit_bytes=VMEM_LIMIT)


def _dot(a, b):
    return jnp.dot(a, b, preferred_element_type=F32)


def _dot_nt(a, b):
    return lax.dot_general(a, b, (((1,), (1,)), ((), ())), preferred_element_type=F32)


def _rms(x, g):
    return x * lax.rsqrt(jnp.mean(x * x, axis=-1, keepdims=True) + RMS_EPS) * g


def _rope_lanes(x, cos2, sin2):
    w = x.shape[-1]
    lane = lax.broadcasted_iota(jnp.int32, x.shape, 1)
    swapped = jnp.where(lane % MLA_ROPE < MLA_ROPE // 2, pltpu.roll(x, w - MLA_ROPE // 2, 1),
                        pltpu.roll(x, MLA_ROPE // 2, 1))
    return x * cos2 + swapped * sin2


def _t5_bucket_np(dist):
    max_exact = N_BUCKETS // 2
    d = np.maximum(dist, 0)
    log_ratio = np.log(np.maximum(d, 1).astype(np.float32) / max_exact) / math.log(MAX_DISTANCE / max_exact)
    large = np.minimum(max_exact + (log_ratio * (N_BUCKETS - max_exact)).astype(np.int32), N_BUCKETS - 1)
    return np.where(d < max_exact, d, large).astype(np.int32)


def _rope_tables(pos, reps):
    half = MLA_ROPE // 2
    inv = ROPE_THETA ** (-jnp.arange(half, dtype=F32) / half)
    ang = pos.astype(F32)[:, None] * inv[None, :]
    cos, sin = jnp.cos(ang), jnp.sin(ang)
    cos2 = jnp.concatenate([cos, cos], axis=-1)
    sin2 = jnp.concatenate([-sin, sin], axis=-1)
    return jnp.tile(cos2, (1, reps)), jnp.tile(sin2, (1, reps))


def _bias_kernel(table_ref, bucket_ref, mask_ref, o_ref, *, shift):
    h = pl.program_id(0)
    bucket = bucket_ref[...]
    acc = jnp.zeros(bucket.shape, F32)
    for b in range(N_BUCKETS):
        acc = jnp.where(bucket == b, table_ref[b, h], acc)
    if shift:
        acc = acc - table_ref[N_BUCKETS - 1, h]
    o_ref[0] = acc + mask_ref[...]


def _expand_bias(table, bucket, addmask, shift):
    r, c = bucket.shape
    return pl.pallas_call(
        functools.partial(_bias_kernel, shift=shift),
        grid=(NSA_HEADS,),
        in_specs=[pl.BlockSpec(memory_space=pltpu.SMEM),
                  pl.BlockSpec((r, c), lambda h: (0, 0)),
                  pl.BlockSpec((r, c), lambda h: (0, 0))],
        out_specs=pl.BlockSpec((1, r, c), lambda h: (h, 0, 0)),
        out_shape=jax.ShapeDtypeStruct((NSA_HEADS, r, c), F32),
        compiler_params=_cparams(("arbitrary",)),
        name="bias_expand",
    )(table, jnp.asarray(bucket), jnp.asarray(addmask))


def _mod_kernel(c_ref, w_ref, b_ref, o_ref):
    c = c_ref[...]
    cond = (c * jax.nn.sigmoid(c)).astype(BF16)
    o_ref[0] = _dot(cond, w_ref[0].astype(BF16)) + b_ref[0]


def _modulation(c_all, ada_w, ada_b):
    depth, d, n = ada_w.shape
    rows = c_all.shape[0]
    tn = 1536
    return pl.pallas_call(
        _mod_kernel,
        grid=(depth, n // tn),
        in_specs=[pl.BlockSpec((rows, d), lambda l, j: (0, 0)),
                  pl.BlockSpec((1, d, tn), lambda l, j: (l, 0, j)),
                  pl.BlockSpec((1, 1, tn), lambda l, j: (l, 0, j))],
        out_specs=pl.BlockSpec((1, rows, tn), lambda l, j: (l, 0, j)),
        out_shape=jax.ShapeDtypeStruct((depth, rows, n), F32),
        compiler_params=_cparams(("arbitrary", "arbitrary")),
        name="adaln_mod",
    )(c_all, ada_w, ada_b.reshape(depth, 1, n))


def _inproj_kernel(*refs, with_states):
    (x_ref, mod_ref, gn_ref, w_ref, cos_ref, sin_ref, gq_ref, wuq_ref, wuk_ref, gkv_ref) = refs[:10]
    outs = refs[13:] if with_states else refs[10:]
    (q_ref, kvc_ref, kvs_ref, kvw_ref, kst_ref, kse_ref, kso_ref, kwt_ref, kwe_ref, kwo_ref, gate_ref,
     qmla_ref, lat_ref, latb_ref, latt_ref, gm_ref) = outs[:16]
    x = x_ref[...]
    y = _rms(x, gn_ref[...])
    h = (y * (1.0 + mod_ref[:, 1, :]) + mod_ref[:, 0, :]).astype(BF16)

    def seg(s):
        return _dot(h, w_ref[:, s[0]:s[1]])

    q = (seg(SEG_Q) * NSA_SCALE).astype(BF16)
    for hd in range(NSA_HEADS):
        q_ref[0, hd] = q[:, hd * NSA_DIM:(hd + 1) * NSA_DIM]
    kvc = seg(SEG_CMP)
    kvc_ref[...] = kvc
    half = NSA_KV_HEADS * NSA_DIM
    low = lax.broadcasted_iota(jnp.int32, (x.shape[0], half), 1) < NSA_DIM
    kv_t = {}
    for s, kv_ref, kt_ref, ve_ref, vo_ref in ((SEG_SLC, kvs_ref, kst_ref, kse_ref, kso_ref),
                                              (SEG_WIN, kvw_ref, kwt_ref, kwe_ref, kwo_ref)):
        kv = seg(s)
        kv_ref[...] = kv
        kv_t[s] = kv.T
        kt = kv_t[s][0:half].astype(BF16)
        vv = kv[:, half:2 * half]
        vr = pltpu.roll(vv, NSA_DIM, 1)
        for g in range(NSA_KV_HEADS):
            kt_ref[0, g] = kt[g * NSA_DIM:(g + 1) * NSA_DIM]
            ve_ref[0, g] = jnp.where(low, vv if g == 0 else vr, 1.0).astype(BF16)
            vo_ref[0, g] = jnp.where(low, 1.0, vr if g == 0 else vv).astype(BF16)
    gate_ref[...] = jax.nn.sigmoid(seg(SEG_GATE))
    gm_ref[...] = jax.nn.sigmoid(seg(SEG_GM))

    cos = cos_ref[...]
    sin = sin_ref[...]
    cqn = _rms(seg(SEG_CQ), gq_ref[...]).astype(BF16)
    qm = _dot(cqn, wuq_ref[...])
    nope_w = MLA_HEADS * MLA_NOPE
    qr = _rope_lanes(qm[:, nope_w:], cos, sin) * MLA_SCALE
    for hd in range(MLA_HEADS):
        qn = qm[:, hd * MLA_NOPE:(hd + 1) * MLA_NOPE].astype(BF16)
        qmla_ref[0, hd, :, 0:KV_LORA] = (_dot(qn, wuk_ref[hd]) * MLA_SCALE).astype(BF16)
        qmla_ref[0, hd, :, KV_LORA:LAT] = qr[:, hd * MLA_ROPE:(hd + 1) * MLA_ROPE].astype(BF16)
    ckv = _rms(seg(SEG_CKV), gkv_ref[...])
    kr = _rope_lanes(seg(SEG_KR), cos[:, 0:128], sin[:, 0:128])
    lat_ref[:, 0:KV_LORA] = ckv
    lat_ref[:, KV_LORA:LAT] = kr[:, 0:MLA_ROPE]
    latb_ref[...] = ckv.astype(BF16)
    ckv_t = ckv.T
    kr_t = kr.T[0:MLA_ROPE]
    latt_ref[0, 0:KV_LORA, :] = ckv_t.astype(BF16)
    latt_ref[0, KV_LORA:LAT, :] = kr_t.astype(BF16)
    if with_states:
        stc_ref, sts_ref, stl_ref, stw_ref = outs[16:]
        stc_ref[0, 0] = kvc.T
        sts_ref[0, 0] = kv_t[SEG_SLC]
        stw_ref[0] = kv_t[SEG_WIN]
        stl_ref[0, 0, 0:KV_LORA, :] = ckv_t
        stl_ref[0, 0, KV_LORA:LAT, :] = kr_t


def _inproj(x, mod, per_row_mod, lw, cos_t, sin_t, nb, nt, tm, states=None, layer=0):
    m = nb * nt
    tpb = nt // tm
    d = D_MODEL
    if per_row_mod:
        mod_spec = pl.BlockSpec((tm, 6, d), lambda i: (i, 0, 0))
    else:
        mod_spec = pl.BlockSpec((1, 6, d), lambda i: (i // tpb, 0, 0))
    const2 = lambda i: (0, 0)
    row = lambda i: (i, 0)
    bt = lambda i: (i // tpb, 0, i % tpb, 0)
    btt = lambda i: (i // tpb, 0, 0, i % tpb)
    g = NSA_KV_HEADS
    out_shape = (
        jax.ShapeDtypeStruct((nb, NSA_HEADS, nt, NSA_DIM), BF16),
        jax.ShapeDtypeStruct((m, KV_ROW), F32),
        jax.ShapeDtypeStruct((m, KV_ROW), F32),
        jax.ShapeDtypeStruct((m, KV_ROW), F32),
        jax.ShapeDtypeStruct((nb, g, NSA_DIM, nt), BF16),
        jax.ShapeDtypeStruct((nb, g, nt, 2 * NSA_DIM), BF16),
        jax.ShapeDtypeStruct((nb, g, nt, 2 * NSA_DIM), BF16),
        jax.ShapeDtypeStruct((nb, g, NSA_DIM, nt), BF16),
        jax.ShapeDtypeStruct((nb, g, nt, 2 * NSA_DIM), BF16),
        jax.ShapeDtypeStruct((nb, g, nt, 2 * NSA_DIM), BF16),
        jax.ShapeDtypeStruct((m, g * GATE_LANES), F32),
        jax.ShapeDtypeStruct((nb, MLA_HEADS, nt, LAT), BF16),
        jax.ShapeDtypeStruct((m, LAT), F32),
        jax.ShapeDtypeStruct((m, KV_LORA), BF16),
        jax.ShapeDtypeStruct((nb, LAT, nt), BF16),
        jax.ShapeDtypeStruct((m, 2 * d), F32),
    )
    kt_spec = pl.BlockSpec((1, g, NSA_DIM, tm), btt)
    v_spec = pl.BlockSpec((1, g, tm, 2 * NSA_DIM), bt)
    out_specs = (
        pl.BlockSpec((1, NSA_HEADS, tm, NSA_DIM), bt),
        pl.BlockSpec((tm, KV_ROW), row), pl.BlockSpec((tm, KV_ROW), row), pl.BlockSpec((tm, KV_ROW), row),
        kt_spec, v_spec, v_spec, kt_spec, v_spec, v_spec,
        pl.BlockSpec((tm, g * GATE_LANES), row),
        pl.BlockSpec((1, MLA_HEADS, tm, LAT), bt),
        pl.BlockSpec((tm, LAT), row), pl.BlockSpec((tm, KV_LORA), row),
        pl.BlockSpec((1, LAT, tm), lambda i: (i // tpb, 0, i % tpb)),
        pl.BlockSpec((tm, 2 * d), row),
    )
    in_specs = [
        pl.BlockSpec((tm, d), row), mod_spec, pl.BlockSpec((1, d), const2),
        pl.BlockSpec((d, D_IN_PACKED), const2),
        pl.BlockSpec((tm, MLA_HEADS * MLA_ROPE), lambda i: (i % tpb, 0)),
        pl.BlockSpec((tm, MLA_HEADS * MLA_ROPE), lambda i: (i % tpb, 0)),
        pl.BlockSpec((1, Q_LORA), const2),
        pl.BlockSpec((Q_LORA, MLA_HEADS * (MLA_NOPE + MLA_ROPE)), const2),
        pl.BlockSpec((MLA_HEADS, MLA_NOPE, KV_LORA), lambda i: (0, 0, 0)),
        pl.BlockSpec((1, KV_LORA), const2),
    ]
    args = [x, mod, lw["norm_mix_g"], lw["w_in"], cos_t, sin_t, lw["mla_q_norm_g"], lw["w_uq"], lw["w_ukT"],
            lw["mla_kv_norm_g"]]
    aliases = {}
    if states is not None:
        n_in, n_out = len(args), len(out_shape)
        st_block = lambda width: pl.BlockSpec((1, 1, width, tm), lambda i: (layer, i // tpb, 0, i % tpb))
        in_specs += [pl.BlockSpec(memory_space=pl.ANY)] * 3
        args += list(states)
        out_shape += tuple(jax.ShapeDtypeStruct(s.shape, s.dtype) for s in states)
        out_shape += (jax.ShapeDtypeStruct((nb, KV_ROW, nt), F32),)
        out_specs += (st_block(KV_ROW), st_block(KV_ROW), st_block(LAT),
                      pl.BlockSpec((1, KV_ROW, tm), lambda i: (i // tpb, 0, i % tpb)))
        aliases = {n_in + k: n_out + k for k in range(3)}
    return pl.pallas_call(
        functools.partial(_inproj_kernel, with_states=states is not None),
        grid=(m // tm,), in_specs=in_specs, out_specs=out_specs, out_shape=out_shape,
        input_output_aliases=aliases,
        compiler_params=_cparams(("arbitrary",)), name="in_proj",
    )(*args)


def _gelu_tanh(z):
    return 0.5 * z * (1.0 + jnp.tanh(math.sqrt(2.0 / math.pi) * (z + 0.044715 * (z * z * z))))


def _compress_t_kernel(tab_ref, *refs, pages):
    del tab_ref
    page_refs = refs[:pages]
    pet_ref, sel_ref, w1_ref, b1_ref, w2_ref, o_ref, slab_ref = refs[pages:]
    pairs = pages // 2
    rows_per_pair = 2 * (PAGE_SIZE // CMP_BLOCK)
    sel = sel_ref[...]
    pet = pet_ref[...]
    for pr in range(pairs):
        xt2 = jnp.concatenate([page_refs[2 * pr][0], page_refs[2 * pr + 1][0]], axis=1)
        slab_ref[pr] = _dot_nt(sel, (xt2 + pet).astype(BF16))
    acc = jnp.zeros((pairs * rows_per_pair, KV_ROW), F32)
    for r in range(CMP_BLOCK):
        xr = jnp.concatenate([slab_ref[pr, r * rows_per_pair:(r + 1) * rows_per_pair, :] for pr in range(pairs)],
                             axis=0)
        acc = acc + _dot(xr.astype(BF16), w1_ref[r])
    hmid = _gelu_tanh(acc + b1_ref[...])
    o_ref[0] = _dot(hmid.astype(BF16), w2_ref[...])


def _compress_t(pages3d, page_ids, lw):
    n_logical = page_ids.shape[0]
    pages = min(CMP_PAGES, n_logical)
    assert n_logical % pages == 0 and pages % 2 == 0
    steps = n_logical // pages
    per_page = PAGE_SIZE // CMP_BLOCK

    def page_map(k):
        return lambda s, tab: (tab[s * pages + k], 0, 0)

    in_specs = [pl.BlockSpec((1, KV_ROW, PAGE_SIZE), page_map(k)) for k in range(pages)]
    in_specs += [
        pl.BlockSpec((KV_ROW, 2 * PAGE_SIZE), lambda s, tab: (0, 0)),
        pl.BlockSpec((2 * PAGE_SIZE, 2 * PAGE_SIZE), lambda s, tab: (0, 0)),
        pl.BlockSpec((CMP_BLOCK, KV_ROW, KV_ROW), lambda s, tab: (0, 0, 0)),
        pl.BlockSpec((1, KV_ROW), lambda s, tab: (0, 0)),
        pl.BlockSpec((KV_ROW, KV_ROW), lambda s, tab: (0, 0)),
    ]
    r, pg, c = np.meshgrid(np.arange(CMP_BLOCK), np.arange(2), np.arange(per_page), indexing="ij")
    src = (pg * PAGE_SIZE + c * CMP_BLOCK + r).reshape(-1)
    sel = jnp.asarray((src[:, None] == np.arange(2 * PAGE_SIZE)[None, :]).astype(np.float32), dtype=BF16)
    out = pl.pallas_call(
        functools.partial(_compress_t_kernel, pages=pages),
        grid_spec=pltpu.PrefetchScalarGridSpec(
            num_scalar_prefetch=1, grid=(steps,), in_specs=in_specs,
            out_specs=pl.BlockSpec((1, per_page * pages, KV_ROW), lambda s, tab: (s, 0, 0)),
            scratch_shapes=[pltpu.VMEM((pages // 2, 2 * PAGE_SIZE, KV_ROW), F32)]),
        out_shape=jax.ShapeDtypeStruct((steps, per_page * pages, KV_ROW), F32),
        compiler_params=_cparams(("arbitrary",)), name="cmp_compress_paged",
    )(page_ids, *([pages3d] * pages), lw["cmp_pe_t"], sel, lw["cmp_w1"], lw["cmp_b1"], lw["cmp_w2"])
    return out.reshape(n_logical * per_page, KV_ROW)


def _compress_kernel(tab_ref, *refs, pages):
    del tab_ref
    page_refs = refs[:pages]
    pe_ref, w1_ref, b1_ref, w2_ref, o_ref, slabk_ref, slabv_ref = refs[pages:]
    half = KV_ROW // 2
    for k in range(pages):
        slabk_ref[k * PAGE_SIZE:(k + 1) * PAGE_SIZE, :] = page_refs[k][0, :, 0:half]
        slabv_ref[k * PAGE_SIZE:(k + 1) * PAGE_SIZE, :] = page_refs[k][0, :, half:KV_ROW]
    per_page = PAGE_SIZE // CMP_BLOCK
    acc = jnp.zeros((per_page * pages, KV_ROW), F32)
    for r in range(CMP_BLOCK):
        rows = [jnp.concatenate([slab[pl.ds(c * CMP_BLOCK + r, pages, stride=PAGE_SIZE), :]
                                 for slab in (slabk_ref, slabv_ref)], axis=1) for c in range(per_page)]
        xr = (jnp.concatenate(rows, axis=0) + pe_ref[r]).astype(BF16)
        acc = acc + _dot(xr, w1_ref[r])
    hmid = _gelu_tanh(acc + b1_ref[...])
    o_ref[0] = _dot(hmid.astype(BF16), w2_ref[...])


def _compress(rows3d, page_ids, lw):
    n_logical = page_ids.shape[0]
    pages = min(CMP_PAGES, n_logical)
    assert n_logical % pages == 0
    steps = n_logical // pages
    per_page = PAGE_SIZE // CMP_BLOCK

    def page_map(k):
        return lambda s, tab: (tab[s * pages + k], 0, 0)

    in_specs = [pl.BlockSpec((1, PAGE_SIZE, KV_ROW), page_map(k)) for k in range(pages)]
    in_specs += [
        pl.BlockSpec((CMP_BLOCK, 1, KV_ROW), lambda s, tab: (0, 0, 0)),
        pl.BlockSpec((CMP_BLOCK, KV_ROW, KV_ROW), lambda s, tab: (0, 0, 0)),
        pl.BlockSpec((1, KV_ROW), lambda s, tab: (0, 0)),
        pl.BlockSpec((KV_ROW, KV_ROW), lambda s, tab: (0, 0)),
    ]
    out = pl.pallas_call(
        functools.partial(_compress_kernel, pages=pages),
        grid_spec=pltpu.PrefetchScalarGridSpec(
            num_scalar_prefetch=1, grid=(steps,), in_specs=in_specs,
            out_specs=pl.BlockSpec((1, per_page * pages, KV_ROW), lambda s, tab: (s, 0, 0)),
            scratch_shapes=[pltpu.VMEM((pages * PAGE_SIZE, KV_ROW // 2), F32),
                            pltpu.VMEM((pages * PAGE_SIZE, KV_ROW // 2), F32)]),
        out_shape=jax.ShapeDtypeStruct((steps, per_page * pages, KV_ROW), F32),
        compiler_params=_cparams(("arbitrary",)), name="cmp_compress",
    )(page_ids, *([rows3d] * pages), lw["cmp_pe"], lw["cmp_w1"], lw["cmp_b1"], lw["cmp_w2"])
    out = out.reshape(steps, per_page, pages, KV_ROW).transpose(0, 2, 1, 3)
    return out.reshape(n_logical * per_page, KV_ROW)


def _split_compressed(tok, nb):
    n = tok.shape[0] // nb
    t = tok.reshape(nb, n // 2, 2, 2, NSA_KV_HEADS, NSA_DIM)
    t = t.transpose(3, 0, 4, 2, 1, 5).reshape(2, nb, NSA_KV_HEADS, n, NSA_DIM).astype(BF16)
    return t[0], t[1]


def _cmp_prompt_kernel(q_ref, kc_ref, vct_ref, bias_ref, gate_ref, o_ref, sel_ref, *, tq, n_cmp):
    qi = pl.program_id(2)
    kc = kc_ref[0, 0]
    vct = vct_ref[0, 0]
    maskf = (bias_ref[0] > 0.5 * NEG_BIG).astype(F32)
    gates_t = gate_ref[0].T
    zs = [_dot_nt(kc, q_ref[0, hh]) + bias_ref[hh] for hh in range(NSA_GROUP)]
    es = [jnp.exp(z - jnp.max(z, axis=0, keepdims=True)) * maskf for z in zs]
    ps = [e / jnp.maximum(jnp.sum(e, axis=0, keepdims=True), 1e-30) for e in es]
    outs = [_dot(vct, ps[hh].astype(BF16)) * gates_t[hh:hh + 1, :] for hh in range(NSA_GROUP)]
    imp = ps[0]
    for p in ps[1:]:
        imp = imp + p
    o_ref[0] = jnp.concatenate(outs, axis=0).T
    n_slc = n_cmp // 2
    imp_slc = imp[0:n_slc] + imp[n_slc:n_cmp]
    t = qi * tq + lax.broadcasted_iota(jnp.int32, (n_slc, tq), 1)
    j = lax.broadcasted_iota(jnp.int32, (n_slc, tq), 0)
    cur = t // SLC_BLOCK
    forced = (j == 0) | (j == cur) | (j == cur - 1)
    score = jnp.where(forced, jnp.inf, jnp.where(j <= cur, imp_slc, -jnp.inf))
    rank = jnp.zeros((n_slc, tq), F32)
    for i in range(n_slc):
        ci = score[i:i + 1, :]
        rank = rank + jnp.where(j > i, jnp.where(ci >= score, 1.0, 0.0), jnp.where(ci > score, 1.0, 0.0))
    sel_t = jnp.where(rank < float(min(N_SELECT, n_slc)), 1.0, 0.0)
    sel_pad = jnp.concatenate([sel_t, jnp.zeros((SEL_LANES - n_slc, tq), F32)], axis=0)
    sel_ref[0, 0] = sel_pad.T


def _cmp_prompt(q, kc, vct, bias_t, gates, nb, nt):
    tq = ATT_TILE
    n_cmp = kc.shape[2]
    g = NSA_KV_HEADS
    gw = NSA_GROUP * NSA_DIM
    return pl.pallas_call(
        functools.partial(_cmp_prompt_kernel, tq=tq, n_cmp=n_cmp),
        grid=(nb, g, nt // tq),
        in_specs=[pl.BlockSpec((1, NSA_GROUP, tq, NSA_DIM), lambda b, gi, i: (b, gi, i, 0)),
                  pl.BlockSpec((1, 1, n_cmp, NSA_DIM), lambda b, gi, i: (b, gi, 0, 0)),
                  pl.BlockSpec((1, 1, NSA_DIM, n_cmp), lambda b, gi, i: (b, gi, 0, 0)),
                  pl.BlockSpec((NSA_GROUP, n_cmp, tq), lambda b, gi, i: (gi, 0, i)),
                  pl.BlockSpec((1, tq, GATE_LANES), lambda b, gi, i: (b, i, gi))],
        out_specs=(pl.BlockSpec((1, tq, gw), lambda b, gi, i: (b, i, gi)),
                   pl.BlockSpec((1, 1, tq, SEL_LANES), lambda b, gi, i: (b, gi, i, 0))),
        out_shape=(jax.ShapeDtypeStruct((nb, nt, D_MODEL), F32),
                   jax.ShapeDtypeStruct((nb, g, nt, SEL_LANES), F32)),
        compiler_params=_cparams(("arbitrary", "arbitrary", "arbitrary")), name="cmp_attn_select",
    )(q, kc, vct, bias_t, gates.reshape(nb, nt, g * GATE_LANES))


def _flash_update(s, vs, m_ref, l_ref, acc_ref):
    tk = s.shape[-1]
    m_old = m_ref[...]
    m_new = jnp.maximum(m_old, jnp.max(s, axis=-1, keepdims=True))
    alpha = jnp.exp(m_old - m_new)
    p = jnp.exp(s - jnp.concatenate([m_new] * (tk // 128), axis=1))
    psum = p[:, 0:128]
    for c in range(1, tk // 128):
        psum = psum + p[:, c * 128:(c + 1) * 128]
    l_ref[...] = alpha * l_ref[...] + psum
    dv = acc_ref.shape[-1]
    acc_ref[...] = jnp.concatenate([alpha] * (dv // 128), axis=1) * acc_ref[...] + _dot(p.astype(BF16), vs)
    m_ref[...] = m_new


def _nsa_flash_kernel(*refs, tq, slc):
    if slc:
        q_ref, k_ref, ve_ref, vo_ref, bias_ref, gate_ref, ge_ref, sel_ref, exp_ref, o_ref, m_ref, acc_ref = refs
    else:
        q_ref, k_ref, ve_ref, vo_ref, bias_ref, gate_ref, ge_ref, o_ref, m_ref, acc_ref = refs
    qi = pl.program_id(2)
    ch = FLASH_ROWS
    nsub = tq // ch
    m_ref[...] = jnp.full(m_ref.shape, NEG_BIG, F32)
    acc_ref[...] = jnp.zeros(acc_ref.shape, F32)
    if slc:
        sel = sel_ref[0, 0].astype(BF16)

    def tile(kt, btype):
        start = pl.multiple_of(kt * tq, tq)
        kt_tile = k_ref[0, 0, :, pl.ds(start, tq)]
        vs = (ve_ref[0, 0, pl.ds(start, tq), :], vo_ref[0, 0, pl.ds(start, tq), :])
        if slc:
            maskadd = (_dot(sel, exp_ref[kt]) - 1.0) * (-NEG_BIG)
        for c in range(NSA_GROUP * nsub):
            hh, qs = divmod(c, nsub)
            qrows = slice(qs * ch, (qs + 1) * ch)
            srows = slice(c * ch, (c + 1) * ch)
            s = _dot(q_ref[0, hh, qrows, :], kt_tile)
            if btype is not None:
                s = s + bias_ref[hh, btype, qrows, :]
            if slc:
                s = s + maskadd[qrows]
            m_old = m_ref[srows]
            m_new = jnp.maximum(m_old, jnp.max(s, axis=-1, keepdims=True))
            alpha = jnp.exp(m_old - m_new)
            p = jnp.exp(s - jnp.concatenate([m_new] * (tq // 128), axis=1))
            acc_ref[srows] = alpha * acc_ref[srows] + _dot(p.astype(BF16), vs[hh % 2])
            m_ref[srows] = m_new

    if slc:
        def far(kt, carry):
            tile(kt, None)
            return carry
        lax.fori_loop(0, jnp.maximum(qi - 1, 0), far, 0)
    else:
        @pl.when(qi >= 2)
        def _():
            tile(qi - 2, 2)

    @pl.when(qi >= 1)
    def _():
        tile(qi - 1, 1)

    tile(qi, 0)
    gexp = _expand_gates(gate_ref[0], ge_ref[...])
    lane = lax.broadcasted_iota(jnp.int32, (tq, 2 * NSA_DIM), 1)
    for j in range(NSA_GROUP // 2):
        a_e = acc_ref[(2 * j) * tq:(2 * j + 1) * tq]
        a_o = acc_ref[(2 * j + 1) * tq:(2 * j + 2) * tq]
        num = jnp.where(lane < NSA_DIM, a_e, a_o)
        den = pltpu.roll(jnp.where(lane < NSA_DIM, a_o, a_e), NSA_DIM, 1)
        cols = slice(j * 2 * NSA_DIM, (j + 1) * 2 * NSA_DIM)
        o_ref[0, :, cols] = num / den * gexp[:, cols]


def _expand_gates(g, expand):
    g1 = g.astype(BF16)
    r1 = g - g1.astype(F32)
    g2 = r1.astype(BF16)
    g3 = (r1 - g2.astype(F32)).astype(BF16)
    return _dot(g1, expand) + _dot(g2, expand) + _dot(g3, expand)


def _gate_expand_matrix(branch):
    k = np.arange(GATE_LANES)[:, None]
    n = np.arange(NSA_GROUP * NSA_DIM)[None, :]
    return jnp.asarray((k == branch * NSA_GROUP + n // NSA_DIM).astype(np.float32), dtype=BF16)


def _nsa_flash(q, kt, ve, vo, bias_tiles, gates, nb, nt, sel=None, expand=None):
    tq = ATT_TILE
    g = NSA_KV_HEADS
    gw = NSA_GROUP * NSA_DIM
    slc = sel is not None
    v_spec = pl.BlockSpec((1, 1, nt, 2 * NSA_DIM), lambda b, gi, i: (b, gi, 0, 0))
    in_specs = [pl.BlockSpec((1, NSA_GROUP, tq, NSA_DIM), lambda b, gi, i: (b, gi, i, 0)),
                pl.BlockSpec((1, 1, NSA_DIM, nt), lambda b, gi, i: (b, gi, 0, 0)),
                v_spec, v_spec,
                pl.BlockSpec((NSA_GROUP, 3, tq, tq), lambda b, gi, i: (gi, 0, 0, 0)),
                pl.BlockSpec((1, tq, GATE_LANES), lambda b, gi, i: (b, i, gi)),
                pl.BlockSpec((GATE_LANES, gw), lambda b, gi, i: (0, 0))]
    args = [q, kt, ve, vo, bias_tiles, gates.reshape(nb, nt, g * GATE_LANES), _gate_expand_matrix(1 if slc else 2)]
    if slc:
        in_specs += [pl.BlockSpec((1, 1, tq, SEL_LANES), lambda b, gi, i: (b, gi, i, 0)),
                     pl.BlockSpec((nt // tq, SEL_LANES, tq), lambda b, gi, i: (0, 0, 0))]
        args += [sel, expand]
    return pl.pallas_call(
        functools.partial(_nsa_flash_kernel, tq=tq, slc=slc),
        grid=(nb, g, nt // tq),
        in_specs=in_specs,
        out_specs=pl.BlockSpec((1, tq, gw), lambda b, gi, i: (b, i, gi)),
        out_shape=jax.ShapeDtypeStruct((nb, nt, D_MODEL), F32),
        scratch_shapes=[pltpu.VMEM((NSA_GROUP * tq, 128), F32), pltpu.VMEM((NSA_GROUP * tq, 2 * NSA_DIM), F32)],
        compiler_params=_cparams(("arbitrary", "arbitrary", "arbitrary")),
        name="slc_attn" if slc else "win_attn",
    )(*args)


def _mla_prompt_kernel(q_ref, latt_ref, latv_ref, o_ref, m_ref, l_ref, acc_ref, *, tq):
    qi = pl.program_id(1)
    rows = MLA_HEADS * tq
    m_ref[...] = jnp.full(m_ref.shape, NEG_BIG, F32)
    l_ref[...] = jnp.zeros(l_ref.shape, F32)
    acc_ref[...] = jnp.zeros(acc_ref.shape, F32)

    heads_per_chunk = MLA_FLASH_ROWS // tq
    n_chunks = MLA_HEADS // heads_per_chunk

    def tile(kt, diag):
        start = pl.multiple_of(kt * tq, tq)
        kt_tile = latt_ref[0, :, pl.ds(start, tq)]
        vs = latv_ref[0, pl.ds(start, tq), :]

        def scores(c):
            qc = q_ref[0, c * heads_per_chunk:(c + 1) * heads_per_chunk].reshape(MLA_FLASH_ROWS, LAT)
            s = _dot(qc, kt_tile)
            if diag:
                row = lax.broadcasted_iota(jnp.int32, (tq, tq), 0)
                col = lax.broadcasted_iota(jnp.int32, (tq, tq), 1)
                s = jnp.where((col <= row)[None], s.reshape(heads_per_chunk, tq, tq), NEG_BIG)
                s = s.reshape(MLA_FLASH_ROWS, tq)
            return s

        s_next = scores(0)
        for c in range(n_chunks):
            s_cur = s_next
            if c + 1 < n_chunks:
                s_next = scores(c + 1)
            srows = slice(c * MLA_FLASH_ROWS, (c + 1) * MLA_FLASH_ROWS)
            _flash_update(s_cur, vs, m_ref.at[srows], l_ref.at[srows], acc_ref.at[srows])

    def body(kt, carry):
        tile(kt, False)
        return carry
    lax.fori_loop(0, qi, body, 0)
    tile(qi, True)
    o = acc_ref[...] / jnp.sum(l_ref[...], axis=-1, keepdims=True)
    o_ref[0] = o.reshape(MLA_HEADS, tq, KV_LORA).astype(BF16)


def _mla_prompt(qmla, latt, latv, nb, nt):
    tq = ATT_TILE
    return pl.pallas_call(
        functools.partial(_mla_prompt_kernel, tq=tq),
        grid=(nb, nt // tq),
        in_specs=[pl.BlockSpec((1, MLA_HEADS, tq, LAT), lambda b, i: (b, 0, i, 0)),
                  pl.BlockSpec((1, LAT, nt), lambda b, i: (b, 0, 0)),
                  pl.BlockSpec((1, nt, KV_LORA), lambda b, i: (b, 0, 0))],
        out_specs=pl.BlockSpec((1, MLA_HEADS, tq, KV_LORA), lambda b, i: (b, 0, i, 0)),
        out_shape=jax.ShapeDtypeStruct((nb, MLA_HEADS, nt, KV_LORA), BF16),
        scratch_shapes=[pltpu.VMEM((MLA_HEADS * tq, 128), F32), pltpu.VMEM((MLA_HEADS * tq, 128), F32),
                        pltpu.VMEM((MLA_HEADS * tq, KV_LORA), F32)],
        compiler_params=_cparams(("arbitrary", "arbitrary")), name="mla_attn",
    )(qmla, latt, latv.reshape(nb, nt, KV_LORA))


def _merge_kernel(oc_ref, os_ref, ow_ref, ol_ref, gm_ref, x_ref, mod_ref, wuv_ref, wo_ref, o_ref):
    o_nsa = oc_ref[...] + os_ref[...] + ow_ref[...]
    o_mla = jnp.concatenate([_dot(ol_ref[0, hd], wuv_ref[hd]) for hd in range(MLA_HEADS)], axis=-1)
    gm = gm_ref[...]
    merged = (gm[:, 0:D_MODEL] * o_nsa + gm[:, D_MODEL:] * o_mla).astype(BF16)
    o_ref[...] = x_ref[...] + mod_ref[:, 2, :] * _dot(merged, wo_ref[...])


def _merge(o_cmp, o_slc, o_win, o_lat, gm, x, mod, per_row_mod, lw, nb, nt, tm):
    m = nb * nt
    tpb = nt // tm
    d = D_MODEL
    row = lambda i: (i, 0)
    if per_row_mod:
        mod_spec = pl.BlockSpec((tm, 6, d), lambda i: (i, 0, 0))
    else:
        mod_spec = pl.BlockSpec((1, 6, d), lambda i: (i // tpb, 0, 0))
    return pl.pallas_call(
        _merge_kernel, grid=(m // tm,),
        in_specs=[pl.BlockSpec((tm, d), row), pl.BlockSpec((tm, d), row), pl.BlockSpec((tm, d), row),
                  pl.BlockSpec((1, MLA_HEADS, tm, KV_LORA), lambda i: (i // tpb, 0, i % tpb, 0)),
                  pl.BlockSpec((tm, 2 * d), row), pl.BlockSpec((tm, d), row), mod_spec,
                  pl.BlockSpec((MLA_HEADS, KV_LORA, MLA_V), lambda i: (0, 0, 0)),
                  pl.BlockSpec((d, d), lambda i: (0, 0))],
        out_specs=pl.BlockSpec((tm, d), row),
        out_shape=jax.ShapeDtypeStruct((m, d), F32),
        compiler_params=_cparams(("arbitrary",)), name="merge_out_proj",
    )(o_cmp, o_slc, o_win, o_lat, gm, x, mod, lw["w_uv"], lw["w_out"])


def _ffn_kernel(x_ref, mod_ref, gn_ref, w1_ref, w3_ref, w2_ref, gf_ref, o_ref, h_ref, acc_ref, *, final):
    f = pl.program_id(1)

    @pl.when(f == 0)
    def _():
        y = _rms(x_ref[...], gn_ref[...])
        h_ref[...] = (y * (1.0 + mod_ref[:, 4, :]) + mod_ref[:, 3, :]).astype(BF16)
        acc_ref[...] = jnp.zeros(acc_ref.shape, F32)

    h = h_ref[...]
    a = _dot(h, w1_ref[...])
    b = _dot(h, w3_ref[...])
    act = (a * jax.nn.sigmoid(a) * b).astype(BF16)
    acc_ref[...] += _dot(act, w2_ref[...])

    @pl.when(f == pl.num_programs(1) - 1)
    def _():
        y = x_ref[...] + mod_ref[:, 5, :] * acc_ref[...]
        if final:
            y = _rms(y, gf_ref[...])
        o_ref[...] = y


def _ffn(x, mod, per_row_mod, lw, gfinal, final, nb, nt, tm):
    m = nb * nt
    tpb = nt // tm
    d = D_MODEL
    dff = lw["ffn_w2"].shape[0]
    tf = dff // 2
    nf = dff // tf
    row = lambda i, f: (i, 0)
    if per_row_mod:
        mod_spec = pl.BlockSpec((tm, 6, d), lambda i, f: (i, 0, 0))
    else:
        mod_spec = pl.BlockSpec((1, 6, d), lambda i, f: (i // tpb, 0, 0))
    return pl.pallas_call(
        functools.partial(_ffn_kernel, final=final), grid=(m // tm, nf),
        in_specs=[pl.BlockSpec((tm, d), row), mod_spec, pl.BlockSpec((1, d), lambda i, f: (0, 0)),
                  pl.BlockSpec((d, tf), lambda i, f: (0, f)),
                  pl.BlockSpec((d, tf), lambda i, f: (0, nf + f)),
                  pl.BlockSpec((tf, d), lambda i, f: (f, 0)),
                  pl.BlockSpec((1, d), lambda i, f: (0, 0))],
        out_specs=pl.BlockSpec((tm, d), row),
        out_shape=jax.ShapeDtypeStruct((m, d), F32),
        scratch_shapes=[pltpu.VMEM((tm, d), BF16), pltpu.VMEM((tm, d), F32)],
        compiler_params=_cparams(("arbitrary", "arbitrary")), name="ffn",
    )(x, mod, lw["norm_ffn_g"], lw["ffn_w13"], lw["ffn_w13"], lw["ffn_w2"], gfinal)


def _cmp_sample_kernel(q_ref, kc_ref, vc_ref, bias_ref, gate_ref, o_ref, imp_ref, *, nt, n_cmp):
    rows = nt * NSA_GROUP
    n_past = n_cmp // 2
    for g in range(NSA_KV_HEADS):
        qg = q_ref[0, :, g].reshape(rows, NSA_DIM)
        z = _dot_nt(qg, kc_ref[0, g]) + bias_ref[g]
        maskf = (bias_ref[g] > 0.5 * NEG_BIG).astype(F32)
        e = jnp.exp(z - jnp.max(z, axis=-1, keepdims=True)) * maskf
        p = e / jnp.maximum(jnp.sum(e, axis=-1, keepdims=True), 1e-30)
        o = _dot(p.astype(BF16), vc_ref[0, g]) * gate_ref[0, g, 0]
        o_ref[0, :, g] = o.reshape(nt, NSA_GROUP, NSA_DIM)
        imp = jnp.sum(p.reshape(nt, NSA_GROUP, n_cmp), axis=1)
        imp_ref[0, g] = imp[:, 0:n_past] + imp[:, n_past:n_cmp]


def _topk_sample_kernel(imp_ref, idx_ref, *, n_pick):
    imp = imp_ref[...]
    rows, n_past = imp.shape
    lane = lax.broadcasted_iota(jnp.int32, (rows, n_past), 1)
    score = jnp.where((lane == 0) | (lane == n_past - 1), jnp.inf, imp)

    def body(i, rank):
        ci = jnp.sum(jnp.where(lane == i, score, 0.0), axis=-1, keepdims=True)
        beats = (ci > score) | ((ci == score) & (i < lane))
        return rank + jnp.where(beats, 1, 0)
    rank = lax.fori_loop(0, n_past, body, jnp.zeros((rows, n_past), jnp.int32))
    out_lane = lax.broadcasted_iota(jnp.int32, (rows, 128), 1)
    picked = jnp.zeros((rows, 128), jnp.int32)
    for r in range(n_pick):
        ir = jnp.sum(jnp.where(rank == r, lane, 0), axis=-1, keepdims=True)
        picked = jnp.where(out_lane == r, ir, picked)
    idx_ref[...] = picked


def _topk_sample(imp2d, n_pick):
    rows, n_past = imp2d.shape
    return pl.pallas_call(
        functools.partial(_topk_sample_kernel, n_pick=n_pick),
        grid=(1,),
        in_specs=[pl.BlockSpec((rows, n_past), lambda i: (0, 0))],
        out_specs=pl.BlockSpec((rows, 128), lambda i: (0, 0)),
        out_shape=jax.ShapeDtypeStruct((rows, 128), jnp.int32),
        compiler_params=_cparams(("arbitrary",)), name="topk_blocks_sample",
    )(imp2d)


def _cmp_sample(q5, kc, vc, bias, gate_cols, nb, nt):
    n_cmp = kc.shape[2]
    g = NSA_KV_HEADS
    rows = nt * NSA_GROUP
    return pl.pallas_call(
        functools.partial(_cmp_sample_kernel, nt=nt, n_cmp=n_cmp),
        grid=(nb,),
        in_specs=[pl.BlockSpec((1, nt, g, NSA_GROUP, NSA_DIM), lambda b: (b, 0, 0, 0, 0)),
                  pl.BlockSpec((1, g, n_cmp, NSA_DIM), lambda b: (b, 0, 0, 0)),
                  pl.BlockSpec((1, g, n_cmp, NSA_DIM), lambda b: (b, 0, 0, 0)),
                  pl.BlockSpec((g, rows, n_cmp), lambda b: (0, 0, 0)),
                  pl.BlockSpec((1, g, 1, rows, 1), lambda b: (b, 0, 0, 0, 0))],
        out_specs=(pl.BlockSpec((1, nt, g, NSA_GROUP, NSA_DIM), lambda b: (b, 0, 0, 0, 0)),
                   pl.BlockSpec((1, g, nt, n_cmp // 2), lambda b: (b, 0, 0, 0))),
        out_shape=(jax.ShapeDtypeStruct((nb, nt, g, NSA_GROUP, NSA_DIM), F32),
                   jax.ShapeDtypeStruct((nb, g, nt, n_cmp // 2), F32)),
        compiler_params=_cparams(("arbitrary",)), name="cmp_attn_sample",
    )(q5, kc, vc, bias, gate_cols)


def _slc_sample_kernel(idx_ref, pt_ref, *refs, nt, n_pick, n_past, n_pages):
    del pt_ref, n_pages
    nblk = NSA_KV_HEADS * n_pick
    blk_refs = refs[:nblk]
    q_ref, new_ref, bias_ref, biasc_ref, gate_ref, o_ref = refs[nblk:]
    b = pl.program_id(0)
    t = pl.program_id(1)
    half = NSA_KV_HEADS * NSA_DIM
    lane_half = lax.broadcasted_iota(jnp.int32, (NSA_GROUP, PAGE_SIZE), 1) // SLC_BLOCK
    new = new_ref[0]
    groups = range(NSA_KV_HEADS)
    s_all, sn_all, vt_all = [], [], []
    for g in groups:
        qg = q_ref[0, 0, g]
        kt_list, vt_list, bias_list = [], [], []
        for n in range(n_pick):
            page = blk_refs[g * n_pick + n]
            j = idx_ref[((b * nt + t) * NSA_KV_HEADS + g) * n_pick + n]
            near = jnp.clip(j - (n_past - 3), 0, 2)
            kt_list.append(page[0, g * NSA_DIM:(g + 1) * NSA_DIM, :].astype(BF16))
            vt_list.append(page[0, half + g * NSA_DIM:half + (g + 1) * NSA_DIM, :].astype(BF16))
            bias_list.append(jnp.where(lane_half == (j & 1), bias_ref[0, g, near], NEG_BIG))
        s_all.append(_dot(qg, jnp.concatenate(kt_list, axis=1)) + jnp.concatenate(bias_list, axis=1))
        sn_all.append(_dot_nt(qg, new[:, g * NSA_DIM:(g + 1) * NSA_DIM].astype(BF16)) + biasc_ref[0, g])
        vt_all.append(jnp.concatenate(vt_list, axis=1))
    m_all = [jnp.maximum(s_all[g].max(axis=-1, keepdims=True), sn_all[g].max(axis=-1, keepdims=True)) for g in groups]
    p_all = [jnp.exp(s_all[g] - m_all[g]) for g in groups]
    pn_all = [jnp.exp(sn_all[g] - m_all[g]) for g in groups]
    for g in groups:
        l = jnp.sum(p_all[g], axis=-1, keepdims=True) + jnp.sum(pn_all[g], axis=-1, keepdims=True)
        acc = _dot_nt(p_all[g].astype(BF16), vt_all[g])
        acc = acc + _dot(pn_all[g].astype(BF16), new[:, half + g * NSA_DIM:half + (g + 1) * NSA_DIM].astype(BF16))
        o_ref[0, 0, g] = acc / l * gate_ref[0, g, 0, 0]


def _slc_sample(idx_flat, pt_flat, cache_blocks, q5, new_rows, bias_near, bias_cur, gate_cols, nb, nt, n_pick,
                n_past, n_pages):
    g = NSA_KV_HEADS

    def blk_map(gi, n):
        def f(b, t, idx, pt):
            j = idx[((b * nt + t) * g + gi) * n_pick + n]
            return (pt[b * n_pages + (j >> 1)], 0, 0)
        return f

    in_specs = [pl.BlockSpec((1, KV_ROW, PAGE_SIZE), blk_map(gi, n)) for gi in range(g) for n in range(n_pick)]
    in_specs += [
        pl.BlockSpec((1, 1, g, NSA_GROUP, NSA_DIM), lambda b, t, idx, pt: (b, t, 0, 0, 0)),
        pl.BlockSpec((1, 8, KV_ROW), lambda b, t, idx, pt: (b, 0, 0)),
        pl.BlockSpec((1, g, 3, NSA_GROUP, PAGE_SIZE), lambda b, t, idx, pt: (t, 0, 0, 0, 0)),
        pl.BlockSpec((1, g, NSA_GROUP, 8), lambda b, t, idx, pt: (t, 0, 0, 0)),
        pl.BlockSpec((1, g, 1, 1, NSA_GROUP, 1), lambda b, t, idx, pt: (b, 0, 0, t, 0, 0)),
    ]
    return pl.pallas_call(
        functools.partial(_slc_sample_kernel, nt=nt, n_pick=n_pick, n_past=n_past, n_pages=n_pages),
        grid_spec=pltpu.PrefetchScalarGridSpec(
            num_scalar_prefetch=2, grid=(nb, nt), in_specs=in_specs,
            out_specs=pl.BlockSpec((1, 1, g, NSA_GROUP, NSA_DIM), lambda b, t, idx, pt: (b, t, 0, 0, 0))),
        out_shape=jax.ShapeDtypeStruct((nb, nt, g, NSA_GROUP, NSA_DIM), F32),
        compiler_params=_cparams(("arbitrary", "arbitrary")), name="slc_attn_sample",
    )(idx_flat, pt_flat, *([cache_blocks] * (g * n_pick)), q5, new_rows, bias_near, bias_cur, gate_cols)


def _win_sample_kernel(q_ref, buf_ref, new_ref, bias_ref, biasn_ref, gate_ref, o_ref, *, nt):
    rows = nt * NSA_GROUP
    half = NSA_KV_HEADS * NSA_DIM
    buf = buf_ref[0, 0]
    new = new_ref[0]
    for g in range(NSA_KV_HEADS):
        qg = q_ref[0, :, g].reshape(rows, NSA_DIM)
        s1 = _dot(qg, buf[g * NSA_DIM:(g + 1) * NSA_DIM, :].astype(BF16)) + bias_ref[g]
        s2 = _dot_nt(qg, new[:, g * NSA_DIM:(g + 1) * NSA_DIM].astype(BF16)) + biasn_ref[g]
        m = jnp.maximum(s1.max(axis=-1, keepdims=True), s2.max(axis=-1, keepdims=True))
        p1 = jnp.exp(s1 - m)
        p2 = jnp.exp(s2 - m)
        l = jnp.sum(p1, axis=-1, keepdims=True) + jnp.sum(p2, axis=-1, keepdims=True)
        acc = _dot_nt(p1.astype(BF16), buf[half + g * NSA_DIM:half + (g + 1) * NSA_DIM, :].astype(BF16))
        acc = acc + _dot(p2.astype(BF16), new[:, half + g * NSA_DIM:half + (g + 1) * NSA_DIM].astype(BF16))
        o_ref[0, :, g] = (acc / l * gate_ref[0, g, 0]).reshape(nt, NSA_GROUP, NSA_DIM)


def _win_sample(q5, win_state, layer, new_rows, bias_buf, bias_new, gate_cols, nb, nt):
    g = NSA_KV_HEADS
    rows = nt * NSA_GROUP
    wlen = win_state.shape[3]
    return pl.pallas_call(
        functools.partial(_win_sample_kernel, nt=nt),
        grid=(nb,),
        in_specs=[pl.BlockSpec((1, nt, g, NSA_GROUP, NSA_DIM), lambda b: (b, 0, 0, 0, 0)),
                  pl.BlockSpec((1, 1, KV_ROW, wlen), lambda b: (layer, b, 0, 0)),
                  pl.BlockSpec((1, 8, KV_ROW), lambda b: (b, 0, 0)),
                  pl.BlockSpec((g, rows, wlen), lambda b: (0, 0, 0)),
                  pl.BlockSpec((g, rows, 8), lambda b: (0, 0, 0)),
                  pl.BlockSpec((1, g, 1, rows, 1), lambda b: (b, 0, 2, 0, 0))],
        out_specs=pl.BlockSpec((1, nt, g, NSA_GROUP, NSA_DIM), lambda b: (b, 0, 0, 0, 0)),
        out_shape=jax.ShapeDtypeStruct((nb, nt, g, NSA_GROUP, NSA_DIM), F32),
        compiler_params=_cparams(("arbitrary",)), name="win_attn_sample",
    )(q5, win_state, new_rows, bias_buf, bias_new, gate_cols)


def _mla_sample_kernel(pt_ref, *refs, pages, nt):
    del pt_ref
    page_refs = refs[:pages]
    q_ref, new_ref, o_ref, m_ref, l_ref, acc_ref = refs[pages:]
    step = pl.program_id(1)
    rows = MLA_HEADS * nt

    @pl.when(step == 0)
    def _():
        m_ref[...] = jnp.full(m_ref.shape, NEG_BIG, F32)
        l_ref[...] = jnp.zeros(l_ref.shape, F32)
        acc_ref[...] = jnp.zeros(acc_ref.shape, F32)

    q = q_ref[0]
    kt = jnp.concatenate([page_refs[k][0].astype(BF16) for k in range(pages)], axis=1)
    s = _dot(q, kt)
    m_old = m_ref[...]
    m_new = jnp.maximum(m_old, s.max(axis=-1, keepdims=True))
    alpha = jnp.exp(m_old - m_new)
    p = jnp.exp(s - m_new)
    m_ref[...] = m_new
    l_ref[...] = alpha * l_ref[...] + jnp.sum(p, axis=-1, keepdims=True)
    acc_ref[...] = alpha * acc_ref[...] + _dot_nt(p.astype(BF16), kt[0:KV_LORA, :])

    @pl.when(step == pl.num_programs(1) - 1)
    def _():
        new = new_ref[0].astype(BF16)
        tq = lax.broadcasted_iota(jnp.int32, (rows, 8), 0) % nt
        tk = lax.broadcasted_iota(jnp.int32, (rows, 8), 1)
        s = jnp.where(tk <= tq, _dot_nt(q, new), NEG_BIG)
        m_o = m_ref[...]
        m_n = jnp.maximum(m_o, s.max(axis=-1, keepdims=True))
        a = jnp.exp(m_o - m_n)
        p = jnp.exp(s - m_n)
        lf = a * l_ref[...] + jnp.sum(p, axis=-1, keepdims=True)
        accf = a * acc_ref[...] + _dot(p.astype(BF16), new[:, 0:KV_LORA])
        o_ref[0] = (accf / lf).astype(BF16)


def _mla_sample(pt_flat, cache_pages, q_rows, new_rows, nb, nt, n_pages):
    pages = MLA_PAGES
    steps = n_pages // pages
    rows = MLA_HEADS * nt

    def page_map(k):
        return lambda b, s, pt: (pt[b * n_pages + s * pages + k], 0, 0)

    in_specs = [pl.BlockSpec((1, LAT, PAGE_SIZE), page_map(k)) for k in range(pages)]
    in_specs += [pl.BlockSpec((1, rows, LAT), lambda b, s, pt: (b, 0, 0)),
                 pl.BlockSpec((1, 8, LAT), lambda b, s, pt: (b, 0, 0))]
    return pl.pallas_call(
        functools.partial(_mla_sample_kernel, pages=pages, nt=nt),
        grid_spec=pltpu.PrefetchScalarGridSpec(
            num_scalar_prefetch=1, grid=(nb, steps), in_specs=in_specs,
            out_specs=pl.BlockSpec((1, rows, KV_LORA), lambda b, s, pt: (b, 0, 0)),
            scratch_shapes=[pltpu.VMEM((rows, 1), F32), pltpu.VMEM((rows, 1), F32),
                            pltpu.VMEM((rows, KV_LORA), F32)]),
        out_shape=jax.ShapeDtypeStruct((nb, rows, KV_LORA), BF16),
        compiler_params=_cparams(("arbitrary", "arbitrary")), name="mla_attn_sample",
    )(pt_flat, *([cache_pages] * pages), q_rows, new_rows)


def _pack_layer(l, w):
    d = D_MODEL
    w_in = w["w_in"][l]
    sizes = (NSA_HEADS * NSA_DIM, KV_ROW, KV_ROW, KV_ROW, 3 * NSA_HEADS, Q_LORA, KV_LORA, MLA_ROPE, 2 * d)
    offs = np.concatenate([[0], np.cumsum(sizes)])
    seg = [w_in[:, offs[i]:offs[i + 1]] for i in range(len(sizes))]
    gsrc = seg[4].reshape(d, NSA_KV_HEADS, NSA_GROUP, 3).transpose(0, 1, 3, 2).reshape(d, NSA_KV_HEADS, 3 * NSA_GROUP)
    gates = jnp.pad(gsrc, ((0, 0), (0, 0), (0, GATE_LANES - 3 * NSA_GROUP))).reshape(d, NSA_KV_HEADS * GATE_LANES)
    kr = jnp.pad(seg[7], ((0, 0), (0, 128 - MLA_ROPE)))
    w_packed = jnp.concatenate([seg[0], seg[1], seg[2], seg[3], gates, seg[5], seg[6], kr, seg[8]], axis=1)
    wuq = w["mla_w_uq"][l].reshape(Q_LORA, MLA_HEADS, MLA_NOPE + MLA_ROPE)
    wuq = jnp.concatenate([wuq[:, :, :MLA_NOPE].reshape(Q_LORA, -1), wuq[:, :, MLA_NOPE:].reshape(Q_LORA, -1)], axis=1)
    w1 = w["nsa_cmp_w1"][l].reshape(2, CMP_BLOCK, NSA_DIM, NSA_DIM)
    def diag_blocks(blocks):
        rows = []
        for i in range(2):
            for g in range(NSA_KV_HEADS):
                off = (i * NSA_KV_HEADS + g) * NSA_DIM
                pad = [(0, 0)] * (blocks[i].ndim - 1) + [(off, KV_ROW - NSA_DIM - off)]
                rows.append(jnp.pad(blocks[i], pad))
        return jnp.concatenate(rows, axis=-2)
    w1big = diag_blocks([w1[0], w1[1]])
    w2big = diag_blocks([w["nsa_cmp_w2"][l][0], w["nsa_cmp_w2"][l][1]])
    pe = w["nsa_cmp_pe"][l]
    pe_big = jnp.broadcast_to(pe.transpose(1, 0, 2)[:, :, None, :], (CMP_BLOCK, 2, NSA_KV_HEADS, NSA_DIM))
    b1big = jnp.broadcast_to(w["nsa_cmp_b1"][l][:, None, :], (2, NSA_KV_HEADS, NSA_DIM))
    return dict(
        norm_mix_g=w["norm_mix_g"][l].reshape(1, d),
        w_in=w_packed.astype(BF16),
        mla_q_norm_g=w["mla_q_norm_g"][l].reshape(1, Q_LORA),
        w_uq=wuq.astype(BF16),
        w_ukT=w["mla_w_uk"][l].transpose(1, 2, 0).astype(BF16),
        mla_kv_norm_g=w["mla_kv_norm_g"][l].reshape(1, KV_LORA),
        w_uv=w["mla_w_uv"][l].transpose(1, 0, 2).astype(BF16),
        w_out=w["w_out"][l].astype(BF16),
        norm_ffn_g=w["norm_ffn_g"][l].reshape(1, d),
        ffn_w13=w["ffn_w13"][l].astype(BF16),
        ffn_w2=w["ffn_w2"][l].astype(BF16),
        cmp_pe=pe_big.reshape(CMP_BLOCK, 1, KV_ROW),
        cmp_pe_t=jnp.tile(pe_big.reshape(CMP_BLOCK, KV_ROW).T, (1, 2 * PAGE_SIZE // CMP_BLOCK)),
        cmp_w1=w1big.astype(BF16),
        cmp_b1=b1big.reshape(1, KV_ROW),
        cmp_w2=w2big.astype(BF16),
    )


def _cmp_block_ends(n_cmp):
    order = np.concatenate([np.arange(0, n_cmp, 2), np.arange(1, n_cmp, 2)])
    return (order + 1) * CMP_BLOCK - 1


def _prompt_bias_tables(table, nt):
    tq = ATT_TILE
    i = np.arange(tq)[:, None]
    j = np.arange(tq)[None, :]
    d0, d1, d2 = i - j, tq + i - j, 2 * tq + i - j
    bucket = np.concatenate([_t5_bucket_np(d0), _t5_bucket_np(d1), _t5_bucket_np(d2)], axis=0)
    mask = np.concatenate([np.where(d0 >= 0, 0.0, NEG_BIG), np.zeros((tq, tq)),
                           np.where(d2 < WINDOW, 0.0, NEG_BIG)], axis=0).astype(np.float32)
    tiles = _expand_bias(table, bucket, mask, True).reshape(NSA_HEADS, 3, tq, tq)
    n_cmp = nt // CMP_BLOCK
    dist = np.arange(nt)[None, :] - _cmp_block_ends(n_cmp)[:, None]
    cmp_bias = _expand_bias(table, _t5_bucket_np(dist), np.where(dist >= 0, 0.0, NEG_BIG).astype(np.float32), False)
    expand = (np.arange(SEL_LANES)[None, :, None] ==
              (np.arange(nt // tq)[:, None, None] * tq + np.arange(tq)[None, None, :]) // SLC_BLOCK)
    return tiles, cmp_bias, jnp.asarray(expand.astype(np.float32), dtype=BF16)


def _sample_bias_tables(table, nt, past, wlen):
    g, hg = NSA_KV_HEADS, NSA_GROUP
    rows = nt * hg
    q_pos = past + np.arange(nt)

    def per_group(b, width):
        return b.reshape(g, hg, nt, width).transpose(0, 2, 1, 3).reshape(g, rows, width)

    n_cmp = past // CMP_BLOCK
    dist = q_pos[:, None] - _cmp_block_ends(n_cmp)[None, :]
    cmp_bias = per_group(_expand_bias(table, _t5_bucket_np(dist),
                                      np.where(dist >= 0, 0.0, NEG_BIG).astype(np.float32), False), n_cmp)
    dist = q_pos[:, None] - (past - wlen + np.arange(wlen))[None, :]
    ok = (dist >= 0) & (dist < WINDOW)
    win_bias = per_group(_expand_bias(table, _t5_bucket_np(dist), np.where(ok, 0.0, NEG_BIG).astype(np.float32),
                                      False), wlen)
    dist = q_pos[:, None] - (past + np.arange(8))[None, :]
    ok = (dist >= 0) & (dist < WINDOW) & (np.arange(8)[None, :] < nt)
    new_mask = np.where(ok, 0.0, NEG_BIG).astype(np.float32)
    win_new = per_group(_expand_bias(table, _t5_bucket_np(dist), new_mask, False), 8)
    n_past = past // SLC_BLOCK
    near = np.stack([np.full((nt, SLC_BLOCK), 10 * MAX_DISTANCE),
                     q_pos[:, None] - ((n_past - 2) * SLC_BLOCK + np.arange(SLC_BLOCK))[None, :],
                     q_pos[:, None] - ((n_past - 1) * SLC_BLOCK + np.arange(SLC_BLOCK))[None, :]], axis=1)
    near = near.reshape(nt, 3 * SLC_BLOCK)
    slc_near = _expand_bias(table, _t5_bucket_np(near), np.zeros(near.shape, np.float32), True)
    slc_near = slc_near.reshape(g, hg, nt, 3, SLC_BLOCK).transpose(2, 0, 3, 1, 4)
    slc_near = jnp.tile(slc_near, (1, 1, 1, 1, PAGE_SIZE // SLC_BLOCK))
    slc_cur = _expand_bias(table, _t5_bucket_np(dist), np.where((dist >= 0) & (np.arange(8)[None, :] < nt), 0.0,
                                                                 NEG_BIG).astype(np.float32), True)
    slc_cur = slc_cur.reshape(g, hg, nt, 8).transpose(2, 0, 1, 3)
    return cmp_bias, win_bias, win_new, slc_near, slc_cur


def kernel(x_prompt, x_sample, cache_cmp_kv, cache_slc_kv, cache_mla, state_win_kv, page_table, c_prompt, c_sample,
           rel_bias_table, ada_w, ada_b, norm_mix_g, w_in, nsa_cmp_pe, nsa_cmp_w1, nsa_cmp_b1, nsa_cmp_w2,
           mla_q_norm_g, mla_w_uq, mla_kv_norm_g, mla_w_uk, mla_w_uv, w_out, norm_ffn_g, ffn_w13, ffn_w2,
           final_norm_g):
    weights = dict(norm_mix_g=norm_mix_g, w_in=w_in, nsa_cmp_pe=nsa_cmp_pe, nsa_cmp_w1=nsa_cmp_w1,
                   nsa_cmp_b1=nsa_cmp_b1, nsa_cmp_w2=nsa_cmp_w2, mla_q_norm_g=mla_q_norm_g, mla_w_uq=mla_w_uq,
                   mla_kv_norm_g=mla_kv_norm_g, mla_w_uk=mla_w_uk, mla_w_uv=mla_w_uv, w_out=w_out,
                   norm_ffn_g=norm_ffn_g, ffn_w13=ffn_w13, ffn_w2=ffn_w2)
    depth = ada_w.shape[0]
    d = D_MODEL
    g = NSA_KV_HEADS
    pb, pt_len, _ = x_prompt.shape
    sb, st_len, _ = x_sample.shape
    n_pool = cache_cmp_kv.shape[1]
    n_pages = page_table.shape[1]
    past = n_pages * PAGE_SIZE
    wlen = state_win_kv.shape[2]
    n_past_blocks = past // SLC_BLOCK
    n_pick = N_SELECT - 1
    assert pt_len % ATT_TILE == 0 and pt_len >= WINDOW and st_len <= 8 and past >= wlen
    assert n_past_blocks > n_pick and n_pages % MLA_PAGES == 0

    layers = [_pack_layer(l, weights) for l in range(depth)]
    gfinal = final_norm_g.reshape(1, d)
    mod_all = _modulation(jnp.concatenate([c_prompt, c_sample], axis=0), ada_w, ada_b)

    tiles, cmp_bias_p, expand = _prompt_bias_tables(rel_bias_table, pt_len)
    cmp_bias_s, win_bias_s, win_new_s, slc_near_s, slc_cur_s = _sample_bias_tables(rel_bias_table, st_len, past, wlen)
    cos_p, sin_p = _rope_tables(jnp.arange(pt_len), MLA_HEADS)
    cos_s, sin_s = _rope_tables(jnp.tile(past + jnp.arange(st_len), sb), MLA_HEADS)
    prompt_pages = jnp.arange(pb * pt_len // PAGE_SIZE, dtype=jnp.int32)
    pt_flat = page_table.reshape(-1).astype(jnp.int32)

    x = x_prompt.reshape(pb * pt_len, d)
    tm = 256
    states = (jnp.zeros((depth, pb, KV_ROW, pt_len), F32), jnp.zeros((depth, pb, KV_ROW, pt_len), F32),
              jnp.zeros((depth, pb, LAT, pt_len), F32))
    win_t = []
    for l in range(depth):
        lw = layers[l]
        mod = mod_all[l, :pb].reshape(pb, 6, d)
        (q, kvc, _, _, kst, kse, kso, kwt, kwe, kwo, gates, qmla, _, latv, latt, gm, st_cmp, st_slc, st_lat,
         kvw_t) = _inproj(x, mod, False, lw, cos_p, sin_p, pb, pt_len, tm, states=states, layer=l)
        states = (st_cmp, st_slc, st_lat)
        win_t.append(kvw_t[:, :, pt_len - min(WINDOW, pt_len):])
        tok = _compress(kvc.reshape(-1, PAGE_SIZE, KV_ROW), prompt_pages, lw)
        kc, vc = _split_compressed(tok, pb)
        o_cmp, sel = _cmp_prompt(q, kc, vc.transpose(0, 1, 3, 2), cmp_bias_p, gates, pb, pt_len)
        o_slc = _nsa_flash(q, kst, kse, kso, tiles, gates, pb, pt_len, sel=sel, expand=expand)
        o_win = _nsa_flash(q, kwt, kwe, kwo, tiles, gates, pb, pt_len)
        o_lat = _mla_prompt(qmla, latt, latv, pb, pt_len)
        x = _merge(o_cmp.reshape(-1, d), o_slc.reshape(-1, d), o_win.reshape(-1, d), o_lat, gm, x, mod, False, lw,
                   pb, pt_len, tm)
        x = _ffn(x, mod, False, lw, gfinal, l == depth - 1, pb, pt_len, 512)
    y_prompt = x.reshape(pb, pt_len, d)

    def rows_first(a):
        return a.reshape(depth, pb, 2, g, NSA_DIM, a.shape[-1]).transpose(0, 1, 5, 2, 3, 4)
    st_p = (rows_first(states[0]), rows_first(states[1]), states[2].transpose(0, 1, 3, 2),
            rows_first(jnp.stack(win_t)))

    ms = sb * st_len
    x = x_sample.reshape(ms, d)
    st_s = [[], [], [], []]
    cmp_pages = cache_cmp_kv.transpose(0, 1, 3, 4, 5, 2).reshape(depth * n_pool, KV_ROW, PAGE_SIZE)
    slc_pages = cache_slc_kv.transpose(0, 1, 3, 4, 5, 2).reshape(depth * n_pool, KV_ROW, PAGE_SIZE)
    mla_pages = cache_mla.transpose(0, 1, 3, 2).reshape(depth * n_pool, LAT, PAGE_SIZE)
    win_state = state_win_kv.transpose(0, 1, 3, 4, 5, 2).reshape(depth, sb, KV_ROW, wlen)
    for l in range(depth):
        lw = layers[l]
        mod = jnp.repeat(mod_all[l, pb:], st_len, axis=0).reshape(ms, 6, d)
        (q, kvc, kvs, kvw, _, _, _, _, _, _, gates, qmla, lat, _, _, gm) = _inproj(
            x, mod, True, lw, cos_s, sin_s, 1, ms, ms)
        pt_l = pt_flat + l * n_pool
        tok = _compress_t(cmp_pages, pt_l, lw)
        kc, vc = _split_compressed(tok, sb)
        q5 = q.reshape(g, NSA_GROUP, sb, st_len, NSA_DIM).transpose(2, 3, 0, 1, 4)
        gate_cols = gates.reshape(sb, st_len, g, GATE_LANES)[..., :3 * NSA_GROUP]
        gate_cols = gate_cols.reshape(sb, st_len, g, 3, NSA_GROUP).transpose(0, 2, 3, 1, 4)
        gate_rows = gate_cols.reshape(sb, g, 3, st_len * NSA_GROUP, 1)
        o_cmp, imp = _cmp_sample(q5, kc, vc, cmp_bias_s, gate_rows, sb, st_len)
        idx = _topk_sample(imp.reshape(sb * g * st_len, n_past_blocks), n_pick)
        idx_flat = idx[:, :n_pick].reshape(sb, g, st_len, n_pick).transpose(0, 2, 1, 3).reshape(-1)
        pad8 = lambda a: jnp.pad(a.reshape(sb, st_len, -1), ((0, 0), (0, 8 - st_len), (0, 0)))
        o_slc = _slc_sample(idx_flat, pt_l, slc_pages, q5, pad8(kvs), slc_near_s, slc_cur_s,
                            gate_cols.reshape(sb, g, 3, st_len, NSA_GROUP, 1)[:, :, 1:2], sb, st_len, n_pick,
                            n_past_blocks, n_pages)
        o_win = _win_sample(q5, win_state, l, pad8(kvw), win_bias_s, win_new_s, gate_rows, sb, st_len)
        q_rows = qmla.reshape(MLA_HEADS, sb, st_len, LAT).transpose(1, 0, 2, 3).reshape(sb, MLA_HEADS * st_len, LAT)
        o_lat = _mla_sample(pt_l, mla_pages, q_rows, pad8(lat), sb, st_len, n_pages)
        o_lat = o_lat.reshape(sb, MLA_HEADS, st_len, KV_LORA).transpose(1, 0, 2, 3).reshape(1, MLA_HEADS, ms, KV_LORA)
        x = _merge(o_cmp.reshape(ms, d), o_slc.reshape(ms, d), o_win.reshape(ms, d), o_lat, gm, x, mod, True, lw,
                   1, ms, ms)
        x = _ffn(x, mod, True, lw, gfinal, l == depth - 1, 1, ms, ms)
        st_s[0].append(kvc.reshape(sb, st_len, 2, g, NSA_DIM))
        st_s[1].append(kvs.reshape(sb, st_len, 2, g, NSA_DIM))
        st_s[2].append(lat.reshape(sb, st_len, LAT))
        win_all = jnp.concatenate([state_win_kv[l], kvw.reshape(sb, st_len, 2, g, NSA_DIM)], axis=1)
        st_s[3].append(win_all[:, st_len:])
    y_sample = x.reshape(sb, st_len, d)

    return (y_prompt, y_sample, st_p[0], st_p[1], st_p[2], st_p[3],
            jnp.stack(st_s[0]), jnp.stack(st_s[1]), jnp.stack(st_s[2]), jnp.stack(st_s[3]))
```

```python
import functools
import math

import numpy as np
import jax
import jax.numpy as jnp
from jax import lax
from jax.experimental import pallas as pl
from jax.experimental.pallas import tpu as pltpu

F32 = jnp.float32
BF16 = jnp.bfloat16

D_MODEL = 1024
PAGE_SIZE = 128
NSA_HEADS = 16
NSA_KV_HEADS = 2
NSA_GROUP = NSA_HEADS // NSA_KV_HEADS
NSA_DIM = D_MODEL // NSA_HEADS
CMP_BLOCK = 32
SLC_BLOCK = 64
N_SELECT = 16
WINDOW = 512
MLA_HEADS = 8
MLA_NOPE = 128
MLA_ROPE = 64
MLA_V = D_MODEL // MLA_HEADS
Q_LORA = 384
KV_LORA = 256
LAT = KV_LORA + MLA_ROPE
MLA_QK = MLA_NOPE + MLA_ROPE
ROPE_THETA = 10000.0
N_BUCKETS = 32
MAX_DISTANCE = 128
NSA_SCALE = NSA_DIM ** -0.5
MLA_SCALE = (MLA_NOPE + MLA_ROPE) ** -0.5
NEG_BIG = -1e30
RMS_EPS = 1e-6
KV_ROW = 2 * NSA_KV_HEADS * NSA_DIM
GATE_LANES = 128

SEG_Q = (0, 1024)
SEG_CMP = (1024, 1280)
SEG_SLC = (1280, 1536)
SEG_WIN = (1536, 1792)
SEG_GATE = (1792, 2048)
SEG_CQ = (2048, 2432)
SEG_CKV = (2432, 2688)
SEG_KR = (2688, 2816)
SEG_GM = (2816, 4864)
D_IN_PACKED = 4864

ATT_TILE = 256
FLASH_ROWS = 128
MLA_FLASH_ROWS = 512
SEL_LANES = 128
VMEM_LIMIT = 56 * 1024 * 1024
CMP_PAGES = 64
MLA_PAGES = 32


def _cparams(sem):
    return pltpu.CompilerParams(dimension_semantics=sem, vmem_limit_bytes=VMEM_LIMIT)


def _dot(a, b):
    return jnp.dot(a, b, preferred_element_type=F32)


def _dot_nt(a, b):
    return lax.dot_general(a, b, (((1,), (1,)), ((), ())), preferred_element_type=F32)


def _rms(x, g):
    return x * lax.rsqrt(jnp.mean(x * x, axis=-1, keepdims=True) + RMS_EPS) * g


def _rope_lanes(x, cos2, sin2):
    w = x.shape[-1]
    lane = lax.broadcasted_iota(jnp.int32, x.shape, 1)
    swapped = jnp.where(lane % MLA_ROPE < MLA_ROPE // 2, pltpu.roll(x, w - MLA_ROPE // 2, 1),
                        pltpu.roll(x, MLA_ROPE // 2, 1))
    return x * cos2 + swapped * sin2


def _t5_bucket_np(dist):
    max_exact = N_BUCKETS // 2
    d = np.maximum(dist, 0)
    log_ratio = np.log(np.maximum(d, 1).astype(np.float32) / max_exact) / math.log(MAX_DISTANCE / max_exact)
    large = np.minimum(max_exact + (log_ratio * (N_BUCKETS - max_exact)).astype(np.int32), N_BUCKETS - 1)
    return np.where(d < max_exact, d, large).astype(np.int32)


def _rope_tables(pos, reps):
    half = MLA_ROPE // 2
    inv = ROPE_THETA ** (-jnp.arange(half, dtype=F32) / half)
    ang = pos.astype(F32)[:, None] * inv[None, :]
    cos, sin = jnp.cos(ang), jnp.sin(ang)
    cos2 = jnp.concatenate([cos, cos], axis=-1)
    sin2 = jnp.concatenate([-sin, sin], axis=-1)
    return jnp.tile(cos2, (1, reps)), jnp.tile(sin2, (1, reps))


def _bias_kernel(table_ref, bucket_ref, mask_ref, o_ref, *, shift):
    h = pl.program_id(0)
    bucket = bucket_ref[...]
    acc = jnp.zeros(bucket.shape, F32)
    for b in range(N_BUCKETS):
        acc = jnp.where(bucket == b, table_ref[b, h], acc)
    if shift:
        acc = acc - table_ref[N_BUCKETS - 1, h]
    o_ref[0] = acc + mask_ref[...]


def _expand_bias(table, bucket, addmask, shift):
    r, c = bucket.shape
    return pl.pallas_call(
        functools.partial(_bias_kernel, shift=shift),
        grid=(NSA_HEADS,),
        in_specs=[pl.BlockSpec(memory_space=pltpu.SMEM),
                  pl.BlockSpec((r, c), lambda h: (0, 0)),
                  pl.BlockSpec((r, c), lambda h: (0, 0))],
        out_specs=pl.BlockSpec((1, r, c), lambda h: (h, 0, 0)),
        out_shape=jax.ShapeDtypeStruct((NSA_HEADS, r, c), F32),
        compiler_params=_cparams(("arbitrary",)),
        name="bias_expand",
    )(table, jnp.asarray(bucket), jnp.asarray(addmask))


def _mod_kernel(c_ref, w_ref, b_ref, o_ref):
    c = c_ref[...]
    cond = (c * jax.nn.sigmoid(c)).astype(BF16)
    o_ref[0] = _dot(cond, w_ref[0].astype(BF16)) + b_ref[0]


def _modulation(c_all, ada_w, ada_b):
    depth, d, n = ada_w.shape
    rows = c_all.shape[0]
    tn = 1536
    return pl.pallas_call(
        _mod_kernel,
        grid=(depth, n // tn),
        in_specs=[pl.BlockSpec((rows, d), lambda l, j: (0, 0)),
                  pl.BlockSpec((1, d, tn), lambda l, j: (l, 0, j)),
                  pl.BlockSpec((1, 1, tn), lambda l, j: (l, 0, j))],
        out_specs=pl.BlockSpec((1, rows, tn), lambda l, j: (l, 0, j)),
        out_shape=jax.ShapeDtypeStruct((depth, rows, n), F32),
        compiler_params=_cparams(("arbitrary", "arbitrary")),
        name="adaln_mod",
    )(c_all, ada_w, ada_b.reshape(depth, 1, n))


def _inproj_kernel(*refs, with_states):
    (x_ref, mod_ref, gn_ref, w_ref, cos_ref, sin_ref, gq_ref, wuq_ref, wuk_ref, gkv_ref) = refs[:10]
    outs = refs[13:] if with_states else refs[10:]
    (q_ref, kvc_ref, kvs_ref, kvw_ref, kst_ref, kse_ref, kso_ref, kwt_ref, kwe_ref, kwo_ref, gate_ref,
     qmla_ref, lat_ref, latb_ref, latt_ref, gm_ref) = outs[:16]
    x = x_ref[...]
    y = _rms(x, gn_ref[...])
    h = (y * (1.0 + mod_ref[:, 1, :]) + mod_ref[:, 0, :]).astype(BF16)

    def seg(s):
        return _dot(h, w_ref[:, s[0]:s[1]])

    q = (seg(SEG_Q) * NSA_SCALE).astype(BF16)
    for hd in range(NSA_HEADS):
        q_ref[0, hd] = q[:, hd * NSA_DIM:(hd + 1) * NSA_DIM]
    kvc = seg(SEG_CMP)
    kvc_ref[...] = kvc
    half = NSA_KV_HEADS * NSA_DIM
    low = lax.broadcasted_iota(jnp.int32, (x.shape[0], half), 1) < NSA_DIM
    kv_t = {}
    for s, kv_ref, kt_ref, ve_ref, vo_ref in ((SEG_SLC, kvs_ref, kst_ref, kse_ref, kso_ref),
                                              (SEG_WIN, kvw_ref, kwt_ref, kwe_ref, kwo_ref)):
        kv = seg(s)
        kv_ref[...] = kv
        kv_t[s] = kv.T
        kt = kv_t[s][0:half].astype(BF16)
        vv = kv[:, half:2 * half]
        vr = pltpu.roll(vv, NSA_DIM, 1)
        for g in range(NSA_KV_HEADS):
            kt_ref[0, g] = kt[g * NSA_DIM:(g + 1) * NSA_DIM]
            ve_ref[0, g] = jnp.where(low, vv if g == 0 else vr, 1.0).astype(BF16)
            vo_ref[0, g] = jnp.where(low, 1.0, vr if g == 0 else vv).astype(BF16)
    gate_ref[...] = jax.nn.sigmoid(seg(SEG_GATE))
    gm_ref[...] = jax.nn.sigmoid(seg(SEG_GM))

    cos = cos_ref[...]
    sin = sin_ref[...]
    cqn = _rms(seg(SEG_CQ), gq_ref[...]).astype(BF16)
    qm = _dot(cqn, wuq_ref[...])
    nope_w = MLA_HEADS * MLA_NOPE
    qr = _rope_lanes(qm[:, nope_w:], cos, sin) * MLA_SCALE
    for hd in range(MLA_HEADS):
        qn = qm[:, hd * MLA_NOPE:(hd + 1) * MLA_NOPE]
        qrh = qr[:, hd * MLA_ROPE:(hd + 1) * MLA_ROPE].astype(BF16)
        if with_states:
            qmla_ref[0, hd, :, 0:MLA_NOPE] = (qn * MLA_SCALE).astype(BF16)
            qmla_ref[0, hd, :, MLA_NOPE:MLA_QK] = qrh
        else:
            qmla_ref[0, hd, :, 0:KV_LORA] = (_dot(qn.astype(BF16), wuk_ref[hd]) * MLA_SCALE).astype(BF16)
            qmla_ref[0, hd, :, KV_LORA:LAT] = qrh
    ckv = _rms(seg(SEG_CKV), gkv_ref[...])
    kr = _rope_lanes(seg(SEG_KR), cos[:, 0:128], sin[:, 0:128])
    lat_ref[:, 0:KV_LORA] = ckv
    lat_ref[:, KV_LORA:LAT] = kr[:, 0:MLA_ROPE]
    latb_ref[...] = ckv.astype(BF16)
    ckv_t = ckv.T
    kr_t = kr.T[0:MLA_ROPE]
    if with_states:
        ckv_tb = ckv_t.astype(BF16)
        for hd in range(MLA_HEADS):
            latt_ref[0, hd, 0:MLA_NOPE, :] = _dot(wuk_ref[hd], ckv_tb).astype(BF16)
            latt_ref[0, hd, MLA_NOPE:MLA_QK, :] = kr_t.astype(BF16)
    else:
        latt_ref[0, 0:KV_LORA, :] = ckv_t.astype(BF16)
        latt_ref[0, KV_LORA:LAT, :] = kr_t.astype(BF16)
    if with_states:
        stc_ref, sts_ref, stl_ref, stw_ref = outs[16:]
        stc_ref[0, 0] = kvc.T
        sts_ref[0, 0] = kv_t[SEG_SLC]
        stw_ref[0] = kv_t[SEG_WIN]
        stl_ref[0, 0, 0:KV_LORA, :] = ckv_t
        stl_ref[0, 0, KV_LORA:LAT, :] = kr_t


def _inproj(x, mod, per_row_mod, lw, cos_t, sin_t, nb, nt, tm, states=None, layer=0):
    m = nb * nt
    tpb = nt // tm
    d = D_MODEL
    if per_row_mod:
        mod_spec = pl.BlockSpec((tm, 6, d), lambda i: (i, 0, 0))
    else:
        mod_spec = pl.BlockSpec((1, 6, d), lambda i: (i // tpb, 0, 0))
    const2 = lambda i: (0, 0)
    row = lambda i: (i, 0)
    bt = lambda i: (i // tpb, 0, i % tpb, 0)
    btt = lambda i: (i // tpb, 0, 0, i % tpb)
    g = NSA_KV_HEADS
    prompt = states is not None
    out_shape = (
        jax.ShapeDtypeStruct((nb, NSA_HEADS, nt, NSA_DIM), BF16),
        jax.ShapeDtypeStruct((m, KV_ROW), F32),
        jax.ShapeDtypeStruct((m, KV_ROW), F32),
        jax.ShapeDtypeStruct((m, KV_ROW), F32),
        jax.ShapeDtypeStruct((nb, g, NSA_DIM, nt), BF16),
        jax.ShapeDtypeStruct((nb, g, nt, 2 * NSA_DIM), BF16),
        jax.ShapeDtypeStruct((nb, g, nt, 2 * NSA_DIM), BF16),
        jax.ShapeDtypeStruct((nb, g, NSA_DIM, nt), BF16),
        jax.ShapeDtypeStruct((nb, g, nt, 2 * NSA_DIM), BF16),
        jax.ShapeDtypeStruct((nb, g, nt, 2 * NSA_DIM), BF16),
        jax.ShapeDtypeStruct((m, g * GATE_LANES), F32),
        jax.ShapeDtypeStruct((nb, MLA_HEADS, nt, MLA_QK if prompt else LAT), BF16),
        jax.ShapeDtypeStruct((m, LAT), F32),
        jax.ShapeDtypeStruct((m, KV_LORA), BF16),
        jax.ShapeDtypeStruct((nb, MLA_HEADS, MLA_QK, nt) if prompt else (nb, LAT, nt), BF16),
        jax.ShapeDtypeStruct((m, 2 * d), F32),
    )
    kt_spec = pl.BlockSpec((1, g, NSA_DIM, tm), btt)
    v_spec = pl.BlockSpec((1, g, tm, 2 * NSA_DIM), bt)
    out_specs = (
        pl.BlockSpec((1, NSA_HEADS, tm, NSA_DIM), bt),
        pl.BlockSpec((tm, KV_ROW), row), pl.BlockSpec((tm, KV_ROW), row), pl.BlockSpec((tm, KV_ROW), row),
        kt_spec, v_spec, v_spec, kt_spec, v_spec, v_spec,
        pl.BlockSpec((tm, g * GATE_LANES), row),
        pl.BlockSpec((1, MLA_HEADS, tm, MLA_QK if prompt else LAT), bt),
        pl.BlockSpec((tm, LAT), row), pl.BlockSpec((tm, KV_LORA), row),
        pl.BlockSpec((1, MLA_HEADS, MLA_QK, tm), btt) if prompt else
        pl.BlockSpec((1, LAT, tm), lambda i: (i // tpb, 0, i % tpb)),
        pl.BlockSpec((tm, 2 * d), row),
    )
    in_specs = [
        pl.BlockSpec((tm, d), row), mod_spec, pl.BlockSpec((1, d), const2),
        pl.BlockSpec((d, D_IN_PACKED), const2),
        pl.BlockSpec((tm, MLA_HEADS * MLA_ROPE), lambda i: (i % tpb, 0)),
        pl.BlockSpec((tm, MLA_HEADS * MLA_ROPE), lambda i: (i % tpb, 0)),
        pl.BlockSpec((1, Q_LORA), const2),
        pl.BlockSpec((Q_LORA, MLA_HEADS * (MLA_NOPE + MLA_ROPE)), const2),
        pl.BlockSpec((MLA_HEADS, MLA_NOPE, KV_LORA), lambda i: (0, 0, 0)),
        pl.BlockSpec((1, KV_LORA), const2),
    ]
    args = [x, mod, lw["norm_mix_g"], lw["w_in"], cos_t, sin_t, lw["mla_q_norm_g"], lw["w_uq"], lw["w_ukT"],
            lw["mla_kv_norm_g"]]
    aliases = {}
    if states is not None:
        n_in, n_out = len(args), len(out_shape)
        st_block = lambda width: pl.BlockSpec((1, 1, width, tm), lambda i: (layer, i // tpb, 0, i % tpb))
        in_specs += [pl.BlockSpec(memory_space=pl.ANY)] * 3
        args += list(states)
        out_shape += tuple(jax.ShapeDtypeStruct(s.shape, s.dtype) for s in states)
        out_shape += (jax.ShapeDtypeStruct((nb, KV_ROW, nt), F32),)
        out_specs += (st_block(KV_ROW), st_block(KV_ROW), st_block(LAT),
                      pl.BlockSpec((1, KV_ROW, tm), lambda i: (i // tpb, 0, i % tpb)))
        aliases = {n_in + k: n_out + k for k in range(3)}
    return pl.pallas_call(
        functools.partial(_inproj_kernel, with_states=states is not None),
        grid=(m // tm,), in_specs=in_specs, out_specs=out_specs, out_shape=out_shape,
        input_output_aliases=aliases,
        compiler_params=_cparams(("arbitrary",)), name="in_proj",
    )(*args)


def _gelu_tanh(z):
    return 0.5 * z * (1.0 + jnp.tanh(math.sqrt(2.0 / math.pi) * (z + 0.044715 * (z * z * z))))


def _compress_t_kernel(tab_ref, *refs, pages):
    del tab_ref
    page_refs = refs[:pages]
    pet_ref, sel_ref, w1_ref, b1_ref, w2_ref, o_ref, slab_ref = refs[pages:]
    pairs = pages // 2
    rows_per_pair = 2 * (PAGE_SIZE // CMP_BLOCK)
    sel = sel_ref[...]
    pet = pet_ref[...]
    for pr in range(pairs):
        xt2 = jnp.concatenate([page_refs[2 * pr][0], page_refs[2 * pr + 1][0]], axis=1)
        slab_ref[pr] = _dot_nt(sel, (xt2 + pet).astype(BF16))
    acc = jnp.zeros((pairs * rows_per_pair, KV_ROW), F32)
    for r in range(CMP_BLOCK):
        xr = jnp.concatenate([slab_ref[pr, r * rows_per_pair:(r + 1) * rows_per_pair, :] for pr in range(pairs)],
                             axis=0)
        acc = acc + _dot(xr.astype(BF16), w1_ref[r])
    hmid = _gelu_tanh(acc + b1_ref[...])
    o_ref[0] = _dot(hmid.astype(BF16), w2_ref[...])


def _compress_t(pages3d, page_ids, lw):
    n_logical = page_ids.shape[0]
    pages = min(CMP_PAGES, n_logical)
    assert n_logical % pages == 0 and pages % 2 == 0
    steps = n_logical // pages
    per_page = PAGE_SIZE // CMP_BLOCK

    def page_map(k):
        return lambda s, tab: (tab[s * pages + k], 0, 0)

    in_specs = [pl.BlockSpec((1, KV_ROW, PAGE_SIZE), page_map(k)) for k in range(pages)]
    in_specs += [
        pl.BlockSpec((KV_ROW, 2 * PAGE_SIZE), lambda s, tab: (0, 0)),
        pl.BlockSpec((2 * PAGE_SIZE, 2 * PAGE_SIZE), lambda s, tab: (0, 0)),
        pl.BlockSpec((CMP_BLOCK, KV_ROW, KV_ROW), lambda s, tab: (0, 0, 0)),
        pl.BlockSpec((1, KV_ROW), lambda s, tab: (0, 0)),
        pl.BlockSpec((KV_ROW, KV_ROW), lambda s, tab: (0, 0)),
    ]
    r, pg, c = np.meshgrid(np.arange(CMP_BLOCK), np.arange(2), np.arange(per_page), indexing="ij")
    src = (pg * PAGE_SIZE + c * CMP_BLOCK + r).reshape(-1)
    sel = jnp.asarray((src[:, None] == np.arange(2 * PAGE_SIZE)[None, :]).astype(np.float32), dtype=BF16)
    out = pl.pallas_call(
        functools.partial(_compress_t_kernel, pages=pages),
        grid_spec=pltpu.PrefetchScalarGridSpec(
            num_scalar_prefetch=1, grid=(steps,), in_specs=in_specs,
            out_specs=pl.BlockSpec((1, per_page * pages, KV_ROW), lambda s, tab: (s, 0, 0)),
            scratch_shapes=[pltpu.VMEM((pages // 2, 2 * PAGE_SIZE, KV_ROW), F32)]),
        out_shape=jax.ShapeDtypeStruct((steps, per_page * pages, KV_ROW), F32),
        compiler_params=_cparams(("arbitrary",)), name="cmp_compress_paged",
    )(page_ids, *([pages3d] * pages), lw["cmp_pe_t"], sel, lw["cmp_w1"], lw["cmp_b1"], lw["cmp_w2"])
    return out.reshape(n_logical * per_page, KV_ROW)


def _compress_kernel(tab_ref, *refs, pages):
    del tab_ref
    page_refs = refs[:pages]
    pe_ref, w1_ref, b1_ref, w2_ref, o_ref, slabk_ref, slabv_ref = refs[pages:]
    half = KV_ROW // 2
    for k in range(pages):
        slabk_ref[k * PAGE_SIZE:(k + 1) * PAGE_SIZE, :] = page_refs[k][0, :, 0:half]
        slabv_ref[k * PAGE_SIZE:(k + 1) * PAGE_SIZE, :] = page_refs[k][0, :, half:KV_ROW]
    per_page = PAGE_SIZE // CMP_BLOCK
    acc = jnp.zeros((per_page * pages, KV_ROW), F32)
    for r in range(CMP_BLOCK):
        rows = [jnp.concatenate([slab[pl.ds(c * CMP_BLOCK + r, pages, stride=PAGE_SIZE), :]
                                 for slab in (slabk_ref, slabv_ref)], axis=1) for c in range(per_page)]
        xr = (jnp.concatenate(rows, axis=0) + pe_ref[r]).astype(BF16)
        acc = acc + _dot(xr, w1_ref[r])
    hmid = _gelu_tanh(acc + b1_ref[...])
    o_ref[0] = _dot(hmid.astype(BF16), w2_ref[...])


def _compress(rows3d, page_ids, lw):
    n_logical = page_ids.shape[0]
    pages = min(CMP_PAGES, n_logical)
    assert n_logical % pages == 0
    steps = n_logical // pages
    per_page = PAGE_SIZE // CMP_BLOCK

    def page_map(k):
        return lambda s, tab: (tab[s * pages + k], 0, 0)

    in_specs = [pl.BlockSpec((1, PAGE_SIZE, KV_ROW), page_map(k)) for k in range(pages)]
    in_specs += [
        pl.BlockSpec((CMP_BLOCK, 1, KV_ROW), lambda s, tab: (0, 0, 0)),
        pl.BlockSpec((CMP_BLOCK, KV_ROW, KV_ROW), lambda s, tab: (0, 0, 0)),
        pl.BlockSpec((1, KV_ROW), lambda s, tab: (0, 0)),
        pl.BlockSpec((KV_ROW, KV_ROW), lambda s, tab: (0, 0)),
    ]
    out = pl.pallas_call(
        functools.partial(_compress_kernel, pages=pages),
        grid_spec=pltpu.PrefetchScalarGridSpec(
            num_scalar_prefetch=1, grid=(steps,), in_specs=in_specs,
            out_specs=pl.BlockSpec((1, per_page * pages, KV_ROW), lambda s, tab: (s, 0, 0)),
            scratch_shapes=[pltpu.VMEM((pages * PAGE_SIZE, KV_ROW // 2), F32),
                            pltpu.VMEM((pages * PAGE_SIZE, KV_ROW // 2), F32)]),
        out_shape=jax.ShapeDtypeStruct((steps, per_page * pages, KV_ROW), F32),
        compiler_params=_cparams(("arbitrary",)), name="cmp_compress",
    )(page_ids, *([rows3d] * pages), lw["cmp_pe"], lw["cmp_w1"], lw["cmp_b1"], lw["cmp_w2"])
    out = out.reshape(steps, per_page, pages, KV_ROW).transpose(0, 2, 1, 3)
    return out.reshape(n_logical * per_page, KV_ROW)


def _split_compressed(tok, nb):
    n = tok.shape[0] // nb
    t = tok.reshape(nb, n // 2, 2, 2, NSA_KV_HEADS, NSA_DIM)
    t = t.transpose(3, 0, 4, 2, 1, 5).reshape(2, nb, NSA_KV_HEADS, n, NSA_DIM).astype(BF16)
    return t[0], t[1]


def _cmp_prompt_kernel(q_ref, kc_ref, vct_ref, bias_ref, gate_ref, o_ref, sel_ref, *, tq, n_cmp):
    qi = pl.program_id(2)
    kc = kc_ref[0, 0]
    vct = vct_ref[0, 0]
    maskf = (bias_ref[0] > 0.5 * NEG_BIG).astype(F32)
    gates_t = gate_ref[0].T
    zs = [_dot_nt(kc, q_ref[0, hh]) + bias_ref[hh] for hh in range(NSA_GROUP)]
    es = [jnp.exp(z - jnp.max(z, axis=0, keepdims=True)) * maskf for z in zs]
    ps = [e / jnp.maximum(jnp.sum(e, axis=0, keepdims=True), 1e-30) for e in es]
    outs = [_dot(vct, ps[hh].astype(BF16)) * gates_t[hh:hh + 1, :] for hh in range(NSA_GROUP)]
    imp = ps[0]
    for p in ps[1:]:
        imp = imp + p
    o_ref[0] = jnp.concatenate(outs, axis=0).T
    n_slc = n_cmp // 2
    imp_slc = imp[0:n_slc] + imp[n_slc:n_cmp]
    t = qi * tq + lax.broadcasted_iota(jnp.int32, (n_slc, tq), 1)
    j = lax.broadcasted_iota(jnp.int32, (n_slc, tq), 0)
    cur = t // SLC_BLOCK
    forced = (j == 0) | (j == cur) | (j == cur - 1)
    score = jnp.where(forced, jnp.inf, jnp.where(j <= cur, imp_slc, -jnp.inf))
    rank = jnp.zeros((n_slc, tq), F32)
    for i in range(n_slc):
        ci = score[i:i + 1, :]
        rank = rank + jnp.where(j > i, jnp.where(ci >= score, 1.0, 0.0), jnp.where(ci > score, 1.0, 0.0))
    sel_t = jnp.where(rank < float(min(N_SELECT, n_slc)), 1.0, 0.0)
    sel_pad = jnp.concatenate([sel_t, jnp.zeros((SEL_LANES - n_slc, tq), F32)], axis=0)
    sel_ref[0, 0] = sel_pad.T


def _cmp_prompt(q, kc, vct, bias_t, gates, nb, nt):
    tq = ATT_TILE
    n_cmp = kc.shape[2]
    g = NSA_KV_HEADS
    gw = NSA_GROUP * NSA_DIM
    return pl.pallas_call(
        functools.partial(_cmp_prompt_kernel, tq=tq, n_cmp=n_cmp),
        grid=(nb, g, nt // tq),
        in_specs=[pl.BlockSpec((1, NSA_GROUP, tq, NSA_DIM), lambda b, gi, i: (b, gi, i, 0)),
                  pl.BlockSpec((1, 1, n_cmp, NSA_DIM), lambda b, gi, i: (b, gi, 0, 0)),
                  pl.BlockSpec((1, 1, NSA_DIM, n_cmp), lambda b, gi, i: (b, gi, 0, 0)),
                  pl.BlockSpec((NSA_GROUP, n_cmp, tq), lambda b, gi, i: (gi, 0, i)),
                  pl.BlockSpec((1, tq, GATE_LANES), lambda b, gi, i: (b, i, gi))],
        out_specs=(pl.BlockSpec((1, tq, gw), lambda b, gi, i: (b, i, gi)),
                   pl.BlockSpec((1, 1, tq, SEL_LANES), lambda b, gi, i: (b, gi, i, 0))),
        out_shape=(jax.ShapeDtypeStruct((nb, nt, D_MODEL), F32),
                   jax.ShapeDtypeStruct((nb, g, nt, SEL_LANES), F32)),
        compiler_params=_cparams(("arbitrary", "arbitrary", "arbitrary")), name="cmp_attn_select",
    )(q, kc, vct, bias_t, gates.reshape(nb, nt, g * GATE_LANES))


def _flash_update(s, vs, m_ref, l_ref, acc_ref):
    tk = s.shape[-1]
    m_old = m_ref[...]
    m_new = jnp.maximum(m_old, jnp.max(s, axis=-1, keepdims=True))
    alpha = jnp.exp(m_old - m_new)
    p = jnp.exp(s - jnp.concatenate([m_new] * (tk // 128), axis=1))
    psum = p[:, 0:128]
    for c in range(1, tk // 128):
        psum = psum + p[:, c * 128:(c + 1) * 128]
    l_ref[...] = alpha * l_ref[...] + psum
    dv = acc_ref.shape[-1]
    acc_ref[...] = jnp.concatenate([alpha] * (dv // 128), axis=1) * acc_ref[...] + _dot(p.astype(BF16), vs)
    m_ref[...] = m_new


def _nsa_flash_kernel(*refs, tq, slc):
    if slc:
        q_ref, k_ref, ve_ref, vo_ref, bias_ref, gate_ref, ge_ref, sel_ref, exp_ref, o_ref, m_ref, acc_ref = refs
    else:
        q_ref, k_ref, ve_ref, vo_ref, bias_ref, gate_ref, ge_ref, o_ref, m_ref, acc_ref = refs
    qi = pl.program_id(2)
    ch = FLASH_ROWS
    nsub = tq // ch
    m_ref[...] = jnp.full(m_ref.shape, NEG_BIG, F32)
    acc_ref[...] = jnp.zeros(acc_ref.shape, F32)
    if slc:
        sel = sel_ref[0, 0].astype(BF16)

    def tile(kt, btype):
        start = pl.multiple_of(kt * tq, tq)
        kt_tile = k_ref[0, 0, :, pl.ds(start, tq)]
        vs = (ve_ref[0, 0, pl.ds(start, tq), :], vo_ref[0, 0, pl.ds(start, tq), :])
        if slc:
            maskadd = (_dot(sel, exp_ref[kt]) - 1.0) * (-NEG_BIG)
        for c in range(NSA_GROUP * nsub):
            hh, qs = divmod(c, nsub)
            qrows = slice(qs * ch, (qs + 1) * ch)
            srows = slice(c * ch, (c + 1) * ch)
            s = _dot(q_ref[0, hh, qrows, :], kt_tile)
            if btype is not None:
                s = s + bias_ref[hh, btype, qrows, :]
            if slc:
                s = s + maskadd[qrows]
            m_old = m_ref[srows]
            m_new = jnp.maximum(m_old, jnp.max(s, axis=-1, keepdims=True))
            alpha = jnp.exp(m_old - m_new)
            p = jnp.exp(s - jnp.concatenate([m_new] * (tq // 128), axis=1))
            acc_ref[srows] = alpha * acc_ref[srows] + _dot(p.astype(BF16), vs[hh % 2])
            m_ref[srows] = m_new

    if slc:
        def far(kt, carry):
            tile(kt, None)
            return carry
        lax.fori_loop(0, jnp.maximum(qi - 1, 0), far, 0)
    else:
        @pl.when(qi >= 2)
        def _():
            tile(qi - 2, 2)

    @pl.when(qi >= 1)
    def _():
        tile(qi - 1, 1)

    tile(qi, 0)
    gexp = _expand_gates(gate_ref[0], ge_ref[...])
    lane = lax.broadcasted_iota(jnp.int32, (tq, 2 * NSA_DIM), 1)
    for j in range(NSA_GROUP // 2):
        a_e = acc_ref[(2 * j) * tq:(2 * j + 1) * tq]
        a_o = acc_ref[(2 * j + 1) * tq:(2 * j + 2) * tq]
        num = jnp.where(lane < NSA_DIM, a_e, a_o)
        den = pltpu.roll(jnp.where(lane < NSA_DIM, a_o, a_e), NSA_DIM, 1)
        cols = slice(j * 2 * NSA_DIM, (j + 1) * 2 * NSA_DIM)
        o_ref[0, :, cols] = num / den * gexp[:, cols]


def _expand_gates(g, expand):
    g1 = g.astype(BF16)
    r1 = g - g1.astype(F32)
    g2 = r1.astype(BF16)
    g3 = (r1 - g2.astype(F32)).astype(BF16)
    return _dot(g1, expand) + _dot(g2, expand) + _dot(g3, expand)


def _gate_expand_matrix(branch):
    k = np.arange(GATE_LANES)[:, None]
    n = np.arange(NSA_GROUP * NSA_DIM)[None, :]
    return jnp.asarray((k == branch * NSA_GROUP + n // NSA_DIM).astype(np.float32), dtype=BF16)


def _nsa_flash(q, kt, ve, vo, bias_tiles, gates, nb, nt, sel=None, expand=None):
    tq = ATT_TILE
    g = NSA_KV_HEADS
    gw = NSA_GROUP * NSA_DIM
    slc = sel is not None
    v_spec = pl.BlockSpec((1, 1, nt, 2 * NSA_DIM), lambda b, gi, i: (b, gi, 0, 0))
    in_specs = [pl.BlockSpec((1, NSA_GROUP, tq, NSA_DIM), lambda b, gi, i: (b, gi, i, 0)),
                pl.BlockSpec((1, 1, NSA_DIM, nt), lambda b, gi, i: (b, gi, 0, 0)),
                v_spec, v_spec,
                pl.BlockSpec((NSA_GROUP, 3, tq, tq), lambda b, gi, i: (gi, 0, 0, 0)),
                pl.BlockSpec((1, tq, GATE_LANES), lambda b, gi, i: (b, i, gi)),
                pl.BlockSpec((GATE_LANES, gw), lambda b, gi, i: (0, 0))]
    args = [q, kt, ve, vo, bias_tiles, gates.reshape(nb, nt, g * GATE_LANES), _gate_expand_matrix(1 if slc else 2)]
    if slc:
        in_specs += [pl.BlockSpec((1, 1, tq, SEL_LANES), lambda b, gi, i: (b, gi, i, 0)),
                     pl.BlockSpec((nt // tq, SEL_LANES, tq), lambda b, gi, i: (0, 0, 0))]
        args += [sel, expand]
    return pl.pallas_call(
        functools.partial(_nsa_flash_kernel, tq=tq, slc=slc),
        grid=(nb, g, nt // tq),
        in_specs=in_specs,
        out_specs=pl.BlockSpec((1, tq, gw), lambda b, gi, i: (b, i, gi)),
        out_shape=jax.ShapeDtypeStruct((nb, nt, D_MODEL), F32),
        scratch_shapes=[pltpu.VMEM((NSA_GROUP * tq, 128), F32), pltpu.VMEM((NSA_GROUP * tq, 2 * NSA_DIM), F32)],
        compiler_params=_cparams(("arbitrary", "arbitrary", "arbitrary")),
        name="slc_attn" if slc else "win_attn",
    )(*args)


def _mla_prompt_kernel(q_ref, kn_ref, latv_ref, o_ref, m_ref, l_ref, acc_ref, *, tq):
    qi = pl.program_id(1)
    rows = MLA_HEADS * tq
    m_ref[...] = jnp.full(m_ref.shape, NEG_BIG, F32)
    l_ref[...] = jnp.zeros(l_ref.shape, F32)
    acc_ref[...] = jnp.zeros(acc_ref.shape, F32)

    heads_per_chunk = MLA_FLASH_ROWS // tq
    n_chunks = MLA_HEADS // heads_per_chunk

    def tile(kt, diag):
        start = pl.multiple_of(kt * tq, tq)
        vs = latv_ref[0, pl.ds(start, tq), :]

        def scores(c):
            heads = range(c * heads_per_chunk, (c + 1) * heads_per_chunk)
            s = jnp.concatenate([_dot(q_ref[0, hd], kn_ref[0, hd, :, pl.ds(start, tq)]) for hd in heads], axis=0)
            if diag:
                row = lax.broadcasted_iota(jnp.int32, (tq, tq), 0)
                col = lax.broadcasted_iota(jnp.int32, (tq, tq), 1)
                s = jnp.where((col <= row)[None], s.reshape(heads_per_chunk, tq, tq), NEG_BIG)
                s = s.reshape(MLA_FLASH_ROWS, tq)
            return s

        s_next = scores(0)
        for c in range(n_chunks):
            s_cur = s_next
            if c + 1 < n_chunks:
                s_next = scores(c + 1)
            srows = slice(c * MLA_FLASH_ROWS, (c + 1) * MLA_FLASH_ROWS)
            _flash_update(s_cur, vs, m_ref.at[srows], l_ref.at[srows], acc_ref.at[srows])

    def body(kt, carry):
        tile(kt, False)
        return carry
    lax.fori_loop(0, qi, body, 0)
    tile(qi, True)
    o = acc_ref[...] / jnp.sum(l_ref[...], axis=-1, keepdims=True)
    o_ref[0] = o.reshape(MLA_HEADS, tq, KV_LORA).astype(BF16)


def _mla_prompt(qmla, latt, latv, nb, nt):
    tq = ATT_TILE
    return pl.pallas_call(
        functools.partial(_mla_prompt_kernel, tq=tq),
        grid=(nb, nt // tq),
        in_specs=[pl.BlockSpec((1, MLA_HEADS, tq, MLA_QK), lambda b, i: (b, 0, i, 0)),
                  pl.BlockSpec((1, MLA_HEADS, MLA_QK, nt), lambda b, i: (b, 0, 0, 0)),
                  pl.BlockSpec((1, nt, KV_LORA), lambda b, i: (b, 0, 0))],
        out_specs=pl.BlockSpec((1, MLA_HEADS, tq, KV_LORA), lambda b, i: (b, 0, i, 0)),
        out_shape=jax.ShapeDtypeStruct((nb, MLA_HEADS, nt, KV_LORA), BF16),
        scratch_shapes=[pltpu.VMEM((MLA_HEADS * tq, 128), F32), pltpu.VMEM((MLA_HEADS * tq, 128), F32),
                        pltpu.VMEM((MLA_HEADS * tq, KV_LORA), F32)],
        compiler_params=_cparams(("arbitrary", "arbitrary")), name="mla_attn",
    )(qmla, latt, latv.reshape(nb, nt, KV_LORA))


def _merge_kernel(oc_ref, os_ref, ow_ref, ol_ref, gm_ref, x_ref, mod_ref, wuv_ref, wo_ref, o_ref):
    o_nsa = oc_ref[...] + os_ref[...] + ow_ref[...]
    o_mla = jnp.concatenate([_dot(ol_ref[0, hd], wuv_ref[hd]) for hd in range(MLA_HEADS)], axis=-1)
    gm = gm_ref[...]
    merged = (gm[:, 0:D_MODEL] * o_nsa + gm[:, D_MODEL:] * o_mla).astype(BF16)
    o_ref[...] = x_ref[...] + mod_ref[:, 2, :] * _dot(merged, wo_ref[...])


def _merge(o_cmp, o_slc, o_win, o_lat, gm, x, mod, per_row_mod, lw, nb, nt, tm):
    m = nb * nt
    tpb = nt // tm
    d = D_MODEL
    row = lambda i: (i, 0)
    if per_row_mod:
        mod_spec = pl.BlockSpec((tm, 6, d), lambda i: (i, 0, 0))
    else:
        mod_spec = pl.BlockSpec((1, 6, d), lambda i: (i // tpb, 0, 0))
    return pl.pallas_call(
        _merge_kernel, grid=(m // tm,),
        in_specs=[pl.BlockSpec((tm, d), row), pl.BlockSpec((tm, d), row), pl.BlockSpec((tm, d), row),
                  pl.BlockSpec((1, MLA_HEADS, tm, KV_LORA), lambda i: (i // tpb, 0, i % tpb, 0)),
                  pl.BlockSpec((tm, 2 * d), row), pl.BlockSpec((tm, d), row), mod_spec,
                  pl.BlockSpec((MLA_HEADS, KV_LORA, MLA_V), lambda i: (0, 0, 0)),
                  pl.BlockSpec((d, d), lambda i: (0, 0))],
        out_specs=pl.BlockSpec((tm, d), row),
        out_shape=jax.ShapeDtypeStruct((m, d), F32),
        compiler_params=_cparams(("arbitrary",)), name="merge_out_proj",
    )(o_cmp, o_slc, o_win, o_lat, gm, x, mod, lw["w_uv"], lw["w_out"])


def _ffn_kernel(x_ref, mod_ref, gn_ref, w1_ref, w3_ref, w2_ref, gf_ref, o_ref, h_ref, acc_ref, *, final):
    f = pl.program_id(1)

    @pl.when(f == 0)
    def _():
        y = _rms(x_ref[...], gn_ref[...])
        h_ref[...] = (y * (1.0 + mod_ref[:, 4, :]) + mod_ref[:, 3, :]).astype(BF16)
        acc_ref[...] = jnp.zeros(acc_ref.shape, F32)

    h = h_ref[...]
    a = _dot(h, w1_ref[...])
    b = _dot(h, w3_ref[...])
    act = (a * jax.nn.sigmoid(a) * b).astype(BF16)
    acc_ref[...] += _dot(act, w2_ref[...])

    @pl.when(f == pl.num_programs(1) - 1)
    def _():
        y = x_ref[...] + mod_ref[:, 5, :] * acc_ref[...]
        if final:
            y = _rms(y, gf_ref[...])
        o_ref[...] = y


def _ffn(x, mod, per_row_mod, lw, gfinal, final, nb, nt, tm):
    m = nb * nt
    tpb = nt // tm
    d = D_MODEL
    dff = lw["ffn_w2"].shape[0]
    tf = dff // 2
    nf = dff // tf
    row = lambda i, f: (i, 0)
    if per_row_mod:
        mod_spec = pl.BlockSpec((tm, 6, d), lambda i, f: (i, 0, 0))
    else:
        mod_spec = pl.BlockSpec((1, 6, d), lambda i, f: (i // tpb, 0, 0))
    return pl.pallas_call(
        functools.partial(_ffn_kernel, final=final), grid=(m // tm, nf),
        in_specs=[pl.BlockSpec((tm, d), row), mod_spec, pl.BlockSpec((1, d), lambda i, f: (0, 0)),
                  pl.BlockSpec((d, tf), lambda i, f: (0, f)),
                  pl.BlockSpec((d, tf), lambda i, f: (0, nf + f)),
                  pl.BlockSpec((tf, d), lambda i, f: (f, 0)),
                  pl.BlockSpec((1, d), lambda i, f: (0, 0))],
        out_specs=pl.BlockSpec((tm, d), row),
        out_shape=jax.ShapeDtypeStruct((m, d), F32),
        scratch_shapes=[pltpu.VMEM((tm, d), BF16), pltpu.VMEM((tm, d), F32)],
        compiler_params=_cparams(("arbitrary", "arbitrary")), name="ffn",
    )(x, mod, lw["norm_ffn_g"], lw["ffn_w13"], lw["ffn_w13"], lw["ffn_w2"], gfinal)


def _cmp_sample_kernel(q_ref, tok_ref, bias_ref, gate_ref, o_ref, imp_ref, *, nt, n_cmp):
    rows = nt * NSA_GROUP
    n_past = n_cmp // 2
    half = NSA_KV_HEADS * NSA_DIM
    tok = tok_ref[0]
    for g in range(NSA_KV_HEADS):
        qg = q_ref[0, :, g].reshape(rows, NSA_DIM)
        kc = tok[:, g * NSA_DIM:(g + 1) * NSA_DIM].astype(BF16)
        vc = tok[:, half + g * NSA_DIM:half + (g + 1) * NSA_DIM].astype(BF16)
        z = _dot_nt(qg, kc) + bias_ref[g]
        maskf = (bias_ref[g] > 0.5 * NEG_BIG).astype(F32)
        e = jnp.exp(z - jnp.max(z, axis=-1, keepdims=True)) * maskf
        p = e / jnp.maximum(jnp.sum(e, axis=-1, keepdims=True), 1e-30)
        o = _dot(p.astype(BF16), vc) * gate_ref[0, g, 0]
        o_ref[0, :, g] = o.reshape(nt, NSA_GROUP, NSA_DIM)
        imp = jnp.sum(p.reshape(nt, NSA_GROUP, n_cmp), axis=1)
        imp_ref[0, g] = imp[:, 0:n_past] + imp[:, n_past:n_cmp]


def _topk_sample_kernel(imp_ref, idx_ref, *, n_pick):
    imp = imp_ref[...]
    rows, n_past = imp.shape
    lane = lax.broadcasted_iota(jnp.int32, (rows, n_past), 1)
    score = jnp.where((lane == 0) | (lane == n_past - 1), jnp.inf, imp)

    def body(i, rank):
        ci = jnp.sum(jnp.where(lane == i, score, 0.0), axis=-1, keepdims=True)
        beats = (ci > score) | ((ci == score) & (i < lane))
        return rank + jnp.where(beats, 1, 0)
    rank = lax.fori_loop(0, n_past, body, jnp.zeros((rows, n_past), jnp.int32))
    out_lane = lax.broadcasted_iota(jnp.int32, (rows, 128), 1)
    picked = jnp.zeros((rows, 128), jnp.int32)
    for r in range(n_pick):
        ir = jnp.sum(jnp.where(rank == r, lane, 0), axis=-1, keepdims=True)
        picked = jnp.where(out_lane == r, ir, picked)
    idx_ref[...] = picked


def _topk_sample(imp2d, n_pick):
    rows, n_past = imp2d.shape
    return pl.pallas_call(
        functools.partial(_topk_sample_kernel, n_pick=n_pick),
        grid=(1,),
        in_specs=[pl.BlockSpec((rows, n_past), lambda i: (0, 0))],
        out_specs=pl.BlockSpec((rows, 128), lambda i: (0, 0)),
        out_shape=jax.ShapeDtypeStruct((rows, 128), jnp.int32),
        compiler_params=_cparams(("arbitrary",)), name="topk_blocks_sample",
    )(imp2d)


def _cmp_sample(q5, tok, bias, gate_cols, nb, nt):
    n_cmp = tok.shape[1]
    g = NSA_KV_HEADS
    rows = nt * NSA_GROUP
    return pl.pallas_call(
        functools.partial(_cmp_sample_kernel, nt=nt, n_cmp=n_cmp),
        grid=(nb,),
        in_specs=[pl.BlockSpec((1, nt, g, NSA_GROUP, NSA_DIM), lambda b: (b, 0, 0, 0, 0)),
                  pl.BlockSpec((1, n_cmp, KV_ROW), lambda b: (b, 0, 0)),
                  pl.BlockSpec((g, rows, n_cmp), lambda b: (0, 0, 0)),
                  pl.BlockSpec((1, g, 1, rows, 1), lambda b: (b, 0, 0, 0, 0))],
        out_specs=(pl.BlockSpec((1, nt, g, NSA_GROUP, NSA_DIM), lambda b: (b, 0, 0, 0, 0)),
                   pl.BlockSpec((1, g, nt, n_cmp // 2), lambda b: (b, 0, 0, 0))),
        out_shape=(jax.ShapeDtypeStruct((nb, nt, g, NSA_GROUP, NSA_DIM), F32),
                   jax.ShapeDtypeStruct((nb, g, nt, n_cmp // 2), F32)),
        compiler_params=_cparams(("arbitrary",)), name="cmp_attn_sample",
    )(q5, tok, bias, gate_cols)


def _slc_sample_kernel(idx_ref, pt_ref, *refs, nt, n_pick, n_past, n_pages):
    del pt_ref, n_pages
    nblk = NSA_KV_HEADS * n_pick
    blk_refs = refs[:nblk]
    q_ref, new_ref, bias_ref, biasc_ref, gate_ref, o_ref = refs[nblk:]
    b = pl.program_id(0)
    t = pl.program_id(1)
    half = NSA_KV_HEADS * NSA_DIM
    lane_half = lax.broadcasted_iota(jnp.int32, (NSA_GROUP, PAGE_SIZE), 1) // SLC_BLOCK
    new = new_ref[0]
    groups = range(NSA_KV_HEADS)
    s_all, sn_all, vt_all = [], [], []
    for g in groups:
        qg = q_ref[0, 0, g]
        kt_list, vt_list, bias_list = [], [], []
        for n in range(n_pick):
            page = blk_refs[g * n_pick + n]
            j = idx_ref[((b * nt + t) * NSA_KV_HEADS + g) * n_pick + n]
            near = jnp.clip(j - (n_past - 3), 0, 2)
            kt_list.append(page[0, g * NSA_DIM:(g + 1) * NSA_DIM, :].astype(BF16))
            vt_list.append(page[0, half + g * NSA_DIM:half + (g + 1) * NSA_DIM, :].astype(BF16))
            bias_list.append(jnp.where(lane_half == (j & 1), bias_ref[0, g, near], NEG_BIG))
        s_all.append(_dot(qg, jnp.concatenate(kt_list, axis=1)) + jnp.concatenate(bias_list, axis=1))
        sn_all.append(_dot_nt(qg, new[:, g * NSA_DIM:(g + 1) * NSA_DIM].astype(BF16)) + biasc_ref[0, g])
        vt_all.append(jnp.concatenate(vt_list, axis=1))
    m_all = [jnp.maximum(s_all[g].max(axis=-1, keepdims=True), sn_all[g].max(axis=-1, keepdims=True)) for g in groups]
    p_all = [jnp.exp(s_all[g] - m_all[g]) for g in groups]
    pn_all = [jnp.exp(sn_all[g] - m_all[g]) for g in groups]
    for g in groups:
        l = jnp.sum(p_all[g], axis=-1, keepdims=True) + jnp.sum(pn_all[g], axis=-1, keepdims=True)
        acc = _dot_nt(p_all[g].astype(BF16), vt_all[g])
        acc = acc + _dot(pn_all[g].astype(BF16), new[:, half + g * NSA_DIM:half + (g + 1) * NSA_DIM].astype(BF16))
        o_ref[0, 0, g] = acc / l * gate_ref[0, g, 0, 0]


def _slc_sample(idx_flat, pt_flat, cache_blocks, q5, new_rows, bias_near, bias_cur, gate_cols, nb, nt, n_pick,
                n_past, n_pages):
    g = NSA_KV_HEADS

    def blk_map(gi, n):
        def f(b, t, idx, pt):
            j = idx[((b * nt + t) * g + gi) * n_pick + n]
            return (pt[b * n_pages + (j >> 1)], 0, 0)
        return f

    in_specs = [pl.BlockSpec((1, KV_ROW, PAGE_SIZE), blk_map(gi, n)) for gi in range(g) for n in range(n_pick)]
    in_specs += [
        pl.BlockSpec((1, 1, g, NSA_GROUP, NSA_DIM), lambda b, t, idx, pt: (b, t, 0, 0, 0)),
        pl.BlockSpec((1, 8, KV_ROW), lambda b, t, idx, pt: (b, 0, 0)),
        pl.BlockSpec((1, g, 3, NSA_GROUP, PAGE_SIZE), lambda b, t, idx, pt: (t, 0, 0, 0, 0)),
        pl.BlockSpec((1, g, NSA_GROUP, 8), lambda b, t, idx, pt: (t, 0, 0, 0)),
        pl.BlockSpec((1, g, 1, 1, NSA_GROUP, 1), lambda b, t, idx, pt: (b, 0, 0, t, 0, 0)),
    ]
    return pl.pallas_call(
        functools.partial(_slc_sample_kernel, nt=nt, n_pick=n_pick, n_past=n_past, n_pages=n_pages),
        grid_spec=pltpu.PrefetchScalarGridSpec(
            num_scalar_prefetch=2, grid=(nb, nt), in_specs=in_specs,
            out_specs=pl.BlockSpec((1, 1, g, NSA_GROUP, NSA_DIM), lambda b, t, idx, pt: (b, t, 0, 0, 0))),
        out_shape=jax.ShapeDtypeStruct((nb, nt, g, NSA_GROUP, NSA_DIM), F32),
        compiler_params=_cparams(("arbitrary", "arbitrary")), name="slc_attn_sample",
    )(idx_flat, pt_flat, *([cache_blocks] * (g * n_pick)), q5, new_rows, bias_near, bias_cur, gate_cols)


def _win_sample_kernel(q_ref, buf_ref, new_ref, bias_ref, biasn_ref, gate_ref, o_ref, *, nt):
    rows = nt * NSA_GROUP
    half = NSA_KV_HEADS * NSA_DIM
    buf = buf_ref[0, 0]
    new = new_ref[0]
    for g in range(NSA_KV_HEADS):
        qg = q_ref[0, :, g].reshape(rows, NSA_DIM)
        s1 = _dot(qg, buf[g * NSA_DIM:(g + 1) * NSA_DIM, :].astype(BF16)) + bias_ref[g]
        s2 = _dot_nt(qg, new[:, g * NSA_DIM:(g + 1) * NSA_DIM].astype(BF16)) + biasn_ref[g]
        m = jnp.maximum(s1.max(axis=-1, keepdims=True), s2.max(axis=-1, keepdims=True))
        p1 = jnp.exp(s1 - m)
        p2 = jnp.exp(s2 - m)
        l = jnp.sum(p1, axis=-1, keepdims=True) + jnp.sum(p2, axis=-1, keepdims=True)
        acc = _dot_nt(p1.astype(BF16), buf[half + g * NSA_DIM:half + (g + 1) * NSA_DIM, :].astype(BF16))
        acc = acc + _dot(p2.astype(BF16), new[:, half + g * NSA_DIM:half + (g + 1) * NSA_DIM].astype(BF16))
        o_ref[0, :, g] = (acc / l * gate_ref[0, g, 0]).reshape(nt, NSA_GROUP, NSA_DIM)


def _win_sample(q5, win_state, layer, new_rows, bias_buf, bias_new, gate_cols, nb, nt):
    g = NSA_KV_HEADS
    rows = nt * NSA_GROUP
    wlen = win_state.shape[3]
    return pl.pallas_call(
        functools.partial(_win_sample_kernel, nt=nt),
        grid=(nb,),
        in_specs=[pl.BlockSpec((1, nt, g, NSA_GROUP, NSA_DIM), lambda b: (b, 0, 0, 0, 0)),
                  pl.BlockSpec((1, 1, KV_ROW, wlen), lambda b: (layer, b, 0, 0)),
                  pl.BlockSpec((1, 8, KV_ROW), lambda b: (b, 0, 0)),
                  pl.BlockSpec((g, rows, wlen), lambda b: (0, 0, 0)),
                  pl.BlockSpec((g, rows, 8), lambda b: (0, 0, 0)),
                  pl.BlockSpec((1, g, 1, rows, 1), lambda b: (b, 0, 2, 0, 0))],
        out_specs=pl.BlockSpec((1, nt, g, NSA_GROUP, NSA_DIM), lambda b: (b, 0, 0, 0, 0)),
        out_shape=jax.ShapeDtypeStruct((nb, nt, g, NSA_GROUP, NSA_DIM), F32),
        compiler_params=_cparams(("arbitrary",)), name="win_attn_sample",
    )(q5, win_state, new_rows, bias_buf, bias_new, gate_cols)


def _mla_sample_kernel(pt_ref, *refs, pages, nt):
    del pt_ref
    page_refs = refs[:pages]
    q_ref, new_ref, o_ref, m_ref, l_ref, acc_ref = refs[pages:]
    step = pl.program_id(1)
    rows = MLA_HEADS * nt

    @pl.when(step == 0)
    def _():
        m_ref[...] = jnp.full(m_ref.shape, NEG_BIG, F32)
        l_ref[...] = jnp.zeros(l_ref.shape, F32)
        acc_ref[...] = jnp.zeros(acc_ref.shape, F32)

    q = q_ref[0]
    kt = jnp.concatenate([page_refs[k][0].astype(BF16) for k in range(pages)], axis=1)
    s = _dot(q, kt)
    m_old = m_ref[...]
    m_new = jnp.maximum(m_old, s.max(axis=-1, keepdims=True))
    alpha = jnp.exp(m_old - m_new)
    p = jnp.exp(s - m_new)
    m_ref[...] = m_new
    l_ref[...] = alpha * l_ref[...] + jnp.sum(p, axis=-1, keepdims=True)
    acc_ref[...] = alpha * acc_ref[...] + _dot_nt(p.astype(BF16), kt[0:KV_LORA, :])

    @pl.when(step == pl.num_programs(1) - 1)
    def _():
        new = new_ref[0].astype(BF16)
        tq = lax.broadcasted_iota(jnp.int32, (rows, 8), 0) % nt
        tk = lax.broadcasted_iota(jnp.int32, (rows, 8), 1)
        s = jnp.where(tk <= tq, _dot_nt(q, new), NEG_BIG)
        m_o = m_ref[...]
        m_n = jnp.maximum(m_o, s.max(axis=-1, keepdims=True))
        a = jnp.exp(m_o - m_n)
        p = jnp.exp(s - m_n)
        lf = a * l_ref[...] + jnp.sum(p, axis=-1, keepdims=True)
        accf = a * acc_ref[...] + _dot(p.astype(BF16), new[:, 0:KV_LORA])
        o_ref[0] = (accf / lf).astype(BF16)


def _mla_sample(pt_flat, cache_pages, q_rows, new_rows, nb, nt, n_pages):
    pages = MLA_PAGES
    steps = n_pages // pages
    rows = MLA_HEADS * nt

    def page_map(k):
        return lambda b, s, pt: (pt[b * n_pages + s * pages + k], 0, 0)

    in_specs = [pl.BlockSpec((1, LAT, PAGE_SIZE), page_map(k)) for k in range(pages)]
    in_specs += [pl.BlockSpec((1, rows, LAT), lambda b, s, pt: (b, 0, 0)),
                 pl.BlockSpec((1, 8, LAT), lambda b, s, pt: (b, 0, 0))]
    return pl.pallas_call(
        functools.partial(_mla_sample_kernel, pages=pages, nt=nt),
        grid_spec=pltpu.PrefetchScalarGridSpec(
            num_scalar_prefetch=1, grid=(nb, steps), in_specs=in_specs,
            out_specs=pl.BlockSpec((1, rows, KV_LORA), lambda b, s, pt: (b, 0, 0)),
            scratch_shapes=[pltpu.VMEM((rows, 1), F32), pltpu.VMEM((rows, 1), F32),
                            pltpu.VMEM((rows, KV_LORA), F32)]),
        out_shape=jax.ShapeDtypeStruct((nb, rows, KV_LORA), BF16),
        compiler_params=_cparams(("arbitrary", "arbitrary")), name="mla_attn_sample",
    )(pt_flat, *([cache_pages] * pages), q_rows, new_rows)


def _pack_layer(l, w):
    d = D_MODEL
    w_in = w["w_in"][l]
    sizes = (NSA_HEADS * NSA_DIM, KV_ROW, KV_ROW, KV_ROW, 3 * NSA_HEADS, Q_LORA, KV_LORA, MLA_ROPE, 2 * d)
    offs = np.concatenate([[0], np.cumsum(sizes)])
    seg = [w_in[:, offs[i]:offs[i + 1]] for i in range(len(sizes))]
    gsrc = seg[4].reshape(d, NSA_KV_HEADS, NSA_GROUP, 3).transpose(0, 1, 3, 2).reshape(d, NSA_KV_HEADS, 3 * NSA_GROUP)
    gates = jnp.pad(gsrc, ((0, 0), (0, 0), (0, GATE_LANES - 3 * NSA_GROUP))).reshape(d, NSA_KV_HEADS * GATE_LANES)
    kr = jnp.pad(seg[7], ((0, 0), (0, 128 - MLA_ROPE)))
    w_packed = jnp.concatenate([seg[0], seg[1], seg[2], seg[3], gates, seg[5], seg[6], kr, seg[8]], axis=1)
    wuq = w["mla_w_uq"][l].reshape(Q_LORA, MLA_HEADS, MLA_NOPE + MLA_ROPE)
    wuq = jnp.concatenate([wuq[:, :, :MLA_NOPE].reshape(Q_LORA, -1), wuq[:, :, MLA_NOPE:].reshape(Q_LORA, -1)], axis=1)
    w1 = w["nsa_cmp_w1"][l].reshape(2, CMP_BLOCK, NSA_DIM, NSA_DIM)
    def diag_blocks(blocks):
        rows = []
        for i in range(2):
            for g in range(NSA_KV_HEADS):
                off = (i * NSA_KV_HEADS + g) * NSA_DIM
                pad = [(0, 0)] * (blocks[i].ndim - 1) + [(off, KV_ROW - NSA_DIM - off)]
                rows.append(jnp.pad(blocks[i], pad))
        return jnp.concatenate(rows, axis=-2)
    w1big = diag_blocks([w1[0], w1[1]])
    w2big = diag_blocks([w["nsa_cmp_w2"][l][0], w["nsa_cmp_w2"][l][1]])
    pe = w["nsa_cmp_pe"][l]
    pe_big = jnp.broadcast_to(pe.transpose(1, 0, 2)[:, :, None, :], (CMP_BLOCK, 2, NSA_KV_HEADS, NSA_DIM))
    b1big = jnp.broadcast_to(w["nsa_cmp_b1"][l][:, None, :], (2, NSA_KV_HEADS, NSA_DIM))
    return dict(
        norm_mix_g=w["norm_mix_g"][l].reshape(1, d),
        w_in=w_packed.astype(BF16),
        mla_q_norm_g=w["mla_q_norm_g"][l].reshape(1, Q_LORA),
        w_uq=wuq.astype(BF16),
        w_ukT=w["mla_w_uk"][l].transpose(1, 2, 0).astype(BF16),
        mla_kv_norm_g=w["mla_kv_norm_g"][l].reshape(1, KV_LORA),
        w_uv=w["mla_w_uv"][l].transpose(1, 0, 2).astype(BF16),
        w_out=w["w_out"][l].astype(BF16),
        norm_ffn_g=w["norm_ffn_g"][l].reshape(1, d),
        ffn_w13=w["ffn_w13"][l].astype(BF16),
        ffn_w2=w["ffn_w2"][l].astype(BF16),
        cmp_pe=pe_big.reshape(CMP_BLOCK, 1, KV_ROW),
        cmp_pe_t=jnp.tile(pe_big.reshape(CMP_BLOCK, KV_ROW).T, (1, 2 * PAGE_SIZE // CMP_BLOCK)),
        cmp_w1=w1big.astype(BF16),
        cmp_b1=b1big.reshape(1, KV_ROW),
        cmp_w2=w2big.astype(BF16),
    )


def _cmp_block_ends(n_cmp):
    order = np.concatenate([np.arange(0, n_cmp, 2), np.arange(1, n_cmp, 2)])
    return (order + 1) * CMP_BLOCK - 1


def _prompt_bias_tables(table, nt):
    tq = ATT_TILE
    i = np.arange(tq)[:, None]
    j = np.arange(tq)[None, :]
    d0, d1, d2 = i - j, tq + i - j, 2 * tq + i - j
    bucket = np.concatenate([_t5_bucket_np(d0), _t5_bucket_np(d1), _t5_bucket_np(d2)], axis=0)
    mask = np.concatenate([np.where(d0 >= 0, 0.0, NEG_BIG), np.zeros((tq, tq)),
                           np.where(d2 < WINDOW, 0.0, NEG_BIG)], axis=0).astype(np.float32)
    tiles = _expand_bias(table, bucket, mask, True).reshape(NSA_HEADS, 3, tq, tq)
    n_cmp = nt // CMP_BLOCK
    dist = np.arange(nt)[None, :] - _cmp_block_ends(n_cmp)[:, None]
    cmp_bias = _expand_bias(table, _t5_bucket_np(dist), np.where(dist >= 0, 0.0, NEG_BIG).astype(np.float32), False)
    expand = (np.arange(SEL_LANES)[None, :, None] ==
              (np.arange(nt // tq)[:, None, None] * tq + np.arange(tq)[None, None, :]) // SLC_BLOCK)
    return tiles, cmp_bias, jnp.asarray(expand.astype(np.float32), dtype=BF16)


def _sample_bias_tables(table, nt, past, wlen):
    g, hg = NSA_KV_HEADS, NSA_GROUP
    rows = nt * hg
    q_pos = past + np.arange(nt)

    def per_group(b, width):
        return b.reshape(g, hg, nt, width).transpose(0, 2, 1, 3).reshape(g, rows, width)

    n_cmp = past // CMP_BLOCK
    dist = q_pos[:, None] - _cmp_block_ends(n_cmp)[None, :]
    cmp_bias = per_group(_expand_bias(table, _t5_bucket_np(dist),
                                      np.where(dist >= 0, 0.0, NEG_BIG).astype(np.float32), False), n_cmp)
    dist = q_pos[:, None] - (past - wlen + np.arange(wlen))[None, :]
    ok = (dist >= 0) & (dist < WINDOW)
    win_bias = per_group(_expand_bias(table, _t5_bucket_np(dist), np.where(ok, 0.0, NEG_BIG).astype(np.float32),
                                      False), wlen)
    dist = q_pos[:, None] - (past + np.arange(8))[None, :]
    ok = (dist >= 0) & (dist < WINDOW) & (np.arange(8)[None, :] < nt)
    new_mask = np.where(ok, 0.0, NEG_BIG).astype(np.float32)
    win_new = per_group(_expand_bias(table, _t5_bucket_np(dist), new_mask, False), 8)
    n_past = past // SLC_BLOCK
    near = np.stack([np.full((nt, SLC_BLOCK), 10 * MAX_DISTANCE),
                     q_pos[:, None] - ((n_past - 2) * SLC_BLOCK + np.arange(SLC_BLOCK))[None, :],
                     q_pos[:, None] - ((n_past - 1) * SLC_BLOCK + np.arange(SLC_BLOCK))[None, :]], axis=1)
    near = near.reshape(nt, 3 * SLC_BLOCK)
    slc_near = _expand_bias(table, _t5_bucket_np(near), np.zeros(near.shape, np.float32), True)
    slc_near = slc_near.reshape(g, hg, nt, 3, SLC_BLOCK).transpose(2, 0, 3, 1, 4)
    slc_near = jnp.tile(slc_near, (1, 1, 1, 1, PAGE_SIZE // SLC_BLOCK))
    slc_cur = _expand_bias(table, _t5_bucket_np(dist), np.where((dist >= 0) & (np.arange(8)[None, :] < nt), 0.0,
                                                                 NEG_BIG).astype(np.float32), True)
    slc_cur = slc_cur.reshape(g, hg, nt, 8).transpose(2, 0, 1, 3)
    return cmp_bias, win_bias, win_new, slc_near, slc_cur


def kernel(x_prompt, x_sample, cache_cmp_kv, cache_slc_kv, cache_mla, state_win_kv, page_table, c_prompt, c_sample,
           rel_bias_table, ada_w, ada_b, norm_mix_g, w_in, nsa_cmp_pe, nsa_cmp_w1, nsa_cmp_b1, nsa_cmp_w2,
           mla_q_norm_g, mla_w_uq, mla_kv_norm_g, mla_w_uk, mla_w_uv, w_out, norm_ffn_g, ffn_w13, ffn_w2,
           final_norm_g):
    weights = dict(norm_mix_g=norm_mix_g, w_in=w_in, nsa_cmp_pe=nsa_cmp_pe, nsa_cmp_w1=nsa_cmp_w1,
                   nsa_cmp_b1=nsa_cmp_b1, nsa_cmp_w2=nsa_cmp_w2, mla_q_norm_g=mla_q_norm_g, mla_w_uq=mla_w_uq,
                   mla_kv_norm_g=mla_kv_norm_g, mla_w_uk=mla_w_uk, mla_w_uv=mla_w_uv, w_out=w_out,
                   norm_ffn_g=norm_ffn_g, ffn_w13=ffn_w13, ffn_w2=ffn_w2)
    depth = ada_w.shape[0]
    d = D_MODEL
    g = NSA_KV_HEADS
    pb, pt_len, _ = x_prompt.shape
    sb, st_len, _ = x_sample.shape
    n_pool = cache_cmp_kv.shape[1]
    n_pages = page_table.shape[1]
    past = n_pages * PAGE_SIZE
    wlen = state_win_kv.shape[2]
    n_past_blocks = past // SLC_BLOCK
    n_pick = N_SELECT - 1
    assert pt_len % ATT_TILE == 0 and pt_len >= WINDOW and st_len <= 8 and past >= wlen
    assert n_past_blocks > n_pick and n_pages % MLA_PAGES == 0

    layers = [_pack_layer(l, weights) for l in range(depth)]
    gfinal = final_norm_g.reshape(1, d)
    mod_all = _modulation(jnp.concatenate([c_prompt, c_sample], axis=0), ada_w, ada_b)

    tiles, cmp_bias_p, expand = _prompt_bias_tables(rel_bias_table, pt_len)
    cmp_bias_s, win_bias_s, win_new_s, slc_near_s, slc_cur_s = _sample_bias_tables(rel_bias_table, st_len, past, wlen)
    cos_p, sin_p = _rope_tables(jnp.arange(pt_len), MLA_HEADS)
    cos_s, sin_s = _rope_tables(jnp.tile(past + jnp.arange(st_len), sb), MLA_HEADS)
    prompt_pages = jnp.arange(pb * pt_len // PAGE_SIZE, dtype=jnp.int32)
    pt_flat = page_table.reshape(-1).astype(jnp.int32)

    x = x_prompt.reshape(pb * pt_len, d)
    tm = 256
    states = (jnp.zeros((depth, pb, KV_ROW, pt_len), F32), jnp.zeros((depth, pb, KV_ROW, pt_len), F32),
              jnp.zeros((depth, pb, LAT, pt_len), F32))
    win_t = []
    for l in range(depth):
        lw = layers[l]
        mod = mod_all[l, :pb].reshape(pb, 6, d)
        (q, kvc, _, _, kst, kse, kso, kwt, kwe, kwo, gates, qmla, _, latv, latt, gm, st_cmp, st_slc, st_lat,
         kvw_t) = _inproj(x, mod, False, lw, cos_p, sin_p, pb, pt_len, tm, states=states, layer=l)
        states = (st_cmp, st_slc, st_lat)
        win_t.append(kvw_t[:, :, pt_len - min(WINDOW, pt_len):])
        tok = _compress(kvc.reshape(-1, PAGE_SIZE, KV_ROW), prompt_pages, lw)
        kc, vc = _split_compressed(tok, pb)
        o_cmp, sel = _cmp_prompt(q, kc, vc.transpose(0, 1, 3, 2), cmp_bias_p, gates, pb, pt_len)
        o_slc = _nsa_flash(q, kst, kse, kso, tiles, gates, pb, pt_len, sel=sel, expand=expand)
        o_win = _nsa_flash(q, kwt, kwe, kwo, tiles, gates, pb, pt_len)
        o_lat = _mla_prompt(qmla, latt, latv, pb, pt_len)
        x = _merge(o_cmp.reshape(-1, d), o_slc.reshape(-1, d), o_win.reshape(-1, d), o_lat, gm, x, mod, False, lw,
                   pb, pt_len, tm)
        x = _ffn(x, mod, False, lw, gfinal, l == depth - 1, pb, pt_len, 512)
    y_prompt = x.reshape(pb, pt_len, d)

    def rows_first(a):
        return a.reshape(depth, pb, 2, g, NSA_DIM, a.shape[-1]).transpose(0, 1, 5, 2, 3, 4)
    st_p = (rows_first(states[0]), rows_first(states[1]), states[2].transpose(0, 1, 3, 2),
            rows_first(jnp.stack(win_t)))

    ms = sb * st_len
    x = x_sample.reshape(ms, d)
    st_s = [[], [], [], []]
    cmp_pages = cache_cmp_kv.transpose(0, 1, 3, 4, 5, 2).reshape(depth * n_pool, KV_ROW, PAGE_SIZE)
    slc_pages = cache_slc_kv.transpose(0, 1, 3, 4, 5, 2).reshape(depth * n_pool, KV_ROW, PAGE_SIZE)
    mla_pages = cache_mla.transpose(0, 1, 3, 2).reshape(depth * n_pool, LAT, PAGE_SIZE)
    win_state = state_win_kv.transpose(0, 1, 3, 4, 5, 2).reshape(depth, sb, KV_ROW, wlen)
    for l in range(depth):
        lw = layers[l]
        mod = jnp.repeat(mod_all[l, pb:], st_len, axis=0).reshape(ms, 6, d)
        (q, kvc, kvs, kvw, _, _, _, _, _, _, gates, qmla, lat, _, _, gm) = _inproj(
            x, mod, True, lw, cos_s, sin_s, 1, ms, ms)
        pt_l = pt_flat + l * n_pool
        tok = _compress_t(cmp_pages, pt_l, lw)
        n_cmp_s = tok.shape[0] // sb
        tok = tok.reshape(sb, n_cmp_s // 2, 2, KV_ROW).transpose(0, 2, 1, 3).reshape(sb, n_cmp_s, KV_ROW)
        q5 = q.reshape(g, NSA_GROUP, sb, st_len, NSA_DIM).transpose(2, 3, 0, 1, 4)
        gate_cols = gates.reshape(sb, st_len, g, GATE_LANES)[..., :3 * NSA_GROUP]
        gate_cols = gate_cols.reshape(sb, st_len, g, 3, NSA_GROUP).transpose(0, 2, 3, 1, 4)
        gate_rows = gate_cols.reshape(sb, g, 3, st_len * NSA_GROUP, 1)
        o_cmp, imp = _cmp_sample(q5, tok, cmp_bias_s, gate_rows, sb, st_len)
        idx = _topk_sample(imp.reshape(sb * g * st_len, n_past_blocks), n_pick)
        idx_flat = idx[:, :n_pick].reshape(sb, g, st_len, n_pick).transpose(0, 2, 1, 3).reshape(-1)
        pad8 = lambda a: jnp.pad(a.reshape(sb, st_len, -1), ((0, 0), (0, 8 - st_len), (0, 0)))
        o_slc = _slc_sample(idx_flat, pt_l, slc_pages, q5, pad8(kvs), slc_near_s, slc_cur_s,
                            gate_cols.reshape(sb, g, 3, st_len, NSA_GROUP, 1)[:, :, 1:2], sb, st_len, n_pick,
                            n_past_blocks, n_pages)
        o_win = _win_sample(q5, win_state, l, pad8(kvw), win_bias_s, win_new_s, gate_rows, sb, st_len)
        q_rows = qmla.reshape(MLA_HEADS, sb, st_len, LAT).transpose(1, 0, 2, 3).reshape(sb, MLA_HEADS * st_len, LAT)
        o_lat = _mla_sample(pt_l, mla_pages, q_rows, pad8(lat), sb, st_len, n_pages)
        o_lat = o_lat.reshape(sb, MLA_HEADS, st_len, KV_LORA).transpose(1, 0, 2, 3).reshape(1, MLA_HEADS, ms, KV_LORA)
        x = _merge(o_cmp.reshape(ms, d), o_slc.reshape(ms, d), o_win.reshape(ms, d), o_lat, gm, x, mod, True, lw,
                   1, ms, ms)
        x = _ffn(x, mod, True, lw, gfinal, l == depth - 1, 1, ms, ms)
        st_s[0].append(kvc.reshape(sb, st_len, 2, g, NSA_DIM))
        st_s[1].append(kvs.reshape(sb, st_len, 2, g, NSA_DIM))
        st_s[2].append(lat.reshape(sb, st_len, LAT))
        win_all = jnp.concatenate([state_win_kv[l], kvw.reshape(sb, st_len, 2, g, NSA_DIM)], axis=1)
        st_s[3].append(win_all[:, st_len:])
    y_sample = x.reshape(sb, st_len, d)

    return (y_prompt, y_sample, st_p[0], st_p[1], st_p[2], st_p[3],
            jnp.stack(st_s[0]), jnp.stack(st_s[1]), jnp.stack(st_s[2]), jnp.stack(st_s[3]))
```

```python
import functools
import math

import numpy as np
import jax
import jax.numpy as jnp
from jax import lax
from jax.experimental import pallas as pl
from jax.experimental.pallas import tpu as pltpu

F32 = jnp.float32
BF16 = jnp.bfloat16

D_MODEL = 1024
PAGE_SIZE = 128
NSA_HEADS = 16
NSA_KV_HEADS = 2
NSA_GROUP = NSA_HEADS // NSA_KV_HEADS
NSA_DIM = D_MODEL // NSA_HEADS
CMP_BLOCK = 32
SLC_BLOCK = 64
N_SELECT = 16
WINDOW = 512
MLA_HEADS = 8
MLA_NOPE = 128
MLA_ROPE = 64
MLA_V = D_MODEL // MLA_HEADS
Q_LORA = 384
KV_LORA = 256
LAT = KV_LORA + MLA_ROPE
MLA_QK = MLA_NOPE + MLA_ROPE
ROPE_THETA = 10000.0
N_BUCKETS = 32
MAX_DISTANCE = 128
NSA_SCALE = NSA_DIM ** -0.5
MLA_SCALE = (MLA_NOPE + MLA_ROPE) ** -0.5
NEG_BIG = -1e30
RMS_EPS = 1e-6
KV_ROW = 2 * NSA_KV_HEADS * NSA_DIM
GATE_LANES = 128

SEG_Q = (0, 1024)
SEG_CMP = (1024, 1280)
SEG_SLC = (1280, 1536)
SEG_WIN = (1536, 1792)
SEG_GATE = (1792, 2048)
SEG_CQ = (2048, 2432)
SEG_CKV = (2432, 2688)
SEG_KR = (2688, 2816)
SEG_GM = (2816, 4864)
D_IN_PACKED = 4864

ATT_TILE = 256
FLASH_ROWS = 128
MLA_FLASH_ROWS = 512
SEL_LANES = 128
VMEM_LIMIT = 56 * 1024 * 1024
CMP_PAGES = 64
MLA_PAGES = 32


def _cparams(sem):
    return pltpu.CompilerParams(dimension_semantics=sem, vmem_limit_bytes=VMEM_LIMIT)


def _dot(a, b):
    return jnp.dot(a, b, preferred_element_type=F32)


def _dot_nt(a, b):
    return lax.dot_general(a, b, (((1,), (1,)), ((), ())), preferred_element_type=F32)


def _rms(x, g):
    return x * lax.rsqrt(jnp.mean(x * x, axis=-1, keepdims=True) + RMS_EPS) * g


def _rope_lanes(x, cos2, sin2):
    w = x.shape[-1]
    lane = lax.broadcasted_iota(jnp.int32, x.shape, 1)
    swapped = jnp.where(lane % MLA_ROPE < MLA_ROPE // 2, pltpu.roll(x, w - MLA_ROPE // 2, 1),
                        pltpu.roll(x, MLA_ROPE // 2, 1))
    return x * cos2 + swapped * sin2


def _t5_bucket_np(dist):
    max_exact = N_BUCKETS // 2
    d = np.maximum(dist, 0)
    log_ratio = np.log(np.maximum(d, 1).astype(np.float32) / max_exact) / math.log(MAX_DISTANCE / max_exact)
    large = np.minimum(max_exact + (log_ratio * (N_BUCKETS - max_exact)).astype(np.int32), N_BUCKETS - 1)
    return np.where(d < max_exact, d, large).astype(np.int32)


def _rope_tables(pos, reps):
    half = MLA_ROPE // 2
    inv = ROPE_THETA ** (-jnp.arange(half, dtype=F32) / half)
    ang = pos.astype(F32)[:, None] * inv[None, :]
    cos, sin = jnp.cos(ang), jnp.sin(ang)
    cos2 = jnp.concatenate([cos, cos], axis=-1)
    sin2 = jnp.concatenate([-sin, sin], axis=-1)
    return jnp.tile(cos2, (1, reps)), jnp.tile(sin2, (1, reps))


def _bias_kernel(table_ref, bucket_ref, mask_ref, o_ref, *, shift):
    h = pl.program_id(0)
    bucket = bucket_ref[...]
    acc = jnp.zeros(bucket.shape, F32)
    for b in range(N_BUCKETS):
        acc = jnp.where(bucket == b, table_ref[b, h], acc)
    if shift:
        acc = acc - table_ref[N_BUCKETS - 1, h]
    o_ref[0] = acc + mask_ref[...]


def _expand_bias(table, bucket, addmask, shift):
    r, c = bucket.shape
    return pl.pallas_call(
        functools.partial(_bias_kernel, shift=shift),
        grid=(NSA_HEADS,),
        in_specs=[pl.BlockSpec(memory_space=pltpu.SMEM),
                  pl.BlockSpec((r, c), lambda h: (0, 0)),
                  pl.BlockSpec((r, c), lambda h: (0, 0))],
        out_specs=pl.BlockSpec((1, r, c), lambda h: (h, 0, 0)),
        out_shape=jax.ShapeDtypeStruct((NSA_HEADS, r, c), F32),
        compiler_params=_cparams(("arbitrary",)),
        name="bias_expand",
    )(table, jnp.asarray(bucket), jnp.asarray(addmask))


def _mod_kernel(c_ref, w_ref, b_ref, o_ref):
    c = c_ref[...]
    cond = (c * jax.nn.sigmoid(c)).astype(BF16)
    o_ref[0] = _dot(cond, w_ref[0].astype(BF16)) + b_ref[0]


def _modulation(c_all, ada_w, ada_b):
    depth, d, n = ada_w.shape
    rows = c_all.shape[0]
    tn = 1536
    return pl.pallas_call(
        _mod_kernel,
        grid=(depth, n // tn),
        in_specs=[pl.BlockSpec((rows, d), lambda l, j: (0, 0)),
                  pl.BlockSpec((1, d, tn), lambda l, j: (l, 0, j)),
                  pl.BlockSpec((1, 1, tn), lambda l, j: (l, 0, j))],
        out_specs=pl.BlockSpec((1, rows, tn), lambda l, j: (l, 0, j)),
        out_shape=jax.ShapeDtypeStruct((depth, rows, n), F32),
        compiler_params=_cparams(("arbitrary", "arbitrary")),
        name="adaln_mod",
    )(c_all, ada_w, ada_b.reshape(depth, 1, n))


def _inproj_kernel(*refs, with_states):
    (x_ref, mod_ref, gn_ref, w_ref, cos_ref, sin_ref, gq_ref, wuq_ref, wuk_ref, gkv_ref) = refs[:10]
    outs = refs[13:] if with_states else refs[10:]
    (q_ref, kvc_ref, kvs_ref, kvw_ref, kst_ref, kse_ref, kso_ref, kwt_ref, kwe_ref, kwo_ref, gate_ref,
     qmla_ref, lat_ref, latb_ref, latt_ref, gm_ref) = outs[:16]
    x = x_ref[...]
    y = _rms(x, gn_ref[...])
    h = (y * (1.0 + mod_ref[:, 1, :]) + mod_ref[:, 0, :]).astype(BF16)

    def seg(s):
        return _dot(h, w_ref[:, s[0]:s[1]])

    q = (seg(SEG_Q) * NSA_SCALE).astype(BF16)
    for hd in range(NSA_HEADS):
        q_ref[0, hd] = q[:, hd * NSA_DIM:(hd + 1) * NSA_DIM]
    kvc = seg(SEG_CMP)
    kvc_ref[...] = kvc
    half = NSA_KV_HEADS * NSA_DIM
    low = lax.broadcasted_iota(jnp.int32, (x.shape[0], half), 1) < NSA_DIM
    kv_t = {}
    for s, kv_ref, kt_ref, ve_ref, vo_ref in ((SEG_SLC, kvs_ref, kst_ref, kse_ref, kso_ref),
                                              (SEG_WIN, kvw_ref, kwt_ref, kwe_ref, kwo_ref)):
        kv = seg(s)
        kv_ref[...] = kv
        kv_t[s] = kv.T
        kt = kv_t[s][0:half].astype(BF16)
        vv = kv[:, half:2 * half]
        vr = pltpu.roll(vv, NSA_DIM, 1)
        for g in range(NSA_KV_HEADS):
            kt_ref[0, g] = kt[g * NSA_DIM:(g + 1) * NSA_DIM]
            ve_ref[0, g] = jnp.where(low, vv if g == 0 else vr, 1.0).astype(BF16)
            vo_ref[0, g] = jnp.where(low, 1.0, vr if g == 0 else vv).astype(BF16)
    gate_ref[...] = jax.nn.sigmoid(seg(SEG_GATE))
    gm_ref[...] = jax.nn.sigmoid(seg(SEG_GM))

    cos = cos_ref[...]
    sin = sin_ref[...]
    cqn = _rms(seg(SEG_CQ), gq_ref[...]).astype(BF16)
    qm = _dot(cqn, wuq_ref[...])
    nope_w = MLA_HEADS * MLA_NOPE
    qr = _rope_lanes(qm[:, nope_w:], cos, sin) * MLA_SCALE
    for hd in range(MLA_HEADS):
        qn = qm[:, hd * MLA_NOPE:(hd + 1) * MLA_NOPE]
        qrh = qr[:, hd * MLA_ROPE:(hd + 1) * MLA_ROPE].astype(BF16)
        if with_states:
            qmla_ref[0, hd, :, 0:MLA_NOPE] = (qn * MLA_SCALE).astype(BF16)
            qmla_ref[0, hd, :, MLA_NOPE:MLA_QK] = qrh
        else:
            qmla_ref[0, hd, :, 0:KV_LORA] = (_dot(qn.astype(BF16), wuk_ref[hd]) * MLA_SCALE).astype(BF16)
            qmla_ref[0, hd, :, KV_LORA:LAT] = qrh
    ckv = _rms(seg(SEG_CKV), gkv_ref[...])
    kr = _rope_lanes(seg(SEG_KR), cos[:, 0:128], sin[:, 0:128])
    lat_ref[:, 0:KV_LORA] = ckv
    lat_ref[:, KV_LORA:LAT] = kr[:, 0:MLA_ROPE]
    latb_ref[...] = ckv.astype(BF16)
    ckv_t = ckv.T
    kr_t = kr.T[0:MLA_ROPE]
    if with_states:
        ckv_tb = ckv_t.astype(BF16)
        for hd in range(MLA_HEADS):
            latt_ref[0, hd, 0:MLA_NOPE, :] = _dot(wuk_ref[hd], ckv_tb).astype(BF16)
            latt_ref[0, hd, MLA_NOPE:MLA_QK, :] = kr_t.astype(BF16)
    else:
        latt_ref[0, 0:KV_LORA, :] = ckv_t.astype(BF16)
        latt_ref[0, KV_LORA:LAT, :] = kr_t.astype(BF16)
    if with_states:
        stc_ref, sts_ref, stl_ref, stw_ref = outs[16:]
        stc_ref[0, 0] = kvc.T
        sts_ref[0, 0] = kv_t[SEG_SLC]
        stw_ref[0] = kv_t[SEG_WIN]
        stl_ref[0, 0, 0:KV_LORA, :] = ckv_t
        stl_ref[0, 0, KV_LORA:LAT, :] = kr_t


def _inproj(x, mod, per_row_mod, lw, cos_t, sin_t, nb, nt, tm, states=None, layer=0):
    m = nb * nt
    tpb = nt // tm
    d = D_MODEL
    if per_row_mod:
        mod_spec = pl.BlockSpec((tm, 6, d), lambda i: (i, 0, 0))
    else:
        mod_spec = pl.BlockSpec((1, 6, d), lambda i: (i // tpb, 0, 0))
    const2 = lambda i: (0, 0)
    row = lambda i: (i, 0)
    bt = lambda i: (i // tpb, 0, i % tpb, 0)
    btt = lambda i: (i // tpb, 0, 0, i % tpb)
    g = NSA_KV_HEADS
    prompt = states is not None
    out_shape = (
        jax.ShapeDtypeStruct((nb, NSA_HEADS, nt, NSA_DIM), BF16),
        jax.ShapeDtypeStruct((m, KV_ROW), F32),
        jax.ShapeDtypeStruct((m, KV_ROW), F32),
        jax.ShapeDtypeStruct((m, KV_ROW), F32),
        jax.ShapeDtypeStruct((nb, g, NSA_DIM, nt), BF16),
        jax.ShapeDtypeStruct((nb, g, nt, 2 * NSA_DIM), BF16),
        jax.ShapeDtypeStruct((nb, g, nt, 2 * NSA_DIM), BF16),
        jax.ShapeDtypeStruct((nb, g, NSA_DIM, nt), BF16),
        jax.ShapeDtypeStruct((nb, g, nt, 2 * NSA_DIM), BF16),
        jax.ShapeDtypeStruct((nb, g, nt, 2 * NSA_DIM), BF16),
        jax.ShapeDtypeStruct((m, g * GATE_LANES), F32),
        jax.ShapeDtypeStruct((nb, MLA_HEADS, nt, MLA_QK if prompt else LAT), BF16),
        jax.ShapeDtypeStruct((m, LAT), F32),
        jax.ShapeDtypeStruct((m, KV_LORA), BF16),
        jax.ShapeDtypeStruct((nb, MLA_HEADS, MLA_QK, nt) if prompt else (nb, LAT, nt), BF16),
        jax.ShapeDtypeStruct((m, 2 * d), F32),
    )
    kt_spec = pl.BlockSpec((1, g, NSA_DIM, tm), btt)
    v_spec = pl.BlockSpec((1, g, tm, 2 * NSA_DIM), bt)
    out_specs = (
        pl.BlockSpec((1, NSA_HEADS, tm, NSA_DIM), bt),
        pl.BlockSpec((tm, KV_ROW), row), pl.BlockSpec((tm, KV_ROW), row), pl.BlockSpec((tm, KV_ROW), row),
        kt_spec, v_spec, v_spec, kt_spec, v_spec, v_spec,
        pl.BlockSpec((tm, g * GATE_LANES), row),
        pl.BlockSpec((1, MLA_HEADS, tm, MLA_QK if prompt else LAT), bt),
        pl.BlockSpec((tm, LAT), row), pl.BlockSpec((tm, KV_LORA), row),
        pl.BlockSpec((1, MLA_HEADS, MLA_QK, tm), btt) if prompt else
        pl.BlockSpec((1, LAT, tm), lambda i: (i // tpb, 0, i % tpb)),
        pl.BlockSpec((tm, 2 * d), row),
    )
    in_specs = [
        pl.BlockSpec((tm, d), row), mod_spec, pl.BlockSpec((1, d), const2),
        pl.BlockSpec((d, D_IN_PACKED), const2),
        pl.BlockSpec((tm, MLA_HEADS * MLA_ROPE), lambda i: (i % tpb, 0)),
        pl.BlockSpec((tm, MLA_HEADS * MLA_ROPE), lambda i: (i % tpb, 0)),
        pl.BlockSpec((1, Q_LORA), const2),
        pl.BlockSpec((Q_LORA, MLA_HEADS * (MLA_NOPE + MLA_ROPE)), const2),
        pl.BlockSpec((MLA_HEADS, MLA_NOPE, KV_LORA), lambda i: (0, 0, 0)),
        pl.BlockSpec((1, KV_LORA), const2),
    ]
    args = [x, mod, lw["norm_mix_g"], lw["w_in"], cos_t, sin_t, lw["mla_q_norm_g"], lw["w_uq"], lw["w_ukT"],
            lw["mla_kv_norm_g"]]
    aliases = {}
    if states is not None:
        n_in, n_out = len(args), len(out_shape)
        st_block = lambda width: pl.BlockSpec((1, 1, width, tm), lambda i: (layer, i // tpb, 0, i % tpb))
        in_specs += [pl.BlockSpec(memory_space=pl.ANY)] * 3
        args += list(states)
        out_shape += tuple(jax.ShapeDtypeStruct(s.shape, s.dtype) for s in states)
        out_shape += (jax.ShapeDtypeStruct((nb, KV_ROW, nt), F32),)
        out_specs += (st_block(KV_ROW), st_block(KV_ROW), st_block(LAT),
                      pl.BlockSpec((1, KV_ROW, tm), lambda i: (i // tpb, 0, i % tpb)))
        aliases = {n_in + k: n_out + k for k in range(3)}
    return pl.pallas_call(
        functools.partial(_inproj_kernel, with_states=states is not None),
        grid=(m // tm,), in_specs=in_specs, out_specs=out_specs, out_shape=out_shape,
        input_output_aliases=aliases,
        compiler_params=_cparams(("arbitrary",)), name="in_proj",
    )(*args)


def _gelu_tanh(z):
    return 0.5 * z * (1.0 + jnp.tanh(math.sqrt(2.0 / math.pi) * (z + 0.044715 * (z * z * z))))


def _compress_t_kernel(tab_ref, *refs, pages):
    del tab_ref
    page_refs = refs[:pages]
    pet_ref, sel_ref, w1_ref, b1_ref, w2_ref, o_ref, slab_ref = refs[pages:]
    pairs = pages // 2
    rows_per_pair = 2 * (PAGE_SIZE // CMP_BLOCK)
    sel = sel_ref[...]
    pet = pet_ref[...]
    for pr in range(pairs):
        xt2 = jnp.concatenate([page_refs[2 * pr][0], page_refs[2 * pr + 1][0]], axis=1)
        slab_ref[pr] = _dot_nt(sel, (xt2 + pet).astype(BF16))
    acc = jnp.zeros((pairs * rows_per_pair, KV_ROW), F32)
    for r in range(CMP_BLOCK):
        xr = jnp.concatenate([slab_ref[pr, r * rows_per_pair:(r + 1) * rows_per_pair, :] for pr in range(pairs)],
                             axis=0)
        acc = acc + _dot(xr.astype(BF16), w1_ref[r])
    hmid = _gelu_tanh(acc + b1_ref[...])
    o_ref[0] = _dot(hmid.astype(BF16), w2_ref[...])


def _compress_t(pages3d, page_ids, lw):
    n_logical = page_ids.shape[0]
    pages = min(CMP_PAGES, n_logical)
    assert n_logical % pages == 0 and pages % 2 == 0
    steps = n_logical // pages
    per_page = PAGE_SIZE // CMP_BLOCK

    def page_map(k):
        return lambda s, tab: (tab[s * pages + k], 0, 0)

    in_specs = [pl.BlockSpec((1, KV_ROW, PAGE_SIZE), page_map(k)) for k in range(pages)]
    in_specs += [
        pl.BlockSpec((KV_ROW, 2 * PAGE_SIZE), lambda s, tab: (0, 0)),
        pl.BlockSpec((2 * PAGE_SIZE, 2 * PAGE_SIZE), lambda s, tab: (0, 0)),
        pl.BlockSpec((CMP_BLOCK, KV_ROW, KV_ROW), lambda s, tab: (0, 0, 0)),
        pl.BlockSpec((1, KV_ROW), lambda s, tab: (0, 0)),
        pl.BlockSpec((KV_ROW, KV_ROW), lambda s, tab: (0, 0)),
    ]
    r, pg, c = np.meshgrid(np.arange(CMP_BLOCK), np.arange(2), np.arange(per_page), indexing="ij")
    src = (pg * PAGE_SIZE + c * CMP_BLOCK + r).reshape(-1)
    sel = jnp.asarray((src[:, None] == np.arange(2 * PAGE_SIZE)[None, :]).astype(np.float32), dtype=BF16)
    out = pl.pallas_call(
        functools.partial(_compress_t_kernel, pages=pages),
        grid_spec=pltpu.PrefetchScalarGridSpec(
            num_scalar_prefetch=1, grid=(steps,), in_specs=in_specs,
            out_specs=pl.BlockSpec((1, per_page * pages, KV_ROW), lambda s, tab: (s, 0, 0)),
            scratch_shapes=[pltpu.VMEM((pages // 2, 2 * PAGE_SIZE, KV_ROW), F32)]),
        out_shape=jax.ShapeDtypeStruct((steps, per_page * pages, KV_ROW), F32),
        compiler_params=_cparams(("arbitrary",)), name="cmp_compress_paged",
    )(page_ids, *([pages3d] * pages), lw["cmp_pe_t"], sel, lw["cmp_w1"], lw["cmp_b1"], lw["cmp_w2"])
    return out.reshape(n_logical * per_page, KV_ROW)


def _compress_kernel(tab_ref, *refs, pages):
    del tab_ref
    page_refs = refs[:pages]
    pe_ref, w1_ref, b1_ref, w2_ref, o_ref, slabk_ref, slabv_ref = refs[pages:]
    half = KV_ROW // 2
    for k in range(pages):
        slabk_ref[k * PAGE_SIZE:(k + 1) * PAGE_SIZE, :] = page_refs[k][0, :, 0:half]
        slabv_ref[k * PAGE_SIZE:(k + 1) * PAGE_SIZE, :] = page_refs[k][0, :, half:KV_ROW]
    per_page = PAGE_SIZE // CMP_BLOCK
    acc = jnp.zeros((per_page * pages, KV_ROW), F32)
    for r in range(CMP_BLOCK):
        rows = [jnp.concatenate([slab[pl.ds(c * CMP_BLOCK + r, pages, stride=PAGE_SIZE), :]
                                 for slab in (slabk_ref, slabv_ref)], axis=1) for c in range(per_page)]
        xr = (jnp.concatenate(rows, axis=0) + pe_ref[r]).astype(BF16)
        acc = acc + _dot(xr, w1_ref[r])
    hmid = _gelu_tanh(acc + b1_ref[...])
    o_ref[0] = _dot(hmid.astype(BF16), w2_ref[...])


def _compress(rows3d, page_ids, lw):
    n_logical = page_ids.shape[0]
    pages = min(CMP_PAGES, n_logical)
    assert n_logical % pages == 0
    steps = n_logical // pages
    per_page = PAGE_SIZE // CMP_BLOCK

    def page_map(k):
        return lambda s, tab: (tab[s * pages + k], 0, 0)

    in_specs = [pl.BlockSpec((1, PAGE_SIZE, KV_ROW), page_map(k)) for k in range(pages)]
    in_specs += [
        pl.BlockSpec((CMP_BLOCK, 1, KV_ROW), lambda s, tab: (0, 0, 0)),
        pl.BlockSpec((CMP_BLOCK, KV_ROW, KV_ROW), lambda s, tab: (0, 0, 0)),
        pl.BlockSpec((1, KV_ROW), lambda s, tab: (0, 0)),
        pl.BlockSpec((KV_ROW, KV_ROW), lambda s, tab: (0, 0)),
    ]
    out = pl.pallas_call(
        functools.partial(_compress_kernel, pages=pages),
        grid_spec=pltpu.PrefetchScalarGridSpec(
            num_scalar_prefetch=1, grid=(steps,), in_specs=in_specs,
            out_specs=pl.BlockSpec((1, per_page * pages, KV_ROW), lambda s, tab: (s, 0, 0)),
            scratch_shapes=[pltpu.VMEM((pages * PAGE_SIZE, KV_ROW // 2), F32),
                            pltpu.VMEM((pages * PAGE_SIZE, KV_ROW // 2), F32)]),
        out_shape=jax.ShapeDtypeStruct((steps, per_page * pages, KV_ROW), F32),
        compiler_params=_cparams(("arbitrary",)), name="cmp_compress",
    )(page_ids, *([rows3d] * pages), lw["cmp_pe"], lw["cmp_w1"], lw["cmp_b1"], lw["cmp_w2"])
    out = out.reshape(steps, per_page, pages, KV_ROW).transpose(0, 2, 1, 3)
    return out.reshape(n_logical * per_page, KV_ROW)


def _split_compressed(tok, nb):
    n = tok.shape[0] // nb
    t = tok.reshape(nb, n // 2, 2, 2, NSA_KV_HEADS, NSA_DIM)
    t = t.transpose(3, 0, 4, 2, 1, 5).reshape(2, nb, NSA_KV_HEADS, n, NSA_DIM).astype(BF16)
    return t[0], t[1]


def _cmp_prompt_kernel(q_ref, kc_ref, vct_ref, bias_ref, gate_ref, o_ref, sel_ref, *, tq, n_cmp):
    qi = pl.program_id(2)
    kc = kc_ref[0, 0]
    vct = vct_ref[0, 0]
    maskf = (bias_ref[0] > 0.5 * NEG_BIG).astype(F32)
    gates_t = gate_ref[0].T
    zs = [_dot_nt(kc, q_ref[0, hh]) + bias_ref[hh] for hh in range(NSA_GROUP)]
    es = [jnp.exp(z - jnp.max(z, axis=0, keepdims=True)) * maskf for z in zs]
    ps = [e / jnp.maximum(jnp.sum(e, axis=0, keepdims=True), 1e-30) for e in es]
    outs = [_dot(vct, ps[hh].astype(BF16)) * gates_t[hh:hh + 1, :] for hh in range(NSA_GROUP)]
    imp = ps[0]
    for p in ps[1:]:
        imp = imp + p
    o_ref[0] = jnp.concatenate(outs, axis=0).T
    n_slc = n_cmp // 2
    imp_slc = imp[0:n_slc] + imp[n_slc:n_cmp]
    t = qi * tq + lax.broadcasted_iota(jnp.int32, (n_slc, tq), 1)
    j = lax.broadcasted_iota(jnp.int32, (n_slc, tq), 0)
    cur = t // SLC_BLOCK
    forced = (j == 0) | (j == cur) | (j == cur - 1)
    score = jnp.where(forced, jnp.inf, jnp.where(j <= cur, imp_slc, -jnp.inf))
    rank = jnp.zeros((n_slc, tq), F32)
    for i in range(n_slc):
        ci = score[i:i + 1, :]
        rank = rank + jnp.where(j > i, jnp.where(ci >= score, 1.0, 0.0), jnp.where(ci > score, 1.0, 0.0))
    sel_t = jnp.where(rank < float(min(N_SELECT, n_slc)), 1.0, 0.0)
    sel_pad = jnp.concatenate([sel_t, jnp.zeros((SEL_LANES - n_slc, tq), F32)], axis=0)
    sel_ref[0, 0] = sel_pad.T


def _cmp_prompt(q, kc, vct, bias_t, gates, nb, nt):
    tq = ATT_TILE
    n_cmp = kc.shape[2]
    g = NSA_KV_HEADS
    gw = NSA_GROUP * NSA_DIM
    return pl.pallas_call(
        functools.partial(_cmp_prompt_kernel, tq=tq, n_cmp=n_cmp),
        grid=(nb, g, nt // tq),
        in_specs=[pl.BlockSpec((1, NSA_GROUP, tq, NSA_DIM), lambda b, gi, i: (b, gi, i, 0)),
                  pl.BlockSpec((1, 1, n_cmp, NSA_DIM), lambda b, gi, i: (b, gi, 0, 0)),
                  pl.BlockSpec((1, 1, NSA_DIM, n_cmp), lambda b, gi, i: (b, gi, 0, 0)),
                  pl.BlockSpec((NSA_GROUP, n_cmp, tq), lambda b, gi, i: (gi, 0, i)),
                  pl.BlockSpec((1, tq, GATE_LANES), lambda b, gi, i: (b, i, gi))],
        out_specs=(pl.BlockSpec((1, tq, gw), lambda b, gi, i: (b, i, gi)),
                   pl.BlockSpec((1, 1, tq, SEL_LANES), lambda b, gi, i: (b, gi, i, 0))),
        out_shape=(jax.ShapeDtypeStruct((nb, nt, D_MODEL), F32),
                   jax.ShapeDtypeStruct((nb, g, nt, SEL_LANES), F32)),
        compiler_params=_cparams(("arbitrary", "arbitrary", "arbitrary")), name="cmp_attn_select",
    )(q, kc, vct, bias_t, gates.reshape(nb, nt, g * GATE_LANES))


def _flash_update(s, vs, m_ref, l_ref, acc_ref):
    tk = s.shape[-1]
    m_old = m_ref[...]
    m_new = jnp.maximum(m_old, jnp.max(s, axis=-1, keepdims=True))
    alpha = jnp.exp(m_old - m_new)
    p = jnp.exp(s - jnp.concatenate([m_new] * (tk // 128), axis=1))
    psum = p[:, 0:128]
    for c in range(1, tk // 128):
        psum = psum + p[:, c * 128:(c + 1) * 128]
    l_ref[...] = alpha * l_ref[...] + psum
    dv = acc_ref.shape[-1]
    acc_ref[...] = jnp.concatenate([alpha] * (dv // 128), axis=1) * acc_ref[...] + _dot(p.astype(BF16), vs)
    m_ref[...] = m_new


def _nsa_flash_kernel(*refs, tq, slc):
    if slc:
        q_ref, k_ref, ve_ref, vo_ref, bias_ref, gate_ref, ge_ref, sel_ref, exp_ref, o_ref, m_ref, acc_ref = refs
    else:
        q_ref, k_ref, ve_ref, vo_ref, bias_ref, gate_ref, ge_ref, o_ref, m_ref, acc_ref = refs
    qi = pl.program_id(2)
    ch = FLASH_ROWS
    nsub = tq // ch
    m_ref[...] = jnp.full(m_ref.shape, NEG_BIG, F32)
    acc_ref[...] = jnp.zeros(acc_ref.shape, F32)
    if slc:
        sel = sel_ref[0, 0].astype(BF16)

    def tile(kt, btype):
        start = pl.multiple_of(kt * tq, tq)
        kt_tile = k_ref[0, 0, :, pl.ds(start, tq)]
        vs = (ve_ref[0, 0, pl.ds(start, tq), :], vo_ref[0, 0, pl.ds(start, tq), :])
        if slc:
            maskadd = (_dot(sel, exp_ref[kt]) - 1.0) * (-NEG_BIG)
        def scores(c):
            hh, qs = divmod(c, nsub)
            qrows = slice(qs * ch, (qs + 1) * ch)
            s = _dot(q_ref[0, hh, qrows, :], kt_tile)
            if btype is not None:
                s = s + bias_ref[hh, btype, qrows, :]
            if slc:
                s = s + maskadd[qrows]
            return s

        s_next = scores(0) if slc else None
        for c in range(NSA_GROUP * nsub):
            hh = c // nsub
            srows = slice(c * ch, (c + 1) * ch)
            if slc:
                s = s_next
                if c + 1 < NSA_GROUP * nsub:
                    s_next = scores(c + 1)
            else:
                s = scores(c)
            m_old = m_ref[srows]
            m_new = jnp.maximum(m_old, jnp.max(s, axis=-1, keepdims=True))
            alpha = jnp.exp(m_old - m_new)
            p = jnp.exp(s - jnp.concatenate([m_new] * (tq // 128), axis=1))
            acc_ref[srows] = alpha * acc_ref[srows] + _dot(p.astype(BF16), vs[hh % 2])
            m_ref[srows] = m_new

    if slc:
        def far(kt, carry):
            tile(kt, None)
            return carry
        lax.fori_loop(0, jnp.maximum(qi - 1, 0), far, 0)
    else:
        @pl.when(qi >= 2)
        def _():
            tile(qi - 2, 2)

    @pl.when(qi >= 1)
    def _():
        tile(qi - 1, 1)

    tile(qi, 0)
    gexp = _expand_gates(gate_ref[0], ge_ref[...])
    lane = lax.broadcasted_iota(jnp.int32, (tq, 2 * NSA_DIM), 1)
    for j in range(NSA_GROUP // 2):
        a_e = acc_ref[(2 * j) * tq:(2 * j + 1) * tq]
        a_o = acc_ref[(2 * j + 1) * tq:(2 * j + 2) * tq]
        num = jnp.where(lane < NSA_DIM, a_e, a_o)
        den = pltpu.roll(jnp.where(lane < NSA_DIM, a_o, a_e), NSA_DIM, 1)
        cols = slice(j * 2 * NSA_DIM, (j + 1) * 2 * NSA_DIM)
        o_ref[0, :, cols] = num / den * gexp[:, cols]


def _expand_gates(g, expand):
    g1 = g.astype(BF16)
    r1 = g - g1.astype(F32)
    g2 = r1.astype(BF16)
    g3 = (r1 - g2.astype(F32)).astype(BF16)
    return _dot(g1, expand) + _dot(g2, expand) + _dot(g3, expand)


def _gate_expand_matrix(branch):
    k = np.arange(GATE_LANES)[:, None]
    n = np.arange(NSA_GROUP * NSA_DIM)[None, :]
    return jnp.asarray((k == branch * NSA_GROUP + n // NSA_DIM).astype(np.float32), dtype=BF16)


def _nsa_flash(q, kt, ve, vo, bias_tiles, gates, nb, nt, sel=None, expand=None):
    tq = ATT_TILE
    g = NSA_KV_HEADS
    gw = NSA_GROUP * NSA_DIM
    slc = sel is not None
    v_spec = pl.BlockSpec((1, 1, nt, 2 * NSA_DIM), lambda b, gi, i: (b, gi, 0, 0))
    in_specs = [pl.BlockSpec((1, NSA_GROUP, tq, NSA_DIM), lambda b, gi, i: (b, gi, i, 0)),
                pl.BlockSpec((1, 1, NSA_DIM, nt), lambda b, gi, i: (b, gi, 0, 0)),
                v_spec, v_spec,
                pl.BlockSpec((NSA_GROUP, 3, tq, tq), lambda b, gi, i: (gi, 0, 0, 0)),
                pl.BlockSpec((1, tq, GATE_LANES), lambda b, gi, i: (b, i, gi)),
                pl.BlockSpec((GATE_LANES, gw), lambda b, gi, i: (0, 0))]
    args = [q, kt, ve, vo, bias_tiles, gates.reshape(nb, nt, g * GATE_LANES), _gate_expand_matrix(1 if slc else 2)]
    if slc:
        in_specs += [pl.BlockSpec((1, 1, tq, SEL_LANES), lambda b, gi, i: (b, gi, i, 0)),
                     pl.BlockSpec((nt // tq, SEL_LANES, tq), lambda b, gi, i: (0, 0, 0))]
        args += [sel, expand]
    return pl.pallas_call(
        functools.partial(_nsa_flash_kernel, tq=tq, slc=slc),
        grid=(nb, g, nt // tq),
        in_specs=in_specs,
        out_specs=pl.BlockSpec((1, tq, gw), lambda b, gi, i: (b, i, gi)),
        out_shape=jax.ShapeDtypeStruct((nb, nt, D_MODEL), F32),
        scratch_shapes=[pltpu.VMEM((NSA_GROUP * tq, 128), F32), pltpu.VMEM((NSA_GROUP * tq, 2 * NSA_DIM), F32)],
        compiler_params=_cparams(("arbitrary", "arbitrary", "arbitrary")),
        name="slc_attn" if slc else "win_attn",
    )(*args)


def _mla_prompt_kernel(q_ref, kn_ref, latv_ref, o_ref, m_ref, l_ref, acc_ref, *, tq):
    qi = pl.program_id(1)
    rows = MLA_HEADS * tq
    m_ref[...] = jnp.full(m_ref.shape, NEG_BIG, F32)
    l_ref[...] = jnp.zeros(l_ref.shape, F32)
    acc_ref[...] = jnp.zeros(acc_ref.shape, F32)

    heads_per_chunk = MLA_FLASH_ROWS // tq
    n_chunks = MLA_HEADS // heads_per_chunk

    def tile(kt, diag):
        start = pl.multiple_of(kt * tq, tq)
        vs = latv_ref[0, pl.ds(start, tq), :]

        def scores(c):
            heads = range(c * heads_per_chunk, (c + 1) * heads_per_chunk)
            s = jnp.concatenate([_dot(q_ref[0, hd], kn_ref[0, hd, :, pl.ds(start, tq)]) for hd in heads], axis=0)
            if diag:
                row = lax.broadcasted_iota(jnp.int32, (tq, tq), 0)
                col = lax.broadcasted_iota(jnp.int32, (tq, tq), 1)
                s = jnp.where((col <= row)[None], s.reshape(heads_per_chunk, tq, tq), NEG_BIG)
                s = s.reshape(MLA_FLASH_ROWS, tq)
            return s

        s_next = scores(0)
        for c in range(n_chunks):
            s_cur = s_next
            if c + 1 < n_chunks:
                s_next = scores(c + 1)
            srows = slice(c * MLA_FLASH_ROWS, (c + 1) * MLA_FLASH_ROWS)
            _flash_update(s_cur, vs, m_ref.at[srows], l_ref.at[srows], acc_ref.at[srows])

    def body(kt, carry):
        tile(kt, False)
        return carry
    lax.fori_loop(0, qi, body, 0)
    tile(qi, True)
    o = acc_ref[...] / jnp.sum(l_ref[...], axis=-1, keepdims=True)
    o_ref[0] = o.reshape(MLA_HEADS, tq, KV_LORA).astype(BF16)


def _mla_prompt(qmla, latt, latv, nb, nt):
    tq = ATT_TILE
    return pl.pallas_call(
        functools.partial(_mla_prompt_kernel, tq=tq),
        grid=(nb, nt // tq),
        in_specs=[pl.BlockSpec((1, MLA_HEADS, tq, MLA_QK), lambda b, i: (b, 0, i, 0)),
                  pl.BlockSpec((1, MLA_HEADS, MLA_QK, nt), lambda b, i: (b, 0, 0, 0)),
                  pl.BlockSpec((1, nt, KV_LORA), lambda b, i: (b, 0, 0))],
        out_specs=pl.BlockSpec((1, MLA_HEADS, tq, KV_LORA), lambda b, i: (b, 0, i, 0)),
        out_shape=jax.ShapeDtypeStruct((nb, MLA_HEADS, nt, KV_LORA), BF16),
        scratch_shapes=[pltpu.VMEM((MLA_HEADS * tq, 128), F32), pltpu.VMEM((MLA_HEADS * tq, 128), F32),
                        pltpu.VMEM((MLA_HEADS * tq, KV_LORA), F32)],
        compiler_params=_cparams(("arbitrary", "arbitrary")), name="mla_attn",
    )(qmla, latt, latv.reshape(nb, nt, KV_LORA))


def _merge_kernel(oc_ref, os_ref, ow_ref, ol_ref, gm_ref, x_ref, mod_ref, wuv_ref, wo_ref, o_ref):
    o_nsa = oc_ref[...] + os_ref[...] + ow_ref[...]
    o_mla = jnp.concatenate([_dot(ol_ref[0, hd], wuv_ref[hd]) for hd in range(MLA_HEADS)], axis=-1)
    gm = gm_ref[...]
    merged = (gm[:, 0:D_MODEL] * o_nsa + gm[:, D_MODEL:] * o_mla).astype(BF16)
    o_ref[...] = x_ref[...] + mod_ref[:, 2, :] * _dot(merged, wo_ref[...])


def _merge(o_cmp, o_slc, o_win, o_lat, gm, x, mod, per_row_mod, lw, nb, nt, tm):
    m = nb * nt
    tpb = nt // tm
    d = D_MODEL
    row = lambda i: (i, 0)
    if per_row_mod:
        mod_spec = pl.BlockSpec((tm, 6, d), lambda i: (i, 0, 0))
    else:
        mod_spec = pl.BlockSpec((1, 6, d), lambda i: (i // tpb, 0, 0))
    return pl.pallas_call(
        _merge_kernel, grid=(m // tm,),
        in_specs=[pl.BlockSpec((tm, d), row), pl.BlockSpec((tm, d), row), pl.BlockSpec((tm, d), row),
                  pl.BlockSpec((1, MLA_HEADS, tm, KV_LORA), lambda i: (i // tpb, 0, i % tpb, 0)),
                  pl.BlockSpec((tm, 2 * d), row), pl.BlockSpec((tm, d), row), mod_spec,
                  pl.BlockSpec((MLA_HEADS, KV_LORA, MLA_V), lambda i: (0, 0, 0)),
                  pl.BlockSpec((d, d), lambda i: (0, 0))],
        out_specs=pl.BlockSpec((tm, d), row),
        out_shape=jax.ShapeDtypeStruct((m, d), F32),
        compiler_params=_cparams(("arbitrary",)), name="merge_out_proj",
    )(o_cmp, o_slc, o_win, o_lat, gm, x, mod, lw["w_uv"], lw["w_out"])


def _ffn_kernel(x_ref, mod_ref, gn_ref, w1_ref, w3_ref, w2_ref, gf_ref, o_ref, h_ref, acc_ref, *, final):
    f = pl.program_id(1)

    @pl.when(f == 0)
    def _():
        y = _rms(x_ref[...], gn_ref[...])
        h_ref[...] = (y * (1.0 + mod_ref[:, 4, :]) + mod_ref[:, 3, :]).astype(BF16)
        acc_ref[...] = jnp.zeros(acc_ref.shape, F32)

    h = h_ref[...]
    a = _dot(h, w1_ref[...])
    b = _dot(h, w3_ref[...])
    act = (a * jax.nn.sigmoid(a) * b).astype(BF16)
    acc_ref[...] += _dot(act, w2_ref[...])

    @pl.when(f == pl.num_programs(1) - 1)
    def _():
        y = x_ref[...] + mod_ref[:, 5, :] * acc_ref[...]
        if final:
            y = _rms(y, gf_ref[...])
        o_ref[...] = y


def _ffn(x, mod, per_row_mod, lw, gfinal, final, nb, nt, tm):
    m = nb * nt
    tpb = nt // tm
    d = D_MODEL
    dff = lw["ffn_w2"].shape[0]
    tf = dff // 2
    nf = dff // tf
    row = lambda i, f: (i, 0)
    if per_row_mod:
        mod_spec = pl.BlockSpec((tm, 6, d), lambda i, f: (i, 0, 0))
    else:
        mod_spec = pl.BlockSpec((1, 6, d), lambda i, f: (i // tpb, 0, 0))
    return pl.pallas_call(
        functools.partial(_ffn_kernel, final=final), grid=(m // tm, nf),
        in_specs=[pl.BlockSpec((tm, d), row), mod_spec, pl.BlockSpec((1, d), lambda i, f: (0, 0)),
                  pl.BlockSpec((d, tf), lambda i, f: (0, f)),
                  pl.BlockSpec((d, tf), lambda i, f: (0, nf + f)),
                  pl.BlockSpec((tf, d), lambda i, f: (f, 0)),
                  pl.BlockSpec((1, d), lambda i, f: (0, 0))],
        out_specs=pl.BlockSpec((tm, d), row),
        out_shape=jax.ShapeDtypeStruct((m, d), F32),
        scratch_shapes=[pltpu.VMEM((tm, d), BF16), pltpu.VMEM((tm, d), F32)],
        compiler_params=_cparams(("arbitrary", "arbitrary")), name="ffn",
    )(x, mod, lw["norm_ffn_g"], lw["ffn_w13"], lw["ffn_w13"], lw["ffn_w2"], gfinal)


def _cmp_sample_kernel(q_ref, tok_ref, bias_ref, gate_ref, o_ref, imp_ref, *, nt, n_cmp):
    rows = nt * NSA_GROUP
    n_past = n_cmp // 2
    half = NSA_KV_HEADS * NSA_DIM
    tok = tok_ref[0]
    for g in range(NSA_KV_HEADS):
        qg = q_ref[0, :, g].reshape(rows, NSA_DIM)
        kc = tok[:, g * NSA_DIM:(g + 1) * NSA_DIM].astype(BF16)
        vc = tok[:, half + g * NSA_DIM:half + (g + 1) * NSA_DIM].astype(BF16)
        z = _dot_nt(qg, kc) + bias_ref[g]
        maskf = (bias_ref[g] > 0.5 * NEG_BIG).astype(F32)
        e = jnp.exp(z - jnp.max(z, axis=-1, keepdims=True)) * maskf
        p = e / jnp.maximum(jnp.sum(e, axis=-1, keepdims=True), 1e-30)
        o = _dot(p.astype(BF16), vc) * gate_ref[0, g, 0]
        o_ref[0, :, g] = o.reshape(nt, NSA_GROUP, NSA_DIM)
        imp = jnp.sum(p.reshape(nt, NSA_GROUP, n_cmp), axis=1)
        imp_ref[0, g] = imp[:, 0:n_past] + imp[:, n_past:n_cmp]


def _topk_sample_kernel(imp_ref, idx_ref, sc_ref, *, n_pick):
    imp = imp_ref[...]
    rows, n_past = imp.shape
    lane = lax.broadcasted_iota(jnp.int32, (rows, n_past), 1)
    score = jnp.where((lane == 0) | (lane == n_past - 1), jnp.inf, imp)
    sc_ref[...] = score.T
    st = sc_ref[...]
    blk = lax.broadcasted_iota(jnp.int32, (n_past, rows), 0)

    def body(i, rank):
        ci = sc_ref[pl.ds(i, 1), :]
        return rank + jnp.where(blk > i, jnp.where(ci >= st, 1.0, 0.0), jnp.where(ci > st, 1.0, 0.0))
    rank = lax.fori_loop(0, n_past, body, jnp.zeros((n_past, rows), F32))
    blk_f = blk.astype(F32)
    picked = [jnp.sum(jnp.where(rank == float(r), blk_f, 0.0), axis=0, keepdims=True) for r in range(n_pick)]
    picked.append(jnp.zeros((128 - n_pick, rows), F32))
    idx_ref[...] = jnp.concatenate(picked, axis=0).T.astype(jnp.int32)


def _topk_sample(imp2d, n_pick):
    rows, n_past = imp2d.shape
    return pl.pallas_call(
        functools.partial(_topk_sample_kernel, n_pick=n_pick),
        grid=(1,),
        in_specs=[pl.BlockSpec((rows, n_past), lambda i: (0, 0))],
        out_specs=pl.BlockSpec((rows, 128), lambda i: (0, 0)),
        out_shape=jax.ShapeDtypeStruct((rows, 128), jnp.int32),
        scratch_shapes=[pltpu.VMEM((n_past, rows), F32)],
        compiler_params=_cparams(("arbitrary",)), name="topk_blocks_sample",
    )(imp2d)


def _cmp_sample(q5, tok, bias, gate_cols, nb, nt):
    n_cmp = tok.shape[1]
    g = NSA_KV_HEADS
    rows = nt * NSA_GROUP
    return pl.pallas_call(
        functools.partial(_cmp_sample_kernel, nt=nt, n_cmp=n_cmp),
        grid=(nb,),
        in_specs=[pl.BlockSpec((1, nt, g, NSA_GROUP, NSA_DIM), lambda b: (b, 0, 0, 0, 0)),
                  pl.BlockSpec((1, n_cmp, KV_ROW), lambda b: (b, 0, 0)),
                  pl.BlockSpec((g, rows, n_cmp), lambda b: (0, 0, 0)),
                  pl.BlockSpec((1, g, 1, rows, 1), lambda b: (b, 0, 0, 0, 0))],
        out_specs=(pl.BlockSpec((1, nt, g, NSA_GROUP, NSA_DIM), lambda b: (b, 0, 0, 0, 0)),
                   pl.BlockSpec((1, g, nt, n_cmp // 2), lambda b: (b, 0, 0, 0))),
        out_shape=(jax.ShapeDtypeStruct((nb, nt, g, NSA_GROUP, NSA_DIM), F32),
                   jax.ShapeDtypeStruct((nb, g, nt, n_cmp // 2), F32)),
        compiler_params=_cparams(("arbitrary",)), name="cmp_attn_sample",
    )(q5, tok, bias, gate_cols)


def _slc_sample_kernel(idx_ref, pt_ref, *refs, nt, n_pick, n_past, n_pages):
    del pt_ref, n_pages
    nblk = NSA_KV_HEADS * n_pick
    blk_refs = refs[:nblk]
    q_ref, new_ref, bias_ref, biasc_ref, gate_ref, o_ref = refs[nblk:]
    b = pl.program_id(0)
    t = pl.program_id(1)
    half = NSA_KV_HEADS * NSA_DIM
    lane_half = lax.broadcasted_iota(jnp.int32, (NSA_GROUP, PAGE_SIZE), 1) // SLC_BLOCK
    new = new_ref[0]
    groups = range(NSA_KV_HEADS)
    s_all, sn_all, vt_all = [], [], []
    for g in groups:
        qg = q_ref[0, 0, g]
        kt_list, vt_list, bias_list = [], [], []
        for n in range(n_pick):
            page = blk_refs[g * n_pick + n]
            j = idx_ref[((b * nt + t) * NSA_KV_HEADS + g) * n_pick + n]
            near = jnp.clip(j - (n_past - 3), 0, 2)
            kt_list.append(page[0, g * NSA_DIM:(g + 1) * NSA_DIM, :].astype(BF16))
            vt_list.append(page[0, half + g * NSA_DIM:half + (g + 1) * NSA_DIM, :].astype(BF16))
            bias_list.append(jnp.where(lane_half == (j & 1), bias_ref[0, g, near], NEG_BIG))
        s_all.append(_dot(qg, jnp.concatenate(kt_list, axis=1)) + jnp.concatenate(bias_list, axis=1))
        sn_all.append(_dot_nt(qg, new[:, g * NSA_DIM:(g + 1) * NSA_DIM].astype(BF16)) + biasc_ref[0, g])
        vt_all.append(jnp.concatenate(vt_list, axis=1))
    m_all = [jnp.maximum(s_all[g].max(axis=-1, keepdims=True), sn_all[g].max(axis=-1, keepdims=True)) for g in groups]
    p_all = [jnp.exp(s_all[g] - m_all[g]) for g in groups]
    pn_all = [jnp.exp(sn_all[g] - m_all[g]) for g in groups]
    for g in groups:
        l = jnp.sum(p_all[g], axis=-1, keepdims=True) + jnp.sum(pn_all[g], axis=-1, keepdims=True)
        acc = _dot_nt(p_all[g].astype(BF16), vt_all[g])
        acc = acc + _dot(pn_all[g].astype(BF16), new[:, half + g * NSA_DIM:half + (g + 1) * NSA_DIM].astype(BF16))
        o_ref[0, 0, g] = acc / l * gate_ref[0, g, 0, 0]


def _slc_sample(idx_flat, pt_flat, cache_blocks, q5, new_rows, bias_near, bias_cur, gate_cols, nb, nt, n_pick,
                n_past, n_pages):
    g = NSA_KV_HEADS

    def blk_map(gi, n):
        def f(b, t, idx, pt):
            j = idx[((b * nt + t) * g + gi) * n_pick + n]
            return (pt[b * n_pages + (j >> 1)], 0, 0)
        return f

    in_specs = [pl.BlockSpec((1, KV_ROW, PAGE_SIZE), blk_map(gi, n)) for gi in range(g) for n in range(n_pick)]
    in_specs += [
        pl.BlockSpec((1, 1, g, NSA_GROUP, NSA_DIM), lambda b, t, idx, pt: (b, t, 0, 0, 0)),
        pl.BlockSpec((1, 8, KV_ROW), lambda b, t, idx, pt: (b, 0, 0)),
        pl.BlockSpec((1, g, 3, NSA_GROUP, PAGE_SIZE), lambda b, t, idx, pt: (t, 0, 0, 0, 0)),
        pl.BlockSpec((1, g, NSA_GROUP, 8), lambda b, t, idx, pt: (t, 0, 0, 0)),
        pl.BlockSpec((1, g, 1, 1, NSA_GROUP, 1), lambda b, t, idx, pt: (b, 0, 0, t, 0, 0)),
    ]
    return pl.pallas_call(
        functools.partial(_slc_sample_kernel, nt=nt, n_pick=n_pick, n_past=n_past, n_pages=n_pages),
        grid_spec=pltpu.PrefetchScalarGridSpec(
            num_scalar_prefetch=2, grid=(nb, nt), in_specs=in_specs,
            out_specs=pl.BlockSpec((1, 1, g, NSA_GROUP, NSA_DIM), lambda b, t, idx, pt: (b, t, 0, 0, 0))),
        out_shape=jax.ShapeDtypeStruct((nb, nt, g, NSA_GROUP, NSA_DIM), F32),
        compiler_params=_cparams(("arbitrary", "arbitrary")), name="slc_attn_sample",
    )(idx_flat, pt_flat, *([cache_blocks] * (g * n_pick)), q5, new_rows, bias_near, bias_cur, gate_cols)


def _win_sample_kernel(q_ref, buf_ref, new_ref, bias_ref, biasn_ref, gate_ref, o_ref, *, nt):
    rows = nt * NSA_GROUP
    half = NSA_KV_HEADS * NSA_DIM
    buf = buf_ref[0, 0]
    new = new_ref[0]
    for g in range(NSA_KV_HEADS):
        qg = q_ref[0, :, g].reshape(rows, NSA_DIM)
        s1 = _dot(qg, buf[g * NSA_DIM:(g + 1) * NSA_DIM, :].astype(BF16)) + bias_ref[g]
        s2 = _dot_nt(qg, new[:, g * NSA_DIM:(g + 1) * NSA_DIM].astype(BF16)) + biasn_ref[g]
        m = jnp.maximum(s1.max(axis=-1, keepdims=True), s2.max(axis=-1, keepdims=True))
        p1 = jnp.exp(s1 - m)
        p2 = jnp.exp(s2 - m)
        l = jnp.sum(p1, axis=-1, keepdims=True) + jnp.sum(p2, axis=-1, keepdims=True)
        acc = _dot_nt(p1.astype(BF16), buf[half + g * NSA_DIM:half + (g + 1) * NSA_DIM, :].astype(BF16))
        acc = acc + _dot(p2.astype(BF16), new[:, half + g * NSA_DIM:half + (g + 1) * NSA_DIM].astype(BF16))
        o_ref[0, :, g] = (acc / l * gate_ref[0, g, 0]).reshape(nt, NSA_GROUP, NSA_DIM)


def _win_sample(q5, win_state, layer, new_rows, bias_buf, bias_new, gate_cols, nb, nt):
    g = NSA_KV_HEADS
    rows = nt * NSA_GROUP
    wlen = win_state.shape[3]
    return pl.pallas_call(
        functools.partial(_win_sample_kernel, nt=nt),
        grid=(nb,),
        in_specs=[pl.BlockSpec((1, nt, g, NSA_GROUP, NSA_DIM), lambda b: (b, 0, 0, 0, 0)),
                  pl.BlockSpec((1, 1, KV_ROW, wlen), lambda b: (layer, b, 0, 0)),
                  pl.BlockSpec((1, 8, KV_ROW), lambda b: (b, 0, 0)),
                  pl.BlockSpec((g, rows, wlen), lambda b: (0, 0, 0)),
                  pl.BlockSpec((g, rows, 8), lambda b: (0, 0, 0)),
                  pl.BlockSpec((1, g, 1, rows, 1), lambda b: (b, 0, 2, 0, 0))],
        out_specs=pl.BlockSpec((1, nt, g, NSA_GROUP, NSA_DIM), lambda b: (b, 0, 0, 0, 0)),
        out_shape=jax.ShapeDtypeStruct((nb, nt, g, NSA_GROUP, NSA_DIM), F32),
        compiler_params=_cparams(("arbitrary",)), name="win_attn_sample",
    )(q5, win_state, new_rows, bias_buf, bias_new, gate_cols)


def _mla_sample_kernel(pt_ref, *refs, pages, nt):
    del pt_ref
    page_refs = refs[:pages]
    q_ref, new_ref, o_ref, m_ref, l_ref, acc_ref = refs[pages:]
    step = pl.program_id(1)
    rows = MLA_HEADS * nt

    @pl.when(step == 0)
    def _():
        m_ref[...] = jnp.full(m_ref.shape, NEG_BIG, F32)
        l_ref[...] = jnp.zeros(l_ref.shape, F32)
        acc_ref[...] = jnp.zeros(acc_ref.shape, F32)

    q = q_ref[0]
    kt = jnp.concatenate([page_refs[k][0].astype(BF16) for k in range(pages)], axis=1)
    s = _dot(q, kt)
    m_old = m_ref[...]
    m_new = jnp.maximum(m_old, s.max(axis=-1, keepdims=True))
    alpha = jnp.exp(m_old - m_new)
    p = jnp.exp(s - m_new)
    m_ref[...] = m_new
    l_ref[...] = alpha * l_ref[...] + jnp.sum(p, axis=-1, keepdims=True)
    acc_ref[...] = alpha * acc_ref[...] + _dot_nt(p.astype(BF16), kt[0:KV_LORA, :])

    @pl.when(step == pl.num_programs(1) - 1)
    def _():
        new = new_ref[0].astype(BF16)
        tq = lax.broadcasted_iota(jnp.int32, (rows, 8), 0) % nt
        tk = lax.broadcasted_iota(jnp.int32, (rows, 8), 1)
        s = jnp.where(tk <= tq, _dot_nt(q, new), NEG_BIG)
        m_o = m_ref[...]
        m_n = jnp.maximum(m_o, s.max(axis=-1, keepdims=True))
        a = jnp.exp(m_o - m_n)
        p = jnp.exp(s - m_n)
        lf = a * l_ref[...] + jnp.sum(p, axis=-1, keepdims=True)
        accf = a * acc_ref[...] + _dot(p.astype(BF16), new[:, 0:KV_LORA])
        o_ref[0] = (accf / lf).astype(BF16)


def _mla_sample(pt_flat, cache_pages, q_rows, new_rows, nb, nt, n_pages):
    pages = MLA_PAGES
    steps = n_pages // pages
    rows = MLA_HEADS * nt

    def page_map(k):
        return lambda b, s, pt: (pt[b * n_pages + s * pages + k], 0, 0)

    in_specs = [pl.BlockSpec((1, LAT, PAGE_SIZE), page_map(k)) for k in range(pages)]
    in_specs += [pl.BlockSpec((1, rows, LAT), lambda b, s, pt: (b, 0, 0)),
                 pl.BlockSpec((1, 8, LAT), lambda b, s, pt: (b, 0, 0))]
    return pl.pallas_call(
        functools.partial(_mla_sample_kernel, pages=pages, nt=nt),
        grid_spec=pltpu.PrefetchScalarGridSpec(
            num_scalar_prefetch=1, grid=(nb, steps), in_specs=in_specs,
            out_specs=pl.BlockSpec((1, rows, KV_LORA), lambda b, s, pt: (b, 0, 0)),
            scratch_shapes=[pltpu.VMEM((rows, 1), F32), pltpu.VMEM((rows, 1), F32),
                            pltpu.VMEM((rows, KV_LORA), F32)]),
        out_shape=jax.ShapeDtypeStruct((nb, rows, KV_LORA), BF16),
        compiler_params=_cparams(("arbitrary", "arbitrary")), name="mla_attn_sample",
    )(pt_flat, *([cache_pages] * pages), q_rows, new_rows)


def _pack_layer(l, w):
    d = D_MODEL
    w_in = w["w_in"][l]
    sizes = (NSA_HEADS * NSA_DIM, KV_ROW, KV_ROW, KV_ROW, 3 * NSA_HEADS, Q_LORA, KV_LORA, MLA_ROPE, 2 * d)
    offs = np.concatenate([[0], np.cumsum(sizes)])
    seg = [w_in[:, offs[i]:offs[i + 1]] for i in range(len(sizes))]
    gsrc = seg[4].reshape(d, NSA_KV_HEADS, NSA_GROUP, 3).transpose(0, 1, 3, 2).reshape(d, NSA_KV_HEADS, 3 * NSA_GROUP)
    gates = jnp.pad(gsrc, ((0, 0), (0, 0), (0, GATE_LANES - 3 * NSA_GROUP))).reshape(d, NSA_KV_HEADS * GATE_LANES)
    kr = jnp.pad(seg[7], ((0, 0), (0, 128 - MLA_ROPE)))
    w_packed = jnp.concatenate([seg[0], seg[1], seg[2], seg[3], gates, seg[5], seg[6], kr, seg[8]], axis=1)
    wuq = w["mla_w_uq"][l].reshape(Q_LORA, MLA_HEADS, MLA_NOPE + MLA_ROPE)
    wuq = jnp.concatenate([wuq[:, :, :MLA_NOPE].reshape(Q_LORA, -1), wuq[:, :, MLA_NOPE:].reshape(Q_LORA, -1)], axis=1)
    w1 = w["nsa_cmp_w1"][l].reshape(2, CMP_BLOCK, NSA_DIM, NSA_DIM)
    def diag_blocks(blocks):
        rows = []
        for i in range(2):
            for g in range(NSA_KV_HEADS):
                off = (i * NSA_KV_HEADS + g) * NSA_DIM
                pad = [(0, 0)] * (blocks[i].ndim - 1) + [(off, KV_ROW - NSA_DIM - off)]
                rows.append(jnp.pad(blocks[i], pad))
        return jnp.concatenate(rows, axis=-2)
    w1big = diag_blocks([w1[0], w1[1]])
    w2big = diag_blocks([w["nsa_cmp_w2"][l][0], w["nsa_cmp_w2"][l][1]])
    pe = w["nsa_cmp_pe"][l]
    pe_big = jnp.broadcast_to(pe.transpose(1, 0, 2)[:, :, None, :], (CMP_BLOCK, 2, NSA_KV_HEADS, NSA_DIM))
    b1big = jnp.broadcast_to(w["nsa_cmp_b1"][l][:, None, :], (2, NSA_KV_HEADS, NSA_DIM))
    return dict(
        norm_mix_g=w["norm_mix_g"][l].reshape(1, d),
        w_in=w_packed.astype(BF16),
        mla_q_norm_g=w["mla_q_norm_g"][l].reshape(1, Q_LORA),
        w_uq=wuq.astype(BF16),
        w_ukT=w["mla_w_uk"][l].transpose(1, 2, 0).astype(BF16),
        mla_kv_norm_g=w["mla_kv_norm_g"][l].reshape(1, KV_LORA),
        w_uv=w["mla_w_uv"][l].transpose(1, 0, 2).astype(BF16),
        w_out=w["w_out"][l].astype(BF16),
        norm_ffn_g=w["norm_ffn_g"][l].reshape(1, d),
        ffn_w13=w["ffn_w13"][l].astype(BF16),
        ffn_w2=w["ffn_w2"][l].astype(BF16),
        cmp_pe=pe_big.reshape(CMP_BLOCK, 1, KV_ROW),
        cmp_pe_t=jnp.tile(pe_big.reshape(CMP_BLOCK, KV_ROW).T, (1, 2 * PAGE_SIZE // CMP_BLOCK)),
        cmp_w1=w1big.astype(BF16),
        cmp_b1=b1big.reshape(1, KV_ROW),
        cmp_w2=w2big.astype(BF16),
    )


def _cmp_block_ends(n_cmp):
    order = np.concatenate([np.arange(0, n_cmp, 2), np.arange(1, n_cmp, 2)])
    return (order + 1) * CMP_BLOCK - 1


def _prompt_bias_tables(table, nt):
    tq = ATT_TILE
    i = np.arange(tq)[:, None]
    j = np.arange(tq)[None, :]
    d0, d1, d2 = i - j, tq + i - j, 2 * tq + i - j
    bucket = np.concatenate([_t5_bucket_np(d0), _t5_bucket_np(d1), _t5_bucket_np(d2)], axis=0)
    mask = np.concatenate([np.where(d0 >= 0, 0.0, NEG_BIG), np.zeros((tq, tq)),
                           np.where(d2 < WINDOW, 0.0, NEG_BIG)], axis=0).astype(np.float32)
    tiles = _expand_bias(table, bucket, mask, True).reshape(NSA_HEADS, 3, tq, tq)
    n_cmp = nt // CMP_BLOCK
    dist = np.arange(nt)[None, :] - _cmp_block_ends(n_cmp)[:, None]
    cmp_bias = _expand_bias(table, _t5_bucket_np(dist), np.where(dist >= 0, 0.0, NEG_BIG).astype(np.float32), False)
    expand = (np.arange(SEL_LANES)[None, :, None] ==
              (np.arange(nt // tq)[:, None, None] * tq + np.arange(tq)[None, None, :]) // SLC_BLOCK)
    return tiles, cmp_bias, jnp.asarray(expand.astype(np.float32), dtype=BF16)


def _sample_bias_tables(table, nt, past, wlen):
    g, hg = NSA_KV_HEADS, NSA_GROUP
    rows = nt * hg
    q_pos = past + np.arange(nt)

    def per_group(b, width):
        return b.reshape(g, hg, nt, width).transpose(0, 2, 1, 3).reshape(g, rows, width)

    n_cmp = past // CMP_BLOCK
    dist = q_pos[:, None] - _cmp_block_ends(n_cmp)[None, :]
    cmp_bias = per_group(_expand_bias(table, _t5_bucket_np(dist),
                                      np.where(dist >= 0, 0.0, NEG_BIG).astype(np.float32), False), n_cmp)
    dist = q_pos[:, None] - (past - wlen + np.arange(wlen))[None, :]
    ok = (dist >= 0) & (dist < WINDOW)
    win_bias = per_group(_expand_bias(table, _t5_bucket_np(dist), np.where(ok, 0.0, NEG_BIG).astype(np.float32),
                                      False), wlen)
    dist = q_pos[:, None] - (past + np.arange(8))[None, :]
    ok = (dist >= 0) & (dist < WINDOW) & (np.arange(8)[None, :] < nt)
    new_mask = np.where(ok, 0.0, NEG_BIG).astype(np.float32)
    win_new = per_group(_expand_bias(table, _t5_bucket_np(dist), new_mask, False), 8)
    n_past = past // SLC_BLOCK
    near = np.stack([np.full((nt, SLC_BLOCK), 10 * MAX_DISTANCE),
                     q_pos[:, None] - ((n_past - 2) * SLC_BLOCK + np.arange(SLC_BLOCK))[None, :],
                     q_pos[:, None] - ((n_past - 1) * SLC_BLOCK + np.arange(SLC_BLOCK))[None, :]], axis=1)
    near = near.reshape(nt, 3 * SLC_BLOCK)
    slc_near = _expand_bias(table, _t5_bucket_np(near), np.zeros(near.shape, np.float32), True)
    slc_near = slc_near.reshape(g, hg, nt, 3, SLC_BLOCK).transpose(2, 0, 3, 1, 4)
    slc_near = jnp.tile(slc_near, (1, 1, 1, 1, PAGE_SIZE // SLC_BLOCK))
    slc_cur = _expand_bias(table, _t5_bucket_np(dist), np.where((dist >= 0) & (np.arange(8)[None, :] < nt), 0.0,
                                                                 NEG_BIG).astype(np.float32), True)
    slc_cur = slc_cur.reshape(g, hg, nt, 8).transpose(2, 0, 1, 3)
    return cmp_bias, win_bias, win_new, slc_near, slc_cur


def kernel(x_prompt, x_sample, cache_cmp_kv, cache_slc_kv, cache_mla, state_win_kv, page_table, c_prompt, c_sample,
           rel_bias_table, ada_w, ada_b, norm_mix_g, w_in, nsa_cmp_pe, nsa_cmp_w1, nsa_cmp_b1, nsa_cmp_w2,
           mla_q_norm_g, mla_w_uq, mla_kv_norm_g, mla_w_uk, mla_w_uv, w_out, norm_ffn_g, ffn_w13, ffn_w2,
           final_norm_g):
    weights = dict(norm_mix_g=norm_mix_g, w_in=w_in, nsa_cmp_pe=nsa_cmp_pe, nsa_cmp_w1=nsa_cmp_w1,
                   nsa_cmp_b1=nsa_cmp_b1, nsa_cmp_w2=nsa_cmp_w2, mla_q_norm_g=mla_q_norm_g, mla_w_uq=mla_w_uq,
                   mla_kv_norm_g=mla_kv_norm_g, mla_w_uk=mla_w_uk, mla_w_uv=mla_w_uv, w_out=w_out,
                   norm_ffn_g=norm_ffn_g, ffn_w13=ffn_w13, ffn_w2=ffn_w2)
    depth = ada_w.shape[0]
    d = D_MODEL
    g = NSA_KV_HEADS
    pb, pt_len, _ = x_prompt.shape
    sb, st_len, _ = x_sample.shape
    n_pool = cache_cmp_kv.shape[1]
    n_pages = page_table.shape[1]
    past = n_pages * PAGE_SIZE
    wlen = state_win_kv.shape[2]
    n_past_blocks = past // SLC_BLOCK
    n_pick = N_SELECT - 1
    assert pt_len % ATT_TILE == 0 and pt_len >= WINDOW and st_len <= 8 and past >= wlen
    assert n_past_blocks > n_pick and n_pages % MLA_PAGES == 0

    layers = [_pack_layer(l, weights) for l in range(depth)]
    gfinal = final_norm_g.reshape(1, d)
    mod_all = _modulation(jnp.concatenate([c_prompt, c_sample], axis=0), ada_w, ada_b)

    tiles, cmp_bias_p, expand = _prompt_bias_tables(rel_bias_table, pt_len)
    cmp_bias_s, win_bias_s, win_new_s, slc_near_s, slc_cur_s = _sample_bias_tables(rel_bias_table, st_len, past, wlen)
    cos_p, sin_p = _rope_tables(jnp.arange(pt_len), MLA_HEADS)
    cos_s, sin_s = _rope_tables(jnp.tile(past + jnp.arange(st_len), sb), MLA_HEADS)
    prompt_pages = jnp.arange(pb * pt_len // PAGE_SIZE, dtype=jnp.int32)
    pt_flat = page_table.reshape(-1).astype(jnp.int32)

    x = x_prompt.reshape(pb * pt_len, d)
    tm = 256
    states = (jnp.zeros((depth, pb, KV_ROW, pt_len), F32), jnp.zeros((depth, pb, KV_ROW, pt_len), F32),
              jnp.zeros((depth, pb, LAT, pt_len), F32))
    win_t = []
    for l in range(depth):
        lw = layers[l]
        mod = mod_all[l, :pb].reshape(pb, 6, d)
        (q, kvc, _, _, kst, kse, kso, kwt, kwe, kwo, gates, qmla, _, latv, latt, gm, st_cmp, st_slc, st_lat,
         kvw_t) = _inproj(x, mod, False, lw, cos_p, sin_p, pb, pt_len, tm, states=states, layer=l)
        states = (st_cmp, st_slc, st_lat)
        win_t.append(kvw_t[:, :, pt_len - min(WINDOW, pt_len):])
        tok = _compress(kvc.reshape(-1, PAGE_SIZE, KV_ROW), prompt_pages, lw)
        kc, vc = _split_compressed(tok, pb)
        o_cmp, sel = _cmp_prompt(q, kc, vc.transpose(0, 1, 3, 2), cmp_bias_p, gates, pb, pt_len)
        o_slc = _nsa_flash(q, kst, kse, kso, tiles, gates, pb, pt_len, sel=sel, expand=expand)
        o_win = _nsa_flash(q, kwt, kwe, kwo, tiles, gates, pb, pt_len)
        o_lat = _mla_prompt(qmla, latt, latv, pb, pt_len)
        x = _merge(o_cmp.reshape(-1, d), o_slc.reshape(-1, d), o_win.reshape(-1, d), o_lat, gm, x, mod, False, lw,
                   pb, pt_len, tm)
        x = _ffn(x, mod, False, lw, gfinal, l == depth - 1, pb, pt_len, 512)
    y_prompt = x.reshape(pb, pt_len, d)

    def rows_first(a):
        return a.reshape(depth, pb, 2, g, NSA_DIM, a.shape[-1]).transpose(0, 1, 5, 2, 3, 4)
    st_p = (rows_first(states[0]), rows_first(states[1]), states[2].transpose(0, 1, 3, 2),
            rows_first(jnp.stack(win_t)))

    ms = sb * st_len
    x = x_sample.reshape(ms, d)
    st_s = [[], [], [], []]
    cmp_pages = cache_cmp_kv.transpose(0, 1, 3, 4, 5, 2).reshape(depth * n_pool, KV_ROW, PAGE_SIZE)
    slc_pages = cache_slc_kv.transpose(0, 1, 3, 4, 5, 2).reshape(depth * n_pool, KV_ROW, PAGE_SIZE)
    mla_pages = cache_mla.transpose(0, 1, 3, 2).reshape(depth * n_pool, LAT, PAGE_SIZE)
    win_state = state_win_kv.transpose(0, 1, 3, 4, 5, 2).reshape(depth, sb, KV_ROW, wlen)
    for l in range(depth):
        lw = layers[l]
        mod = jnp.repeat(mod_all[l, pb:], st_len, axis=0).reshape(ms, 6, d)
        (q, kvc, kvs, kvw, _, _, _, _, _, _, gates, qmla, lat, _, _, gm) = _inproj(
            x, mod, True, lw, cos_s, sin_s, 1, ms, ms)
        pt_l = pt_flat + l * n_pool
        tok = _compress_t(cmp_pages, pt_l, lw)
        n_cmp_s = tok.shape[0] // sb
        tok = tok.reshape(sb, n_cmp_s // 2, 2, KV_ROW).transpose(0, 2, 1, 3).reshape(sb, n_cmp_s, KV_ROW)
        q5 = q.reshape(g, NSA_GROUP, sb, st_len, NSA_DIM).transpose(2, 3, 0, 1, 4)
        gate_cols = gates.reshape(sb, st_len, g, GATE_LANES)[..., :3 * NSA_GROUP]
        gate_cols = gate_cols.reshape(sb, st_len, g, 3, NSA_GROUP).transpose(0, 2, 3, 1, 4)
        gate_rows = gate_cols.reshape(sb, g, 3, st_len * NSA_GROUP, 1)
        o_cmp, imp = _cmp_sample(q5, tok, cmp_bias_s, gate_rows, sb, st_len)
        idx = _topk_sample(imp.reshape(sb * g * st_len, n_past_blocks), n_pick)
        idx_flat = idx[:, :n_pick].reshape(sb, g, st_len, n_pick).transpose(0, 2, 1, 3).reshape(-1)
        pad8 = lambda a: jnp.pad(a.reshape(sb, st_len, -1), ((0, 0), (0, 8 - st_len), (0, 0)))
        o_slc = _slc_sample(idx_flat, pt_l, slc_pages, q5, pad8(kvs), slc_near_s, slc_cur_s,
                            gate_cols.reshape(sb, g, 3, st_len, NSA_GROUP, 1)[:, :, 1:2], sb, st_len, n_pick,
                            n_past_blocks, n_pages)
        o_win = _win_sample(q5, win_state, l, pad8(kvw), win_bias_s, win_new_s, gate_rows, sb, st_len)
        q_rows = qmla.reshape(MLA_HEADS, sb, st_len, LAT).transpose(1, 0, 2, 3).reshape(sb, MLA_HEADS * st_len, LAT)
        o_lat = _mla_sample(pt_l, mla_pages, q_rows, pad8(lat), sb, st_len, n_pages)
        o_lat = o_lat.reshape(sb, MLA_HEADS, st_len, KV_LORA).transpose(1, 0, 2, 3).reshape(1, MLA_HEADS, ms, KV_LORA)
        x = _merge(o_cmp.reshape(ms, d), o_slc.reshape(ms, d), o_win.reshape(ms, d), o_lat, gm, x, mod, True, lw,
                   1, ms, ms)
        x = _ffn(x, mod, True, lw, gfinal, l == depth - 1, 1, ms, ms)
        st_s[0].append(kvc.reshape(sb, st_len, 2, g, NSA_DIM))
        st_s[1].append(kvs.reshape(sb, st_len, 2, g, NSA_DIM))
        st_s[2].append(lat.reshape(sb, st_len, LAT))
        win_all = jnp.concatenate([state_win_kv[l], kvw.reshape(sb, st_len, 2, g, NSA_DIM)], axis=1)
        st_s[3].append(win_all[:, st_len:])
    y_sample = x.reshape(sb, st_len, d)

    return (y_prompt, y_sample, st_p[0], st_p[1], st_p[2], st_p[3],
            jnp.stack(st_s[0]), jnp.stack(st_s[1]), jnp.stack(st_s[2]), jnp.stack(st_s[3]))
```

```python
import functools
import math

import numpy as np
import jax
import jax.numpy as jnp
from jax import lax
from jax.experimental import pallas as pl
from jax.experimental.pallas import tpu as pltpu

F32 = jnp.float32
BF16 = jnp.bfloat16

D_MODEL = 1024
PAGE_SIZE = 128
NSA_HEADS = 16
NSA_KV_HEADS = 2
NSA_GROUP = NSA_HEADS // NSA_KV_HEADS
NSA_DIM = D_MODEL // NSA_HEADS
CMP_BLOCK = 32
SLC_BLOCK = 64
N_SELECT = 16
WINDOW = 512
MLA_HEADS = 8
MLA_NOPE = 128
MLA_ROPE = 64
MLA_V = D_MODEL // MLA_HEADS
Q_LORA = 384
KV_LORA = 256
LAT = KV_LORA + MLA_ROPE
MLA_QK = MLA_NOPE + MLA_ROPE
ROPE_THETA = 10000.0
N_BUCKETS = 32
MAX_DISTANCE = 128
NSA_SCALE = NSA_DIM ** -0.5
MLA_SCALE = (MLA_NOPE + MLA_ROPE) ** -0.5
NEG_BIG = -1e30
RMS_EPS = 1e-6
KV_ROW = 2 * NSA_KV_HEADS * NSA_DIM
GATE_LANES = 128

SEG_Q = (0, 1024)
SEG_CMP = (1024, 1280)
SEG_SLC = (1280, 1536)
SEG_WIN = (1536, 1792)
SEG_GATE = (1792, 2048)
SEG_CQ = (2048, 2432)
SEG_CKV = (2432, 2688)
SEG_KR = (2688, 2816)
SEG_GM = (2816, 4864)
D_IN_PACKED = 4864

ATT_TILE = 256
FLASH_ROWS = 128
MLA_FLASH_ROWS = 512
SEL_LANES = 128
VMEM_LIMIT = 56 * 1024 * 1024
CMP_PAGES = 64
MLA_PAGES = 64


def _cparams(sem):
    return pltpu.CompilerParams(dimension_semantics=sem, vmem_limit_bytes=VMEM_LIMIT)


def _dot(a, b):
    return jnp.dot(a, b, preferred_element_type=F32)


def _dot_nt(a, b):
    return lax.dot_general(a, b, (((1,), (1,)), ((), ())), preferred_element_type=F32)


def _rms(x, g):
    return x * lax.rsqrt(jnp.mean(x * x, axis=-1, keepdims=True) + RMS_EPS) * g


def _rope_lanes(x, cos2, sin2):
    w = x.shape[-1]
    lane = lax.broadcasted_iota(jnp.int32, x.shape, 1)
    swapped = jnp.where(lane % MLA_ROPE < MLA_ROPE // 2, pltpu.roll(x, w - MLA_ROPE // 2, 1),
                        pltpu.roll(x, MLA_ROPE // 2, 1))
    return x * cos2 + swapped * sin2


def _t5_bucket_np(dist):
    max_exact = N_BUCKETS // 2
    d = np.maximum(dist, 0)
    log_ratio = np.log(np.maximum(d, 1).astype(np.float32) / max_exact) / math.log(MAX_DISTANCE / max_exact)
    large = np.minimum(max_exact + (log_ratio * (N_BUCKETS - max_exact)).astype(np.int32), N_BUCKETS - 1)
    return np.where(d < max_exact, d, large).astype(np.int32)


def _rope_tables(pos, reps):
    half = MLA_ROPE // 2
    inv = ROPE_THETA ** (-jnp.arange(half, dtype=F32) / half)
    ang = pos.astype(F32)[:, None] * inv[None, :]
    cos, sin = jnp.cos(ang), jnp.sin(ang)
    cos2 = jnp.concatenate([cos, cos], axis=-1)
    sin2 = jnp.concatenate([-sin, sin], axis=-1)
    return jnp.tile(cos2, (1, reps)), jnp.tile(sin2, (1, reps))


def _bias_kernel(table_ref, bucket_ref, mask_ref, o_ref, *, shift):
    h = pl.program_id(0)
    bucket = bucket_ref[...]
    acc = jnp.zeros(bucket.shape, F32)
    for b in range(N_BUCKETS):
        acc = jnp.where(bucket == b, table_ref[b, h], acc)
    if shift:
        acc = acc - table_ref[N_BUCKETS - 1, h]
    o_ref[0] = acc + mask_ref[...]


def _expand_bias(table, bucket, addmask, shift):
    r, c = bucket.shape
    return pl.pallas_call(
        functools.partial(_bias_kernel, shift=shift),
        grid=(NSA_HEADS,),
        in_specs=[pl.BlockSpec(memory_space=pltpu.SMEM),
                  pl.BlockSpec((r, c), lambda h: (0, 0)),
                  pl.BlockSpec((r, c), lambda h: (0, 0))],
        out_specs=pl.BlockSpec((1, r, c), lambda h: (h, 0, 0)),
        out_shape=jax.ShapeDtypeStruct((NSA_HEADS, r, c), F32),
        compiler_params=_cparams(("arbitrary",)),
        name="bias_expand",
    )(table, jnp.asarray(bucket), jnp.asarray(addmask))


def _mod_kernel(c_ref, w_ref, b_ref, o_ref):
    c = c_ref[...]
    cond = (c * jax.nn.sigmoid(c)).astype(BF16)
    o_ref[0] = _dot(cond, w_ref[0].astype(BF16)) + b_ref[0]


def _modulation(c_all, ada_w, ada_b):
    depth, d, n = ada_w.shape
    rows = c_all.shape[0]
    tn = 1536
    return pl.pallas_call(
        _mod_kernel,
        grid=(depth, n // tn),
        in_specs=[pl.BlockSpec((rows, d), lambda l, j: (0, 0)),
                  pl.BlockSpec((1, d, tn), lambda l, j: (l, 0, j)),
                  pl.BlockSpec((1, 1, tn), lambda l, j: (l, 0, j))],
        out_specs=pl.BlockSpec((1, rows, tn), lambda l, j: (l, 0, j)),
        out_shape=jax.ShapeDtypeStruct((depth, rows, n), F32),
        compiler_params=_cparams(("arbitrary", "arbitrary")),
        name="adaln_mod",
    )(c_all, ada_w, ada_b.reshape(depth, 1, n))


def _inproj_kernel(*refs, with_states):
    (x_ref, mod_ref, gn_ref, w_ref, cos_ref, sin_ref, gq_ref, wuq_ref, wuk_ref, gkv_ref) = refs[:10]
    outs = refs[13:] if with_states else refs[10:]
    (q_ref, kvc_ref, kvs_ref, kvw_ref, kst_ref, kse_ref, kso_ref, kwt_ref, kwe_ref, kwo_ref, gate_ref,
     qmla_ref, lat_ref, latb_ref, latt_ref, gm_ref) = outs[:16]
    x = x_ref[...]
    y = _rms(x, gn_ref[...])
    h = (y * (1.0 + mod_ref[:, 1, :]) + mod_ref[:, 0, :]).astype(BF16)

    def seg(s):
        return _dot(h, w_ref[:, s[0]:s[1]])

    q = (seg(SEG_Q) * NSA_SCALE).astype(BF16)
    for hd in range(NSA_HEADS):
        q_ref[0, hd] = q[:, hd * NSA_DIM:(hd + 1) * NSA_DIM]
    kvc = seg(SEG_CMP)
    kvc_ref[...] = kvc
    half = NSA_KV_HEADS * NSA_DIM
    low = lax.broadcasted_iota(jnp.int32, (x.shape[0], half), 1) < NSA_DIM
    kv_t = {}
    for s, kv_ref, kt_ref, ve_ref, vo_ref in ((SEG_SLC, kvs_ref, kst_ref, kse_ref, kso_ref),
                                              (SEG_WIN, kvw_ref, kwt_ref, kwe_ref, kwo_ref)):
        kv = seg(s)
        kv_ref[...] = kv
        kv_t[s] = kv.T
        kt = kv_t[s][0:half].astype(BF16)
        vv = kv[:, half:2 * half]
        vr = pltpu.roll(vv, NSA_DIM, 1)
        for g in range(NSA_KV_HEADS):
            kt_ref[0, g] = kt[g * NSA_DIM:(g + 1) * NSA_DIM]
            ve_ref[0, g] = jnp.where(low, vv if g == 0 else vr, 1.0).astype(BF16)
            vo_ref[0, g] = jnp.where(low, 1.0, vr if g == 0 else vv).astype(BF16)
    gate_ref[...] = jax.nn.sigmoid(seg(SEG_GATE))
    gm_ref[...] = jax.nn.sigmoid(seg(SEG_GM))

    cos = cos_ref[...]
    sin = sin_ref[...]
    cqn = _rms(seg(SEG_CQ), gq_ref[...]).astype(BF16)
    qm = _dot(cqn, wuq_ref[...])
    nope_w = MLA_HEADS * MLA_NOPE
    qr = _rope_lanes(qm[:, nope_w:], cos, sin) * MLA_SCALE
    for hd in range(MLA_HEADS):
        qn = qm[:, hd * MLA_NOPE:(hd + 1) * MLA_NOPE]
        qrh = qr[:, hd * MLA_ROPE:(hd + 1) * MLA_ROPE].astype(BF16)
        if with_states:
            qmla_ref[0, hd, :, 0:MLA_NOPE] = (qn * MLA_SCALE).astype(BF16)
            qmla_ref[0, hd, :, MLA_NOPE:MLA_QK] = qrh
        else:
            qmla_ref[0, hd, :, 0:KV_LORA] = (_dot(qn.astype(BF16), wuk_ref[hd]) * MLA_SCALE).astype(BF16)
            qmla_ref[0, hd, :, KV_LORA:LAT] = qrh
    ckv = _rms(seg(SEG_CKV), gkv_ref[...])
    kr = _rope_lanes(seg(SEG_KR), cos[:, 0:128], sin[:, 0:128])
    lat_ref[:, 0:KV_LORA] = ckv
    lat_ref[:, KV_LORA:LAT] = kr[:, 0:MLA_ROPE]
    latb_ref[...] = ckv.astype(BF16)
    ckv_t = ckv.T
    kr_t = kr.T[0:MLA_ROPE]
    if with_states:
        ckv_tb = ckv_t.astype(BF16)
        for hd in range(MLA_HEADS):
            latt_ref[0, hd, 0:MLA_NOPE, :] = _dot(wuk_ref[hd], ckv_tb).astype(BF16)
            latt_ref[0, hd, MLA_NOPE:MLA_QK, :] = kr_t.astype(BF16)
    else:
        latt_ref[0, 0:KV_LORA, :] = ckv_t.astype(BF16)
        latt_ref[0, KV_LORA:LAT, :] = kr_t.astype(BF16)
    if with_states:
        stc_ref, sts_ref, stl_ref, stw_ref = outs[16:]
        stc_ref[0, 0] = kvc.T
        sts_ref[0, 0] = kv_t[SEG_SLC]
        stw_ref[0] = kv_t[SEG_WIN]
        stl_ref[0, 0, 0:KV_LORA, :] = ckv_t
        stl_ref[0, 0, KV_LORA:LAT, :] = kr_t


def _inproj(x, mod, per_row_mod, lw, cos_t, sin_t, nb, nt, tm, states=None, layer=0):
    m = nb * nt
    tpb = nt // tm
    d = D_MODEL
    if per_row_mod:
        mod_spec = pl.BlockSpec((tm, 6, d), lambda i: (i, 0, 0))
    else:
        mod_spec = pl.BlockSpec((1, 6, d), lambda i: (i // tpb, 0, 0))
    const2 = lambda i: (0, 0)
    row = lambda i: (i, 0)
    bt = lambda i: (i // tpb, 0, i % tpb, 0)
    btt = lambda i: (i // tpb, 0, 0, i % tpb)
    g = NSA_KV_HEADS
    prompt = states is not None
    out_shape = (
        jax.ShapeDtypeStruct((nb, NSA_HEADS, nt, NSA_DIM), BF16),
        jax.ShapeDtypeStruct((m, KV_ROW), F32),
        jax.ShapeDtypeStruct((m, KV_ROW), F32),
        jax.ShapeDtypeStruct((m, KV_ROW), F32),
        jax.ShapeDtypeStruct((nb, g, NSA_DIM, nt), BF16),
        jax.ShapeDtypeStruct((nb, g, nt, 2 * NSA_DIM), BF16),
        jax.ShapeDtypeStruct((nb, g, nt, 2 * NSA_DIM), BF16),
        jax.ShapeDtypeStruct((nb, g, NSA_DIM, nt), BF16),
        jax.ShapeDtypeStruct((nb, g, nt, 2 * NSA_DIM), BF16),
        jax.ShapeDtypeStruct((nb, g, nt, 2 * NSA_DIM), BF16),
        jax.ShapeDtypeStruct((m, g * GATE_LANES), F32),
        jax.ShapeDtypeStruct((nb, MLA_HEADS, nt, MLA_QK if prompt else LAT), BF16),
        jax.ShapeDtypeStruct((m, LAT), F32),
        jax.ShapeDtypeStruct((m, KV_LORA), BF16),
        jax.ShapeDtypeStruct((nb, MLA_HEADS, MLA_QK, nt) if prompt else (nb, LAT, nt), BF16),
        jax.ShapeDtypeStruct((m, 2 * d), F32),
    )
    kt_spec = pl.BlockSpec((1, g, NSA_DIM, tm), btt)
    v_spec = pl.BlockSpec((1, g, tm, 2 * NSA_DIM), bt)
    out_specs = (
        pl.BlockSpec((1, NSA_HEADS, tm, NSA_DIM), bt),
        pl.BlockSpec((tm, KV_ROW), row), pl.BlockSpec((tm, KV_ROW), row), pl.BlockSpec((tm, KV_ROW), row),
        kt_spec, v_spec, v_spec, kt_spec, v_spec, v_spec,
        pl.BlockSpec((tm, g * GATE_LANES), row),
        pl.BlockSpec((1, MLA_HEADS, tm, MLA_QK if prompt else LAT), bt),
        pl.BlockSpec((tm, LAT), row), pl.BlockSpec((tm, KV_LORA), row),
        pl.BlockSpec((1, MLA_HEADS, MLA_QK, tm), btt) if prompt else
        pl.BlockSpec((1, LAT, tm), lambda i: (i // tpb, 0, i % tpb)),
        pl.BlockSpec((tm, 2 * d), row),
    )
    in_specs = [
        pl.BlockSpec((tm, d), row), mod_spec, pl.BlockSpec((1, d), const2),
        pl.BlockSpec((d, D_IN_PACKED), const2),
        pl.BlockSpec((tm, MLA_HEADS * MLA_ROPE), lambda i: (i % tpb, 0)),
        pl.BlockSpec((tm, MLA_HEADS * MLA_ROPE), lambda i: (i % tpb, 0)),
        pl.BlockSpec((1, Q_LORA), const2),
        pl.BlockSpec((Q_LORA, MLA_HEADS * (MLA_NOPE + MLA_ROPE)), const2),
        pl.BlockSpec((MLA_HEADS, MLA_NOPE, KV_LORA), lambda i: (0, 0, 0)),
        pl.BlockSpec((1, KV_LORA), const2),
    ]
    args = [x, mod, lw["norm_mix_g"], lw["w_in"], cos_t, sin_t, lw["mla_q_norm_g"], lw["w_uq"], lw["w_ukT"],
            lw["mla_kv_norm_g"]]
    aliases = {}
    if states is not None:
        n_in, n_out = len(args), len(out_shape)
        st_block = lambda width: pl.BlockSpec((1, 1, width, tm), lambda i: (layer, i // tpb, 0, i % tpb))
        in_specs += [pl.BlockSpec(memory_space=pl.ANY)] * 3
        args += list(states)
        out_shape += tuple(jax.ShapeDtypeStruct(s.shape, s.dtype) for s in states)
        out_shape += (jax.ShapeDtypeStruct((nb, KV_ROW, nt), F32),)
        out_specs += (st_block(KV_ROW), st_block(KV_ROW), st_block(LAT),
                      pl.BlockSpec((1, KV_ROW, tm), lambda i: (i // tpb, 0, i % tpb)))
        aliases = {n_in + k: n_out + k for k in range(3)}
    return pl.pallas_call(
        functools.partial(_inproj_kernel, with_states=states is not None),
        grid=(m // tm,), in_specs=in_specs, out_specs=out_specs, out_shape=out_shape,
        input_output_aliases=aliases,
        compiler_params=_cparams(("arbitrary",)), name="in_proj",
    )(*args)


def _gelu_tanh(z):
    return 0.5 * z * (1.0 + jnp.tanh(math.sqrt(2.0 / math.pi) * (z + 0.044715 * (z * z * z))))


def _compress_t_kernel(tab_ref, *refs, pages):
    del tab_ref
    page_refs = refs[:pages]
    pet_ref, sel_ref, w1_ref, b1_ref, w2_ref, o_ref, slab_ref = refs[pages:]
    pairs = pages // 2
    rows_per_pair = 2 * (PAGE_SIZE // CMP_BLOCK)
    sel = sel_ref[...]
    pet = pet_ref[...]
    for pr in range(pairs):
        xt2 = jnp.concatenate([page_refs[2 * pr][0], page_refs[2 * pr + 1][0]], axis=1)
        slab_ref[pr] = _dot_nt(sel, (xt2 + pet).astype(BF16))
    acc = jnp.zeros((pairs * rows_per_pair, KV_ROW), F32)
    for r in range(CMP_BLOCK):
        xr = jnp.concatenate([slab_ref[pr, r * rows_per_pair:(r + 1) * rows_per_pair, :] for pr in range(pairs)],
                             axis=0)
        acc = acc + _dot(xr.astype(BF16), w1_ref[r])
    hmid = _gelu_tanh(acc + b1_ref[...])
    o_ref[0] = _dot(hmid.astype(BF16), w2_ref[...])


def _compress_t(pages3d, page_ids, lw):
    n_logical = page_ids.shape[0]
    pages = min(CMP_PAGES, n_logical)
    assert n_logical % pages == 0 and pages % 2 == 0
    steps = n_logical // pages
    per_page = PAGE_SIZE // CMP_BLOCK

    def page_map(k):
        return lambda s, tab: (tab[s * pages + k], 0, 0)

    in_specs = [pl.BlockSpec((1, KV_ROW, PAGE_SIZE), page_map(k)) for k in range(pages)]
    in_specs += [
        pl.BlockSpec((KV_ROW, 2 * PAGE_SIZE), lambda s, tab: (0, 0)),
        pl.BlockSpec((2 * PAGE_SIZE, 2 * PAGE_SIZE), lambda s, tab: (0, 0)),
        pl.BlockSpec((CMP_BLOCK, KV_ROW, KV_ROW), lambda s, tab: (0, 0, 0)),
        pl.BlockSpec((1, KV_ROW), lambda s, tab: (0, 0)),
        pl.BlockSpec((KV_ROW, KV_ROW), lambda s, tab: (0, 0)),
    ]
    r, pg, c = np.meshgrid(np.arange(CMP_BLOCK), np.arange(2), np.arange(per_page), indexing="ij")
    src = (pg * PAGE_SIZE + c * CMP_BLOCK + r).reshape(-1)
    sel = jnp.asarray((src[:, None] == np.arange(2 * PAGE_SIZE)[None, :]).astype(np.float32), dtype=BF16)
    out = pl.pallas_call(
        functools.partial(_compress_t_kernel, pages=pages),
        grid_spec=pltpu.PrefetchScalarGridSpec(
            num_scalar_prefetch=1, grid=(steps,), in_specs=in_specs,
            out_specs=pl.BlockSpec((1, per_page * pages, KV_ROW), lambda s, tab: (s, 0, 0)),
            scratch_shapes=[pltpu.VMEM((pages // 2, 2 * PAGE_SIZE, KV_ROW), F32)]),
        out_shape=jax.ShapeDtypeStruct((steps, per_page * pages, KV_ROW), F32),
        compiler_params=_cparams(("arbitrary",)), name="cmp_compress_paged",
    )(page_ids, *([pages3d] * pages), lw["cmp_pe_t"], sel, lw["cmp_w1"], lw["cmp_b1"], lw["cmp_w2"])
    return out.reshape(n_logical * per_page, KV_ROW)


def _compress_kernel(tab_ref, *refs, pages):
    del tab_ref
    page_refs = refs[:pages]
    pe_ref, w1_ref, b1_ref, w2_ref, o_ref, slabk_ref, slabv_ref = refs[pages:]
    half = KV_ROW // 2
    for k in range(pages):
        slabk_ref[k * PAGE_SIZE:(k + 1) * PAGE_SIZE, :] = page_refs[k][0, :, 0:half]
        slabv_ref[k * PAGE_SIZE:(k + 1) * PAGE_SIZE, :] = page_refs[k][0, :, half:KV_ROW]
    per_page = PAGE_SIZE // CMP_BLOCK
    acc = jnp.zeros((per_page * pages, KV_ROW), F32)
    for r in range(CMP_BLOCK):
        rows = [jnp.concatenate([slab[pl.ds(c * CMP_BLOCK + r, pages, stride=PAGE_SIZE), :]
                                 for slab in (slabk_ref, slabv_ref)], axis=1) for c in range(per_page)]
        xr = (jnp.concatenate(rows, axis=0) + pe_ref[r]).astype(BF16)
        acc = acc + _dot(xr, w1_ref[r])
    hmid = _gelu_tanh(acc + b1_ref[...])
    o_ref[0] = _dot(hmid.astype(BF16), w2_ref[...])


def _compress(rows3d, page_ids, lw):
    n_logical = page_ids.shape[0]
    pages = min(CMP_PAGES, n_logical)
    assert n_logical % pages == 0
    steps = n_logical // pages
    per_page = PAGE_SIZE // CMP_BLOCK

    def page_map(k):
        return lambda s, tab: (tab[s * pages + k], 0, 0)

    in_specs = [pl.BlockSpec((1, PAGE_SIZE, KV_ROW), page_map(k)) for k in range(pages)]
    in_specs += [
        pl.BlockSpec((CMP_BLOCK, 1, KV_ROW), lambda s, tab: (0, 0, 0)),
        pl.BlockSpec((CMP_BLOCK, KV_ROW, KV_ROW), lambda s, tab: (0, 0, 0)),
        pl.BlockSpec((1, KV_ROW), lambda s, tab: (0, 0)),
        pl.BlockSpec((KV_ROW, KV_ROW), lambda s, tab: (0, 0)),
    ]
    out = pl.pallas_call(
        functools.partial(_compress_kernel, pages=pages),
        grid_spec=pltpu.PrefetchScalarGridSpec(
            num_scalar_prefetch=1, grid=(steps,), in_specs=in_specs,
            out_specs=pl.BlockSpec((1, per_page * pages, KV_ROW), lambda s, tab: (s, 0, 0)),
            scratch_shapes=[pltpu.VMEM((pages * PAGE_SIZE, KV_ROW // 2), F32),
                            pltpu.VMEM((pages * PAGE_SIZE, KV_ROW // 2), F32)]),
        out_shape=jax.ShapeDtypeStruct((steps, per_page * pages, KV_ROW), F32),
        compiler_params=_cparams(("arbitrary",)), name="cmp_compress",
    )(page_ids, *([rows3d] * pages), lw["cmp_pe"], lw["cmp_w1"], lw["cmp_b1"], lw["cmp_w2"])
    out = out.reshape(steps, per_page, pages, KV_ROW).transpose(0, 2, 1, 3)
    return out.reshape(n_logical * per_page, KV_ROW)


def _split_compressed(tok, nb):
    n = tok.shape[0] // nb
    t = tok.reshape(nb, n // 2, 2, 2, NSA_KV_HEADS, NSA_DIM)
    t = t.transpose(3, 0, 4, 2, 1, 5).reshape(2, nb, NSA_KV_HEADS, n, NSA_DIM).astype(BF16)
    return t[0], t[1]


def _cmp_prompt_kernel(q_ref, kc_ref, vct_ref, bias_ref, gate_ref, o_ref, sel_ref, *, tq, n_cmp):
    qi = pl.program_id(2)
    kc = kc_ref[0, 0]
    vct = vct_ref[0, 0]
    maskf = (bias_ref[0] > 0.5 * NEG_BIG).astype(F32)
    gates_t = gate_ref[0].T
    zs = [_dot_nt(kc, q_ref[0, hh]) + bias_ref[hh] for hh in range(NSA_GROUP)]
    es = [jnp.exp(z - jnp.max(z, axis=0, keepdims=True)) * maskf for z in zs]
    ps = [e / jnp.maximum(jnp.sum(e, axis=0, keepdims=True), 1e-30) for e in es]
    outs = [_dot(vct, ps[hh].astype(BF16)) * gates_t[hh:hh + 1, :] for hh in range(NSA_GROUP)]
    imp = ps[0]
    for p in ps[1:]:
        imp = imp + p
    o_ref[0] = jnp.concatenate(outs, axis=0).T
    n_slc = n_cmp // 2
    imp_slc = imp[0:n_slc] + imp[n_slc:n_cmp]
    t = qi * tq + lax.broadcasted_iota(jnp.int32, (n_slc, tq), 1)
    j = lax.broadcasted_iota(jnp.int32, (n_slc, tq), 0)
    cur = t // SLC_BLOCK
    forced = (j == 0) | (j == cur) | (j == cur - 1)
    score = jnp.where(forced, jnp.inf, jnp.where(j <= cur, imp_slc, -jnp.inf))
    rank = jnp.zeros((n_slc, tq), F32)
    for i in range(n_slc):
        ci = score[i:i + 1, :]
        rank = rank + jnp.where(j > i, jnp.where(ci >= score, 1.0, 0.0), jnp.where(ci > score, 1.0, 0.0))
    sel_t = jnp.where(rank < float(min(N_SELECT, n_slc)), 1.0, 0.0)
    sel_pad = jnp.concatenate([sel_t, jnp.zeros((SEL_LANES - n_slc, tq), F32)], axis=0)
    sel_ref[0, 0] = sel_pad.T


def _cmp_prompt(q, kc, vct, bias_t, gates, nb, nt):
    tq = ATT_TILE
    n_cmp = kc.shape[2]
    g = NSA_KV_HEADS
    gw = NSA_GROUP * NSA_DIM
    return pl.pallas_call(
        functools.partial(_cmp_prompt_kernel, tq=tq, n_cmp=n_cmp),
        grid=(nb, g, nt // tq),
        in_specs=[pl.BlockSpec((1, NSA_GROUP, tq, NSA_DIM), lambda b, gi, i: (b, gi, i, 0)),
                  pl.BlockSpec((1, 1, n_cmp, NSA_DIM), lambda b, gi, i: (b, gi, 0, 0)),
                  pl.BlockSpec((1, 1, NSA_DIM, n_cmp), lambda b, gi, i: (b, gi, 0, 0)),
                  pl.BlockSpec((NSA_GROUP, n_cmp, tq), lambda b, gi, i: (gi, 0, i)),
                  pl.BlockSpec((1, tq, GATE_LANES), lambda b, gi, i: (b, i, gi))],
        out_specs=(pl.BlockSpec((1, tq, gw), lambda b, gi, i: (b, i, gi)),
                   pl.BlockSpec((1, 1, tq, SEL_LANES), lambda b, gi, i: (b, gi, i, 0))),
        out_shape=(jax.ShapeDtypeStruct((nb, nt, D_MODEL), F32),
                   jax.ShapeDtypeStruct((nb, g, nt, SEL_LANES), F32)),
        compiler_params=_cparams(("arbitrary", "arbitrary", "arbitrary")), name="cmp_attn_select",
    )(q, kc, vct, bias_t, gates.reshape(nb, nt, g * GATE_LANES))


def _flash_update(s, vs, m_ref, l_ref, acc_ref):
    tk = s.shape[-1]
    m_old = m_ref[...]
    m_new = jnp.maximum(m_old, jnp.max(s, axis=-1, keepdims=True))
    alpha = jnp.exp(m_old - m_new)
    p = jnp.exp(s - jnp.concatenate([m_new] * (tk // 128), axis=1))
    psum = p[:, 0:128]
    for c in range(1, tk // 128):
        psum = psum + p[:, c * 128:(c + 1) * 128]
    l_ref[...] = alpha * l_ref[...] + psum
    dv = acc_ref.shape[-1]
    acc_ref[...] = jnp.concatenate([alpha] * (dv // 128), axis=1) * acc_ref[...] + _dot(p.astype(BF16), vs)
    m_ref[...] = m_new


def _nsa_flash_kernel(*refs, tq, slc):
    if slc:
        q_ref, k_ref, ve_ref, vo_ref, bias_ref, gate_ref, ge_ref, sel_ref, exp_ref, o_ref, m_ref, acc_ref = refs
    else:
        q_ref, k_ref, ve_ref, vo_ref, bias_ref, gate_ref, ge_ref, o_ref, m_ref, acc_ref = refs
    qi = pl.program_id(2)
    ch = FLASH_ROWS
    nsub = tq // ch
    m_ref[...] = jnp.full(m_ref.shape, NEG_BIG, F32)
    acc_ref[...] = jnp.zeros(acc_ref.shape, F32)
    if slc:
        sel = sel_ref[0, 0].astype(BF16)

    def tile(kt, btype):
        start = pl.multiple_of(kt * tq, tq)
        kt_tile = k_ref[0, 0, :, pl.ds(start, tq)]
        vs = (ve_ref[0, 0, pl.ds(start, tq), :], vo_ref[0, 0, pl.ds(start, tq), :])
        if slc:
            maskadd = (_dot(sel, exp_ref[kt]) - 1.0) * (-NEG_BIG)
        def scores(c):
            hh, qs = divmod(c, nsub)
            qrows = slice(qs * ch, (qs + 1) * ch)
            s = _dot(q_ref[0, hh, qrows, :], kt_tile)
            if btype is not None:
                s = s + bias_ref[hh, btype, qrows, :]
            if slc:
                s = s + maskadd[qrows]
            return s

        s_next = scores(0) if slc else None
        for c in range(NSA_GROUP * nsub):
            hh = c // nsub
            srows = slice(c * ch, (c + 1) * ch)
            if slc:
                s = s_next
                if c + 1 < NSA_GROUP * nsub:
                    s_next = scores(c + 1)
            else:
                s = scores(c)
            m_old = m_ref[srows]
            m_new = jnp.maximum(m_old, jnp.max(s, axis=-1, keepdims=True))
            alpha = jnp.exp(m_old - m_new)
            p = jnp.exp(s - jnp.concatenate([m_new] * (tq // 128), axis=1))
            acc_ref[srows] = alpha * acc_ref[srows] + _dot(p.astype(BF16), vs[hh % 2])
            m_ref[srows] = m_new

    if slc:
        def far(kt, carry):
            tile(kt, None)
            return carry
        lax.fori_loop(0, jnp.maximum(qi - 1, 0), far, 0)
    else:
        @pl.when(qi >= 2)
        def _():
            tile(qi - 2, 2)

    @pl.when(qi >= 1)
    def _():
        tile(qi - 1, 1)

    tile(qi, 0)
    gexp = _expand_gates(gate_ref[0], ge_ref[...])
    lane = lax.broadcasted_iota(jnp.int32, (tq, 2 * NSA_DIM), 1)
    for j in range(NSA_GROUP // 2):
        a_e = acc_ref[(2 * j) * tq:(2 * j + 1) * tq]
        a_o = acc_ref[(2 * j + 1) * tq:(2 * j + 2) * tq]
        num = jnp.where(lane < NSA_DIM, a_e, a_o)
        den = pltpu.roll(jnp.where(lane < NSA_DIM, a_o, a_e), NSA_DIM, 1)
        cols = slice(j * 2 * NSA_DIM, (j + 1) * 2 * NSA_DIM)
        o_ref[0, :, cols] = num / den * gexp[:, cols]


def _expand_gates(g, expand):
    g1 = g.astype(BF16)
    r1 = g - g1.astype(F32)
    g2 = r1.astype(BF16)
    g3 = (r1 - g2.astype(F32)).astype(BF16)
    return _dot(g1, expand) + _dot(g2, expand) + _dot(g3, expand)


def _gate_expand_matrix(branch):
    k = np.arange(GATE_LANES)[:, None]
    n = np.arange(NSA_GROUP * NSA_DIM)[None, :]
    return jnp.asarray((k == branch * NSA_GROUP + n // NSA_DIM).astype(np.float32), dtype=BF16)


def _nsa_flash(q, kt, ve, vo, bias_tiles, gates, nb, nt, sel=None, expand=None):
    tq = ATT_TILE
    g = NSA_KV_HEADS
    gw = NSA_GROUP * NSA_DIM
    slc = sel is not None
    v_spec = pl.BlockSpec((1, 1, nt, 2 * NSA_DIM), lambda b, gi, i: (b, gi, 0, 0))
    in_specs = [pl.BlockSpec((1, NSA_GROUP, tq, NSA_DIM), lambda b, gi, i: (b, gi, i, 0)),
                pl.BlockSpec((1, 1, NSA_DIM, nt), lambda b, gi, i: (b, gi, 0, 0)),
                v_spec, v_spec,
                pl.BlockSpec((NSA_GROUP, 3, tq, tq), lambda b, gi, i: (gi, 0, 0, 0)),
                pl.BlockSpec((1, tq, GATE_LANES), lambda b, gi, i: (b, i, gi)),
                pl.BlockSpec((GATE_LANES, gw), lambda b, gi, i: (0, 0))]
    args = [q, kt, ve, vo, bias_tiles, gates.reshape(nb, nt, g * GATE_LANES), _gate_expand_matrix(1 if slc else 2)]
    if slc:
        in_specs += [pl.BlockSpec((1, 1, tq, SEL_LANES), lambda b, gi, i: (b, gi, i, 0)),
                     pl.BlockSpec((nt // tq, SEL_LANES, tq), lambda b, gi, i: (0, 0, 0))]
        args += [sel, expand]
    return pl.pallas_call(
        functools.partial(_nsa_flash_kernel, tq=tq, slc=slc),
        grid=(nb, g, nt // tq),
        in_specs=in_specs,
        out_specs=pl.BlockSpec((1, tq, gw), lambda b, gi, i: (b, i, gi)),
        out_shape=jax.ShapeDtypeStruct((nb, nt, D_MODEL), F32),
        scratch_shapes=[pltpu.VMEM((NSA_GROUP * tq, 128), F32), pltpu.VMEM((NSA_GROUP * tq, 2 * NSA_DIM), F32)],
        compiler_params=_cparams(("arbitrary", "arbitrary", "arbitrary")),
        name="slc_attn" if slc else "win_attn",
    )(*args)


def _mla_prompt_kernel(q_ref, kn_ref, latv_ref, o_ref, m_ref, l_ref, acc_ref, *, tq):
    qi = pl.program_id(1)
    rows = MLA_HEADS * tq
    m_ref[...] = jnp.full(m_ref.shape, NEG_BIG, F32)
    l_ref[...] = jnp.zeros(l_ref.shape, F32)
    acc_ref[...] = jnp.zeros(acc_ref.shape, F32)

    heads_per_chunk = MLA_FLASH_ROWS // tq
    n_chunks = MLA_HEADS // heads_per_chunk

    def tile(kt, diag):
        start = pl.multiple_of(kt * tq, tq)
        vs = latv_ref[0, pl.ds(start, tq), :]

        def scores(c):
            heads = range(c * heads_per_chunk, (c + 1) * heads_per_chunk)
            s = jnp.concatenate([_dot(q_ref[0, hd], kn_ref[0, hd, :, pl.ds(start, tq)]) for hd in heads], axis=0)
            if diag:
                row = lax.broadcasted_iota(jnp.int32, (tq, tq), 0)
                col = lax.broadcasted_iota(jnp.int32, (tq, tq), 1)
                s = jnp.where((col <= row)[None], s.reshape(heads_per_chunk, tq, tq), NEG_BIG)
                s = s.reshape(MLA_FLASH_ROWS, tq)
            return s

        s_next = scores(0)
        for c in range(n_chunks):
            s_cur = s_next
            if c + 1 < n_chunks:
                s_next = scores(c + 1)
            srows = slice(c * MLA_FLASH_ROWS, (c + 1) * MLA_FLASH_ROWS)
            _flash_update(s_cur, vs, m_ref.at[srows], l_ref.at[srows], acc_ref.at[srows])

    def body(kt, carry):
        tile(kt, False)
        return carry
    lax.fori_loop(0, qi, body, 0)
    tile(qi, True)
    o = acc_ref[...] / jnp.sum(l_ref[...], axis=-1, keepdims=True)
    o_ref[0] = o.reshape(MLA_HEADS, tq, KV_LORA).astype(BF16)


def _mla_prompt(qmla, latt, latv, nb, nt):
    tq = ATT_TILE
    return pl.pallas_call(
        functools.partial(_mla_prompt_kernel, tq=tq),
        grid=(nb, nt // tq),
        in_specs=[pl.BlockSpec((1, MLA_HEADS, tq, MLA_QK), lambda b, i: (b, 0, i, 0)),
                  pl.BlockSpec((1, MLA_HEADS, MLA_QK, nt), lambda b, i: (b, 0, 0, 0)),
                  pl.BlockSpec((1, nt, KV_LORA), lambda b, i: (b, 0, 0))],
        out_specs=pl.BlockSpec((1, MLA_HEADS, tq, KV_LORA), lambda b, i: (b, 0, i, 0)),
        out_shape=jax.ShapeDtypeStruct((nb, MLA_HEADS, nt, KV_LORA), BF16),
        scratch_shapes=[pltpu.VMEM((MLA_HEADS * tq, 128), F32), pltpu.VMEM((MLA_HEADS * tq, 128), F32),
                        pltpu.VMEM((MLA_HEADS * tq, KV_LORA), F32)],
        compiler_params=_cparams(("arbitrary", "arbitrary")), name="mla_attn",
    )(qmla, latt, latv.reshape(nb, nt, KV_LORA))


def _merge_kernel(oc_ref, os_ref, ow_ref, ol_ref, gm_ref, x_ref, mod_ref, wuv_ref, wo_ref, o_ref):
    o_nsa = oc_ref[...] + os_ref[...] + ow_ref[...]
    o_mla = jnp.concatenate([_dot(ol_ref[0, hd], wuv_ref[hd]) for hd in range(MLA_HEADS)], axis=-1)
    gm = gm_ref[...]
    merged = (gm[:, 0:D_MODEL] * o_nsa + gm[:, D_MODEL:] * o_mla).astype(BF16)
    o_ref[...] = x_ref[...] + mod_ref[:, 2, :] * _dot(merged, wo_ref[...])


def _merge(o_cmp, o_slc, o_win, o_lat, gm, x, mod, per_row_mod, lw, nb, nt, tm):
    m = nb * nt
    tpb = nt // tm
    d = D_MODEL
    row = lambda i: (i, 0)
    if per_row_mod:
        mod_spec = pl.BlockSpec((tm, 6, d), lambda i: (i, 0, 0))
    else:
        mod_spec = pl.BlockSpec((1, 6, d), lambda i: (i // tpb, 0, 0))
    return pl.pallas_call(
        _merge_kernel, grid=(m // tm,),
        in_specs=[pl.BlockSpec((tm, d), row), pl.BlockSpec((tm, d), row), pl.BlockSpec((tm, d), row),
                  pl.BlockSpec((1, MLA_HEADS, tm, KV_LORA), lambda i: (i // tpb, 0, i % tpb, 0)),
                  pl.BlockSpec((tm, 2 * d), row), pl.BlockSpec((tm, d), row), mod_spec,
                  pl.BlockSpec((MLA_HEADS, KV_LORA, MLA_V), lambda i: (0, 0, 0)),
                  pl.BlockSpec((d, d), lambda i: (0, 0))],
        out_specs=pl.BlockSpec((tm, d), row),
        out_shape=jax.ShapeDtypeStruct((m, d), F32),
        compiler_params=_cparams(("arbitrary",)), name="merge_out_proj",
    )(o_cmp, o_slc, o_win, o_lat, gm, x, mod, lw["w_uv"], lw["w_out"])


def _ffn_kernel(x_ref, mod_ref, gn_ref, w1_ref, w3_ref, w2_ref, gf_ref, o_ref, h_ref, acc_ref, *, final):
    f = pl.program_id(1)

    @pl.when(f == 0)
    def _():
        y = _rms(x_ref[...], gn_ref[...])
        h_ref[...] = (y * (1.0 + mod_ref[:, 4, :]) + mod_ref[:, 3, :]).astype(BF16)
        acc_ref[...] = jnp.zeros(acc_ref.shape, F32)

    h = h_ref[...]
    a = _dot(h, w1_ref[...])
    b = _dot(h, w3_ref[...])
    act = (a * jax.nn.sigmoid(a) * b).astype(BF16)
    acc_ref[...] += _dot(act, w2_ref[...])

    @pl.when(f == pl.num_programs(1) - 1)
    def _():
        y = x_ref[...] + mod_ref[:, 5, :] * acc_ref[...]
        if final:
            y = _rms(y, gf_ref[...])
        o_ref[...] = y


def _ffn(x, mod, per_row_mod, lw, gfinal, final, nb, nt, tm):
    m = nb * nt
    tpb = nt // tm
    d = D_MODEL
    dff = lw["ffn_w2"].shape[0]
    tf = dff // 2
    nf = dff // tf
    row = lambda i, f: (i, 0)
    if per_row_mod:
        mod_spec = pl.BlockSpec((tm, 6, d), lambda i, f: (i, 0, 0))
    else:
        mod_spec = pl.BlockSpec((1, 6, d), lambda i, f: (i // tpb, 0, 0))
    return pl.pallas_call(
        functools.partial(_ffn_kernel, final=final), grid=(m // tm, nf),
        in_specs=[pl.BlockSpec((tm, d), row), mod_spec, pl.BlockSpec((1, d), lambda i, f: (0, 0)),
                  pl.BlockSpec((d, tf), lambda i, f: (0, f)),
                  pl.BlockSpec((d, tf), lambda i, f: (0, nf + f)),
                  pl.BlockSpec((tf, d), lambda i, f: (f, 0)),
                  pl.BlockSpec((1, d), lambda i, f: (0, 0))],
        out_specs=pl.BlockSpec((tm, d), row),
        out_shape=jax.ShapeDtypeStruct((m, d), F32),
        scratch_shapes=[pltpu.VMEM((tm, d), BF16), pltpu.VMEM((tm, d), F32)],
        compiler_params=_cparams(("arbitrary", "arbitrary")), name="ffn",
    )(x, mod, lw["norm_ffn_g"], lw["ffn_w13"], lw["ffn_w13"], lw["ffn_w2"], gfinal)


def _cmp_sample_kernel(q_ref, tok_ref, bias_ref, gate_ref, o_ref, imp_ref, *, nt, n_cmp):
    rows = nt * NSA_GROUP
    n_past = n_cmp // 2
    half = NSA_KV_HEADS * NSA_DIM
    tok = tok_ref[0]
    for g in range(NSA_KV_HEADS):
        qg = q_ref[0, :, g].reshape(rows, NSA_DIM)
        kc = tok[:, g * NSA_DIM:(g + 1) * NSA_DIM].astype(BF16)
        vc = tok[:, half + g * NSA_DIM:half + (g + 1) * NSA_DIM].astype(BF16)
        z = _dot_nt(qg, kc) + bias_ref[g]
        maskf = (bias_ref[g] > 0.5 * NEG_BIG).astype(F32)
        e = jnp.exp(z - jnp.max(z, axis=-1, keepdims=True)) * maskf
        p = e / jnp.maximum(jnp.sum(e, axis=-1, keepdims=True), 1e-30)
        o = _dot(p.astype(BF16), vc) * gate_ref[0, g, 0]
        o_ref[0, :, g] = o.reshape(nt, NSA_GROUP, NSA_DIM)
        imp = jnp.sum(p.reshape(nt, NSA_GROUP, n_cmp), axis=1)
        imp_ref[0, g] = imp[:, 0:n_past] + imp[:, n_past:n_cmp]


def _topk_sample_kernel(imp_ref, idx_ref, sc_ref, *, n_pick):
    imp = imp_ref[...]
    rows, n_past = imp.shape
    lane = lax.broadcasted_iota(jnp.int32, (rows, n_past), 1)
    score = jnp.where((lane == 0) | (lane == n_past - 1), jnp.inf, imp)
    sc_ref[...] = score.T
    st = sc_ref[...]
    blk = lax.broadcasted_iota(jnp.int32, (n_past, rows), 0)

    def body(i, rank):
        ci = sc_ref[pl.ds(i, 1), :]
        return rank + jnp.where(blk > i, jnp.where(ci >= st, 1.0, 0.0), jnp.where(ci > st, 1.0, 0.0))
    rank = lax.fori_loop(0, n_past, body, jnp.zeros((n_past, rows), F32))
    blk_f = blk.astype(F32)
    picked = [jnp.sum(jnp.where(rank == float(r), blk_f, 0.0), axis=0, keepdims=True) for r in range(n_pick)]
    picked.append(jnp.zeros((128 - n_pick, rows), F32))
    idx_ref[...] = jnp.concatenate(picked, axis=0).T.astype(jnp.int32)


def _topk_sample(imp2d, n_pick):
    rows, n_past = imp2d.shape
    return pl.pallas_call(
        functools.partial(_topk_sample_kernel, n_pick=n_pick),
        grid=(1,),
        in_specs=[pl.BlockSpec((rows, n_past), lambda i: (0, 0))],
        out_specs=pl.BlockSpec((rows, 128), lambda i: (0, 0)),
        out_shape=jax.ShapeDtypeStruct((rows, 128), jnp.int32),
        scratch_shapes=[pltpu.VMEM((n_past, rows), F32)],
        compiler_params=_cparams(("arbitrary",)), name="topk_blocks_sample",
    )(imp2d)


def _cmp_sample(q5, tok, bias, gate_cols, nb, nt):
    n_cmp = tok.shape[1]
    g = NSA_KV_HEADS
    rows = nt * NSA_GROUP
    return pl.pallas_call(
        functools.partial(_cmp_sample_kernel, nt=nt, n_cmp=n_cmp),
        grid=(nb,),
        in_specs=[pl.BlockSpec((1, nt, g, NSA_GROUP, NSA_DIM), lambda b: (b, 0, 0, 0, 0)),
                  pl.BlockSpec((1, n_cmp, KV_ROW), lambda b: (b, 0, 0)),
                  pl.BlockSpec((g, rows, n_cmp), lambda b: (0, 0, 0)),
                  pl.BlockSpec((1, g, 1, rows, 1), lambda b: (b, 0, 0, 0, 0))],
        out_specs=(pl.BlockSpec((1, nt, g, NSA_GROUP, NSA_DIM), lambda b: (b, 0, 0, 0, 0)),
                   pl.BlockSpec((1, g, nt, n_cmp // 2), lambda b: (b, 0, 0, 0))),
        out_shape=(jax.ShapeDtypeStruct((nb, nt, g, NSA_GROUP, NSA_DIM), F32),
                   jax.ShapeDtypeStruct((nb, g, nt, n_cmp // 2), F32)),
        compiler_params=_cparams(("arbitrary",)), name="cmp_attn_sample",
    )(q5, tok, bias, gate_cols)


def _slc_sample_kernel(idx_ref, pt_ref, *refs, nt, n_pick, n_past, n_pages):
    del pt_ref, n_pages
    nblk = NSA_KV_HEADS * n_pick
    blk_refs = refs[:nblk]
    q_ref, new_ref, bias_ref, biasc_ref, gate_ref, o_ref = refs[nblk:]
    b = pl.program_id(0)
    t = pl.program_id(1)
    half = NSA_KV_HEADS * NSA_DIM
    lane_half = lax.broadcasted_iota(jnp.int32, (NSA_GROUP, PAGE_SIZE), 1) // SLC_BLOCK
    new = new_ref[0]
    groups = range(NSA_KV_HEADS)
    s_all, sn_all, vt_all = [], [], []
    for g in groups:
        qg = q_ref[0, 0, g]
        kt_list, vt_list, bias_list = [], [], []
        for n in range(n_pick):
            page = blk_refs[g * n_pick + n]
            j = idx_ref[((b * nt + t) * NSA_KV_HEADS + g) * n_pick + n]
            near = jnp.clip(j - (n_past - 3), 0, 2)
            kt_list.append(page[0, g * NSA_DIM:(g + 1) * NSA_DIM, :].astype(BF16))
            vt_list.append(page[0, half + g * NSA_DIM:half + (g + 1) * NSA_DIM, :].astype(BF16))
            bias_list.append(jnp.where(lane_half == (j & 1), bias_ref[0, g, near], NEG_BIG))
        s_all.append(_dot(qg, jnp.concatenate(kt_list, axis=1)) + jnp.concatenate(bias_list, axis=1))
        sn_all.append(_dot_nt(qg, new[:, g * NSA_DIM:(g + 1) * NSA_DIM].astype(BF16)) + biasc_ref[0, g])
        vt_all.append(jnp.concatenate(vt_list, axis=1))
    m_all = [jnp.maximum(s_all[g].max(axis=-1, keepdims=True), sn_all[g].max(axis=-1, keepdims=True)) for g in groups]
    p_all = [jnp.exp(s_all[g] - m_all[g]) for g in groups]
    pn_all = [jnp.exp(sn_all[g] - m_all[g]) for g in groups]
    for g in groups:
        l = jnp.sum(p_all[g], axis=-1, keepdims=True) + jnp.sum(pn_all[g], axis=-1, keepdims=True)
        acc = _dot_nt(p_all[g].astype(BF16), vt_all[g])
        acc = acc + _dot(pn_all[g].astype(BF16), new[:, half + g * NSA_DIM:half + (g + 1) * NSA_DIM].astype(BF16))
        o_ref[0, 0, g] = acc / l * gate_ref[0, g, 0, 0]


def _slc_sample(idx_flat, pt_flat, cache_blocks, q5, new_rows, bias_near, bias_cur, gate_cols, nb, nt, n_pick,
                n_past, n_pages):
    g = NSA_KV_HEADS

    def blk_map(gi, n):
        def f(b, t, idx, pt):
            j = idx[((b * nt + t) * g + gi) * n_pick + n]
            return (pt[b * n_pages + (j >> 1)], 0, 0)
        return f

    in_specs = [pl.BlockSpec((1, KV_ROW, PAGE_SIZE), blk_map(gi, n)) for gi in range(g) for n in range(n_pick)]
    in_specs += [
        pl.BlockSpec((1, 1, g, NSA_GROUP, NSA_DIM), lambda b, t, idx, pt: (b, t, 0, 0, 0)),
        pl.BlockSpec((1, 8, KV_ROW), lambda b, t, idx, pt: (b, 0, 0)),
        pl.BlockSpec((1, g, 3, NSA_GROUP, PAGE_SIZE), lambda b, t, idx, pt: (t, 0, 0, 0, 0)),
        pl.BlockSpec((1, g, NSA_GROUP, 8), lambda b, t, idx, pt: (t, 0, 0, 0)),
        pl.BlockSpec((1, g, 1, 1, NSA_GROUP, 1), lambda b, t, idx, pt: (b, 0, 0, t, 0, 0)),
    ]
    return pl.pallas_call(
        functools.partial(_slc_sample_kernel, nt=nt, n_pick=n_pick, n_past=n_past, n_pages=n_pages),
        grid_spec=pltpu.PrefetchScalarGridSpec(
            num_scalar_prefetch=2, grid=(nb, nt), in_specs=in_specs,
            out_specs=pl.BlockSpec((1, 1, g, NSA_GROUP, NSA_DIM), lambda b, t, idx, pt: (b, t, 0, 0, 0))),
        out_shape=jax.ShapeDtypeStruct((nb, nt, g, NSA_GROUP, NSA_DIM), F32),
        compiler_params=_cparams(("arbitrary", "arbitrary")), name="slc_attn_sample",
    )(idx_flat, pt_flat, *([cache_blocks] * (g * n_pick)), q5, new_rows, bias_near, bias_cur, gate_cols)


def _win_sample_kernel(q_ref, buf_ref, new_ref, bias_ref, biasn_ref, gate_ref, o_ref, *, nt):
    rows = nt * NSA_GROUP
    half = NSA_KV_HEADS * NSA_DIM
    buf = buf_ref[0, 0]
    new = new_ref[0]
    for g in range(NSA_KV_HEADS):
        qg = q_ref[0, :, g].reshape(rows, NSA_DIM)
        s1 = _dot(qg, buf[g * NSA_DIM:(g + 1) * NSA_DIM, :].astype(BF16)) + bias_ref[g]
        s2 = _dot_nt(qg, new[:, g * NSA_DIM:(g + 1) * NSA_DIM].astype(BF16)) + biasn_ref[g]
        m = jnp.maximum(s1.max(axis=-1, keepdims=True), s2.max(axis=-1, keepdims=True))
        p1 = jnp.exp(s1 - m)
        p2 = jnp.exp(s2 - m)
        l = jnp.sum(p1, axis=-1, keepdims=True) + jnp.sum(p2, axis=-1, keepdims=True)
        acc = _dot_nt(p1.astype(BF16), buf[half + g * NSA_DIM:half + (g + 1) * NSA_DIM, :].astype(BF16))
        acc = acc + _dot(p2.astype(BF16), new[:, half + g * NSA_DIM:half + (g + 1) * NSA_DIM].astype(BF16))
        o_ref[0, :, g] = (acc / l * gate_ref[0, g, 0]).reshape(nt, NSA_GROUP, NSA_DIM)


def _win_sample(q5, win_state, layer, new_rows, bias_buf, bias_new, gate_cols, nb, nt):
    g = NSA_KV_HEADS
    rows = nt * NSA_GROUP
    wlen = win_state.shape[3]
    return pl.pallas_call(
        functools.partial(_win_sample_kernel, nt=nt),
        grid=(nb,),
        in_specs=[pl.BlockSpec((1, nt, g, NSA_GROUP, NSA_DIM), lambda b: (b, 0, 0, 0, 0)),
                  pl.BlockSpec((1, 1, KV_ROW, wlen), lambda b: (layer, b, 0, 0)),
                  pl.BlockSpec((1, 8, KV_ROW), lambda b: (b, 0, 0)),
                  pl.BlockSpec((g, rows, wlen), lambda b: (0, 0, 0)),
                  pl.BlockSpec((g, rows, 8), lambda b: (0, 0, 0)),
                  pl.BlockSpec((1, g, 1, rows, 1), lambda b: (b, 0, 2, 0, 0))],
        out_specs=pl.BlockSpec((1, nt, g, NSA_GROUP, NSA_DIM), lambda b: (b, 0, 0, 0, 0)),
        out_shape=jax.ShapeDtypeStruct((nb, nt, g, NSA_GROUP, NSA_DIM), F32),
        compiler_params=_cparams(("arbitrary",)), name="win_attn_sample",
    )(q5, win_state, new_rows, bias_buf, bias_new, gate_cols)


def _mla_sample_kernel(pt_ref, *refs, pages, nt):
    del pt_ref
    page_refs = refs[:pages]
    q_ref, new_ref, o_ref, m_ref, l_ref, acc_ref = refs[pages:]
    step = pl.program_id(1)
    rows = MLA_HEADS * nt

    @pl.when(step == 0)
    def _():
        m_ref[...] = jnp.full(m_ref.shape, NEG_BIG, F32)
        l_ref[...] = jnp.zeros(l_ref.shape, F32)
        acc_ref[...] = jnp.zeros(acc_ref.shape, F32)

    q = q_ref[0]
    kt = jnp.concatenate([page_refs[k][0].astype(BF16) for k in range(pages)], axis=1)
    s = _dot(q, kt)
    m_old = m_ref[...]
    m_new = jnp.maximum(m_old, s.max(axis=-1, keepdims=True))
    alpha = jnp.exp(m_old - m_new)
    p = jnp.exp(s - m_new)
    m_ref[...] = m_new
    l_ref[...] = alpha * l_ref[...] + jnp.sum(p, axis=-1, keepdims=True)
    acc_ref[...] = alpha * acc_ref[...] + _dot_nt(p.astype(BF16), kt[0:KV_LORA, :])

    @pl.when(step == pl.num_programs(1) - 1)
    def _():
        new = new_ref[0].astype(BF16)
        tq = lax.broadcasted_iota(jnp.int32, (rows, 8), 0) % nt
        tk = lax.broadcasted_iota(jnp.int32, (rows, 8), 1)
        s = jnp.where(tk <= tq, _dot_nt(q, new), NEG_BIG)
        m_o = m_ref[...]
        m_n = jnp.maximum(m_o, s.max(axis=-1, keepdims=True))
        a = jnp.exp(m_o - m_n)
        p = jnp.exp(s - m_n)
        lf = a * l_ref[...] + jnp.sum(p, axis=-1, keepdims=True)
        accf = a * acc_ref[...] + _dot(p.astype(BF16), new[:, 0:KV_LORA])
        o_ref[0] = (accf / lf).astype(BF16)


def _mla_sample(pt_flat, cache_pages, q_rows, new_rows, nb, nt, n_pages):
    pages = min(MLA_PAGES, n_pages)
    steps = n_pages // pages
    rows = MLA_HEADS * nt

    def page_map(k):
        return lambda b, s, pt: (pt[b * n_pages + s * pages + k], 0, 0)

    in_specs = [pl.BlockSpec((1, LAT, PAGE_SIZE), page_map(k)) for k in range(pages)]
    in_specs += [pl.BlockSpec((1, rows, LAT), lambda b, s, pt: (b, 0, 0)),
                 pl.BlockSpec((1, 8, LAT), lambda b, s, pt: (b, 0, 0))]
    return pl.pallas_call(
        functools.partial(_mla_sample_kernel, pages=pages, nt=nt),
        grid_spec=pltpu.PrefetchScalarGridSpec(
            num_scalar_prefetch=1, grid=(nb, steps), in_specs=in_specs,
            out_specs=pl.BlockSpec((1, rows, KV_LORA), lambda b, s, pt: (b, 0, 0)),
            scratch_shapes=[pltpu.VMEM((rows, 1), F32), pltpu.VMEM((rows, 1), F32),
                            pltpu.VMEM((rows, KV_LORA), F32)]),
        out_shape=jax.ShapeDtypeStruct((nb, rows, KV_LORA), BF16),
        compiler_params=_cparams(("arbitrary", "arbitrary")), name="mla_attn_sample",
    )(pt_flat, *([cache_pages] * pages), q_rows, new_rows)


def _pack_layer(l, w):
    d = D_MODEL
    w_in = w["w_in"][l]
    sizes = (NSA_HEADS * NSA_DIM, KV_ROW, KV_ROW, KV_ROW, 3 * NSA_HEADS, Q_LORA, KV_LORA, MLA_ROPE, 2 * d)
    offs = np.concatenate([[0], np.cumsum(sizes)])
    seg = [w_in[:, offs[i]:offs[i + 1]] for i in range(len(sizes))]
    gsrc = seg[4].reshape(d, NSA_KV_HEADS, NSA_GROUP, 3).transpose(0, 1, 3, 2).reshape(d, NSA_KV_HEADS, 3 * NSA_GROUP)
    gates = jnp.pad(gsrc, ((0, 0), (0, 0), (0, GATE_LANES - 3 * NSA_GROUP))).reshape(d, NSA_KV_HEADS * GATE_LANES)
    kr = jnp.pad(seg[7], ((0, 0), (0, 128 - MLA_ROPE)))
    w_packed = jnp.concatenate([seg[0], seg[1], seg[2], seg[3], gates, seg[5], seg[6], kr, seg[8]], axis=1)
    wuq = w["mla_w_uq"][l].reshape(Q_LORA, MLA_HEADS, MLA_NOPE + MLA_ROPE)
    wuq = jnp.concatenate([wuq[:, :, :MLA_NOPE].reshape(Q_LORA, -1), wuq[:, :, MLA_NOPE:].reshape(Q_LORA, -1)], axis=1)
    w1 = w["nsa_cmp_w1"][l].reshape(2, CMP_BLOCK, NSA_DIM, NSA_DIM)
    def diag_blocks(blocks):
        rows = []
        for i in range(2):
            for g in range(NSA_KV_HEADS):
                off = (i * NSA_KV_HEADS + g) * NSA_DIM
                pad = [(0, 0)] * (blocks[i].ndim - 1) + [(off, KV_ROW - NSA_DIM - off)]
                rows.append(jnp.pad(blocks[i], pad))
        return jnp.concatenate(rows, axis=-2)
    w1big = diag_blocks([w1[0], w1[1]])
    w2big = diag_blocks([w["nsa_cmp_w2"][l][0], w["nsa_cmp_w2"][l][1]])
    pe = w["nsa_cmp_pe"][l]
    pe_big = jnp.broadcast_to(pe.transpose(1, 0, 2)[:, :, None, :], (CMP_BLOCK, 2, NSA_KV_HEADS, NSA_DIM))
    b1big = jnp.broadcast_to(w["nsa_cmp_b1"][l][:, None, :], (2, NSA_KV_HEADS, NSA_DIM))
    return dict(
        norm_mix_g=w["norm_mix_g"][l].reshape(1, d),
        w_in=w_packed.astype(BF16),
        mla_q_norm_g=w["mla_q_norm_g"][l].reshape(1, Q_LORA),
        w_uq=wuq.astype(BF16),
        w_ukT=w["mla_w_uk"][l].transpose(1, 2, 0).astype(BF16),
        mla_kv_norm_g=w["mla_kv_norm_g"][l].reshape(1, KV_LORA),
        w_uv=w["mla_w_uv"][l].transpose(1, 0, 2).astype(BF16),
        w_out=w["w_out"][l].astype(BF16),
        norm_ffn_g=w["norm_ffn_g"][l].reshape(1, d),
        ffn_w13=w["ffn_w13"][l].astype(BF16),
        ffn_w2=w["ffn_w2"][l].astype(BF16),
        cmp_pe=pe_big.reshape(CMP_BLOCK, 1, KV_ROW),
        cmp_pe_t=jnp.tile(pe_big.reshape(CMP_BLOCK, KV_ROW).T, (1, 2 * PAGE_SIZE // CMP_BLOCK)),
        cmp_w1=w1big.astype(BF16),
        cmp_b1=b1big.reshape(1, KV_ROW),
        cmp_w2=w2big.astype(BF16),
    )


def _cmp_block_ends(n_cmp):
    order = np.concatenate([np.arange(0, n_cmp, 2), np.arange(1, n_cmp, 2)])
    return (order + 1) * CMP_BLOCK - 1


def _prompt_bias_tables(table, nt):
    tq = ATT_TILE
    i = np.arange(tq)[:, None]
    j = np.arange(tq)[None, :]
    d0, d1, d2 = i - j, tq + i - j, 2 * tq + i - j
    bucket = np.concatenate([_t5_bucket_np(d0), _t5_bucket_np(d1), _t5_bucket_np(d2)], axis=0)
    mask = np.concatenate([np.where(d0 >= 0, 0.0, NEG_BIG), np.zeros((tq, tq)),
                           np.where(d2 < WINDOW, 0.0, NEG_BIG)], axis=0).astype(np.float32)
    tiles = _expand_bias(table, bucket, mask, True).reshape(NSA_HEADS, 3, tq, tq)
    n_cmp = nt // CMP_BLOCK
    dist = np.arange(nt)[None, :] - _cmp_block_ends(n_cmp)[:, None]
    cmp_bias = _expand_bias(table, _t5_bucket_np(dist), np.where(dist >= 0, 0.0, NEG_BIG).astype(np.float32), False)
    expand = (np.arange(SEL_LANES)[None, :, None] ==
              (np.arange(nt // tq)[:, None, None] * tq + np.arange(tq)[None, None, :]) // SLC_BLOCK)
    return tiles, cmp_bias, jnp.asarray(expand.astype(np.float32), dtype=BF16)


def _sample_bias_tables(table, nt, past, wlen):
    g, hg = NSA_KV_HEADS, NSA_GROUP
    rows = nt * hg
    q_pos = past + np.arange(nt)

    def per_group(b, width):
        return b.reshape(g, hg, nt, width).transpose(0, 2, 1, 3).reshape(g, rows, width)

    n_cmp = past // CMP_BLOCK
    dist = q_pos[:, None] - _cmp_block_ends(n_cmp)[None, :]
    cmp_bias = per_group(_expand_bias(table, _t5_bucket_np(dist),
                                      np.where(dist >= 0, 0.0, NEG_BIG).astype(np.float32), False), n_cmp)
    dist = q_pos[:, None] - (past - wlen + np.arange(wlen))[None, :]
    ok = (dist >= 0) & (dist < WINDOW)
    win_bias = per_group(_expand_bias(table, _t5_bucket_np(dist), np.where(ok, 0.0, NEG_BIG).astype(np.float32),
                                      False), wlen)
    dist = q_pos[:, None] - (past + np.arange(8))[None, :]
    ok = (dist >= 0) & (dist < WINDOW) & (np.arange(8)[None, :] < nt)
    new_mask = np.where(ok, 0.0, NEG_BIG).astype(np.float32)
    win_new = per_group(_expand_bias(table, _t5_bucket_np(dist), new_mask, False), 8)
    n_past = past // SLC_BLOCK
    near = np.stack([np.full((nt, SLC_BLOCK), 10 * MAX_DISTANCE),
                     q_pos[:, None] - ((n_past - 2) * SLC_BLOCK + np.arange(SLC_BLOCK))[None, :],
                     q_pos[:, None] - ((n_past - 1) * SLC_BLOCK + np.arange(SLC_BLOCK))[None, :]], axis=1)
    near = near.reshape(nt, 3 * SLC_BLOCK)
    slc_near = _expand_bias(table, _t5_bucket_np(near), np.zeros(near.shape, np.float32), True)
    slc_near = slc_near.reshape(g, hg, nt, 3, SLC_BLOCK).transpose(2, 0, 3, 1, 4)
    slc_near = jnp.tile(slc_near, (1, 1, 1, 1, PAGE_SIZE // SLC_BLOCK))
    slc_cur = _expand_bias(table, _t5_bucket_np(dist), np.where((dist >= 0) & (np.arange(8)[None, :] < nt), 0.0,
                                                                 NEG_BIG).astype(np.float32), True)
    slc_cur = slc_cur.reshape(g, hg, nt, 8).transpose(2, 0, 1, 3)
    return cmp_bias, win_bias, win_new, slc_near, slc_cur


def kernel(x_prompt, x_sample, cache_cmp_kv, cache_slc_kv, cache_mla, state_win_kv, page_table, c_prompt, c_sample,
           rel_bias_table, ada_w, ada_b, norm_mix_g, w_in, nsa_cmp_pe, nsa_cmp_w1, nsa_cmp_b1, nsa_cmp_w2,
           mla_q_norm_g, mla_w_uq, mla_kv_norm_g, mla_w_uk, mla_w_uv, w_out, norm_ffn_g, ffn_w13, ffn_w2,
           final_norm_g):
    weights = dict(norm_mix_g=norm_mix_g, w_in=w_in, nsa_cmp_pe=nsa_cmp_pe, nsa_cmp_w1=nsa_cmp_w1,
                   nsa_cmp_b1=nsa_cmp_b1, nsa_cmp_w2=nsa_cmp_w2, mla_q_norm_g=mla_q_norm_g, mla_w_uq=mla_w_uq,
                   mla_kv_norm_g=mla_kv_norm_g, mla_w_uk=mla_w_uk, mla_w_uv=mla_w_uv, w_out=w_out,
                   norm_ffn_g=norm_ffn_g, ffn_w13=ffn_w13, ffn_w2=ffn_w2)
    depth = ada_w.shape[0]
    d = D_MODEL
    g = NSA_KV_HEADS
    pb, pt_len, _ = x_prompt.shape
    sb, st_len, _ = x_sample.shape
    n_pool = cache_cmp_kv.shape[1]
    n_pages = page_table.shape[1]
    past = n_pages * PAGE_SIZE
    wlen = state_win_kv.shape[2]
    n_past_blocks = past // SLC_BLOCK
    n_pick = N_SELECT - 1
    assert pt_len % ATT_TILE == 0 and pt_len >= WINDOW and st_len <= 8 and past >= wlen
    assert n_past_blocks > n_pick and n_pages % min(MLA_PAGES, n_pages) == 0

    layers = [_pack_layer(l, weights) for l in range(depth)]
    gfinal = final_norm_g.reshape(1, d)
    mod_all = _modulation(jnp.concatenate([c_prompt, c_sample], axis=0), ada_w, ada_b)

    tiles, cmp_bias_p, expand = _prompt_bias_tables(rel_bias_table, pt_len)
    cmp_bias_s, win_bias_s, win_new_s, slc_near_s, slc_cur_s = _sample_bias_tables(rel_bias_table, st_len, past, wlen)
    cos_p, sin_p = _rope_tables(jnp.arange(pt_len), MLA_HEADS)
    cos_s, sin_s = _rope_tables(jnp.tile(past + jnp.arange(st_len), sb), MLA_HEADS)
    prompt_pages = jnp.arange(pb * pt_len // PAGE_SIZE, dtype=jnp.int32)
    pt_flat = page_table.reshape(-1).astype(jnp.int32)

    x = x_prompt.reshape(pb * pt_len, d)
    tm = 256
    states = (jnp.zeros((depth, pb, KV_ROW, pt_len), F32), jnp.zeros((depth, pb, KV_ROW, pt_len), F32),
              jnp.zeros((depth, pb, LAT, pt_len), F32))
    win_t = []
    for l in range(depth):
        lw = layers[l]
        mod = mod_all[l, :pb].reshape(pb, 6, d)
        (q, kvc, _, _, kst, kse, kso, kwt, kwe, kwo, gates, qmla, _, latv, latt, gm, st_cmp, st_slc, st_lat,
         kvw_t) = _inproj(x, mod, False, lw, cos_p, sin_p, pb, pt_len, tm, states=states, layer=l)
        states = (st_cmp, st_slc, st_lat)
        win_t.append(kvw_t[:, :, pt_len - min(WINDOW, pt_len):])
        tok = _compress(kvc.reshape(-1, PAGE_SIZE, KV_ROW), prompt_pages, lw)
        kc, vc = _split_compressed(tok, pb)
        o_cmp, sel = _cmp_prompt(q, kc, vc.transpose(0, 1, 3, 2), cmp_bias_p, gates, pb, pt_len)
        o_slc = _nsa_flash(q, kst, kse, kso, tiles, gates, pb, pt_len, sel=sel, expand=expand)
        o_win = _nsa_flash(q, kwt, kwe, kwo, tiles, gates, pb, pt_len)
        o_lat = _mla_prompt(qmla, latt, latv, pb, pt_len)
        x = _merge(o_cmp.reshape(-1, d), o_slc.reshape(-1, d), o_win.reshape(-1, d), o_lat, gm, x, mod, False, lw,
                   pb, pt_len, tm)
        x = _ffn(x, mod, False, lw, gfinal, l == depth - 1, pb, pt_len, 512)
    y_prompt = x.reshape(pb, pt_len, d)

    def rows_first(a):
        return a.reshape(depth, pb, 2, g, NSA_DIM, a.shape[-1]).transpose(0, 1, 5, 2, 3, 4)
    st_p = (rows_first(states[0]), rows_first(states[1]), states[2].transpose(0, 1, 3, 2),
            rows_first(jnp.stack(win_t)))

    ms = sb * st_len
    x = x_sample.reshape(ms, d)
    st_s = [[], [], [], []]
    cmp_pages = cache_cmp_kv.transpose(0, 1, 3, 4, 5, 2).reshape(depth * n_pool, KV_ROW, PAGE_SIZE)
    slc_pages = cache_slc_kv.transpose(0, 1, 3, 4, 5, 2).reshape(depth * n_pool, KV_ROW, PAGE_SIZE)
    mla_pages = cache_mla.transpose(0, 1, 3, 2).reshape(depth * n_pool, LAT, PAGE_SIZE)
    win_state = state_win_kv.transpose(0, 1, 3, 4, 5, 2).reshape(depth, sb, KV_ROW, wlen)
    for l in range(depth):
        lw = layers[l]
        mod = jnp.repeat(mod_all[l, pb:], st_len, axis=0).reshape(ms, 6, d)
        (q, kvc, kvs, kvw, _, _, _, _, _, _, gates, qmla, lat, _, _, gm) = _inproj(
            x, mod, True, lw, cos_s, sin_s, 1, ms, ms)
        pt_l = pt_flat + l * n_pool
        tok = _compress_t(cmp_pages, pt_l, lw)
        n_cmp_s = tok.shape[0] // sb
        tok = tok.reshape(sb, n_cmp_s // 2, 2, KV_ROW).transpose(0, 2, 1, 3).reshape(sb, n_cmp_s, KV_ROW)
        q5 = q.reshape(g, NSA_GROUP, sb, st_len, NSA_DIM).transpose(2, 3, 0, 1, 4)
        gate_cols = gates.reshape(sb, st_len, g, GATE_LANES)[..., :3 * NSA_GROUP]
        gate_cols = gate_cols.reshape(sb, st_len, g, 3, NSA_GROUP).transpose(0, 2, 3, 1, 4)
        gate_rows = gate_cols.reshape(sb, g, 3, st_len * NSA_GROUP, 1)
        o_cmp, imp = _cmp_sample(q5, tok, cmp_bias_s, gate_rows, sb, st_len)
        idx = _topk_sample(imp.reshape(sb * g * st_len, n_past_blocks), n_pick)
        idx_flat = idx[:, :n_pick].reshape(sb, g, st_len, n_pick).transpose(0, 2, 1, 3).reshape(-1)
        pad8 = lambda a: jnp.pad(a.reshape(sb, st_len, -1), ((0, 0), (0, 8 - st_len), (0, 0)))
        o_slc = _slc_sample(idx_flat, pt_l, slc_pages, q5, pad8(kvs), slc_near_s, slc_cur_s,
                            gate_cols.reshape(sb, g, 3, st_len, NSA_GROUP, 1)[:, :, 1:2], sb, st_len, n_pick,
                            n_past_blocks, n_pages)
        o_win = _win_sample(q5, win_state, l, pad8(kvw), win_bias_s, win_new_s, gate_rows, sb, st_len)
        q_rows = qmla.reshape(MLA_HEADS, sb, st_len, LAT).transpose(1, 0, 2, 3).reshape(sb, MLA_HEADS * st_len, LAT)
        o_lat = _mla_sample(pt_l, mla_pages, q_rows, pad8(lat), sb, st_len, n_pages)
        o_lat = o_lat.reshape(sb, MLA_HEADS, st_len, KV_LORA).transpose(1, 0, 2, 3).reshape(1, MLA_HEADS, ms, KV_LORA)
        x = _merge(o_cmp.reshape(ms, d), o_slc.reshape(ms, d), o_win.reshape(ms, d), o_lat, gm, x, mod, True, lw,
                   1, ms, ms)
        x = _ffn(x, mod, True, lw, gfinal, l == depth - 1, 1, ms, ms)
        st_s[0].append(kvc.reshape(sb, st_len, 2, g, NSA_DIM))
        st_s[1].append(kvs.reshape(sb, st_len, 2, g, NSA_DIM))
        st_s[2].append(lat.reshape(sb, st_len, LAT))
        win_all = jnp.concatenate([state_win_kv[l], kvw.reshape(sb, st_len, 2, g, NSA_DIM)], axis=1)
        st_s[3].append(win_all[:, st_len:])
    y_sample = x.reshape(sb, st_len, d)

    return (y_prompt, y_sample, st_p[0], st_p[1], st_p[2], st_p[3],
            jnp.stack(st_s[0]), jnp.stack(st_s[1]), jnp.stack(st_s[2]), jnp.stack(st_s[3]))
```
